```python
import math
import jax, jax.numpy as jnp
from jax import lax
import numpy as np

D_MODEL = 1024
BATCH = 32
SEQ = 256
DEPTH = 4
DEC_BATCH = 2
DEC_SEQ = 1024
PAST_LEN = 256

GRID_W = 64
N_MIXERS = 3
N_A_LAYERS = len(range(0, DEPTH, N_MIXERS))
N_B_LAYERS = len(range(1, DEPTH, N_MIXERS))
N_C_LAYERS = len(range(2, DEPTH, N_MIXERS))
N_MOD = 9
EPS = 1e-6
D_FF = 2816
D_RNN = D_MODEL
N_LRU_BLOCKS = 16
LRU_BLOCK = D_RNN // N_LRU_BLOCKS
CONV_W = 4
CONV_LEFT = 2
LRU_C = 8.0
N_DIFF_HEADS = 8
DIFF_HD = D_MODEL // N_DIFF_HEADS // 2
ROPE_THETA = 10000.0
Q_BLOCK = 128
POOL_WINDOWS = (2, 4, 8, 16)
N_POOL_GROUPS = len(POOL_WINDOWS)
POOL_GROUP = D_MODEL // N_POOL_GROUPS

kernel_name = "hybrid_diffusion_macaron_rglru_diffattn_pool_step"

F32 = jnp.float32


def rmsnorm(x, g):
    xf = x.astype(F32)
    y = xf * lax.rsqrt(jnp.mean(xf * xf, axis=-1, keepdims=True) + EPS)
    return (y * g.astype(F32)).astype(x.dtype)


def adaln(cond, w_mod, b_mod):
    m = jnp.einsum('nd,de->ne', jax.nn.silu(cond), w_mod) + b_mod
    return m.reshape(cond.shape[0], N_MOD, D_MODEL)


def pre(x, g, mod, k):
    return rmsnorm(x, g) * (1 + mod[:, 3 * k + 1][:, None]) + mod[:, 3 * k][:, None]


def gate_of(mod, k):
    return mod[:, 3 * k + 2][:, None]


def swiglu(h, w_in, w_out):
    a, b = jnp.split(h @ w_in, 2, axis=-1)
    return (jax.nn.silu(a) * b) @ w_out


def centred_conv(x, w, b):
    T = x.shape[1]
    xp = jnp.pad(x, ((0, 0), (CONV_LEFT, CONV_W - 1 - CONV_LEFT), (0, 0)))
    out = xp[:, 0:T] * w[0]
    for k in range(1, CONV_W):
        out = out + xp[:, k:k + T] * w[k]
    return out + b


def block_diag(x, w, b):
    xb = x.reshape(*x.shape[:-1], N_LRU_BLOCKS, LRU_BLOCK)
    y = jnp.einsum('btnk,nkj->btnj', xb, w.astype(F32)).reshape(x.shape)
    return y + b.astype(F32)


def linear_scan(a, b, h0):
    def comb(l, r):
        return l[0] * r[0], r[0] * l[1] + r[1]
    A, Bc = lax.associative_scan(comb, (a, b), axis=1)
    return A * h0[:, None] + Bc


def rglru_mixer(h, h0, w_in, conv_w, conv_b, gw_a, gb_a, gw_x, gb_x, lam, w_out):
    gate, xr = jnp.split(h @ w_in, 2, axis=-1)
    xr = centred_conv(xr, conv_w, conv_b).astype(F32)
    h0 = h0.astype(F32)
    ys, finals = [], []
    for d in range(2):
        r = jax.nn.sigmoid(block_diag(xr, gw_a[d], gb_a[d]))
        i = jax.nn.sigmoid(block_diag(xr, gw_x[d], gb_x[d]))
        log_a = -LRU_C * r * jax.nn.softplus(-lam[d].astype(F32))
        a = jnp.exp(log_a)
        b = jnp.sqrt(-jnp.expm1(2.0 * log_a)) * (i * xr)
        if d == 1:
            a, b = jnp.flip(a, 1), jnp.flip(b, 1)
        hs = linear_scan(a, b, h0[:, d])
        finals.append(hs[:, -1])
        if d == 1:
            hs = jnp.flip(hs, 1)
        ys.append(hs)
    y = (ys[0] + ys[1]).astype(h.dtype) * jax.nn.gelu(gate)
    return y @ w_out, jnp.stack(finals, axis=1)


def axial_rope(x):
    T = x.shape[1]
    rows = T // GRID_W
    row = jnp.repeat(jnp.arange(rows), GRID_W)
    col = jnp.tile(jnp.arange(GRID_W), rows)
    pos = jnp.stack([row, col], axis=-1).astype(F32)
    nf = DIFF_HD // 4
    inv = ROPE_THETA ** (-jnp.arange(nf, dtype=F32) / nf)
    ang = pos[:, :, None] * inv
    cos = jnp.cos(ang)[None, :, None, None]
    sin = jnp.sin(ang)[None, :, None, None]
    xf = x.astype(F32).reshape(*x.shape[:-1], 2, 2, nf)
    x1, x2 = xf[..., 0, :], xf[..., 1, :]
    out = jnp.stack([x1 * cos - x2 * sin, x2 * cos + x1 * sin], axis=-2)
    return out.reshape(x.shape).astype(x.dtype)


def diff_attend(q, k, v, lam):
    B, Tq = q.shape[:2]
    nblk = Tq // Q_BLOCK
    qb = jnp.moveaxis(q.reshape(B, nblk, Q_BLOCK, *q.shape[2:]), 1, 0)
    kf, vf = k.astype(F32), v.astype(F32)
    scale = DIFF_HD ** -0.5

    def one(qblk):
        s = jnp.einsum('bqhmd,bkhmd->bhmqk', qblk.astype(F32), kf) * scale
        p = jax.nn.softmax(s, axis=-1)
        w = p[:, :, 0] - lam * p[:, :, 1]
        return jnp.einsum('bhqk,bkhd->bqhd', w, vf)

    o = lax.map(one, qb)
    return jnp.moveaxis(o, 0, 1).reshape(B, Tq, *o.shape[3:])


def diff_qkv(h, w_qkv):
    B, T, _ = h.shape
    q, k, v = jnp.split(h @ w_qkv, 3, axis=-1)
    q = q.reshape(B, T, N_DIFF_HEADS, 2, DIFF_HD)
    k = k.reshape(B, T, N_DIFF_HEADS, 2, DIFF_HD)
    v = v.reshape(B, T, N_DIFF_HEADS, 2 * DIFF_HD)
    return q, k, v


def diff_out(o, lam_init, subln_g, w_o, dtype):
    B, T = o.shape[:2]
    o = rmsnorm(o, subln_g) * (1.0 - lam_init)
    return o.reshape(B, T, D_MODEL).astype(dtype) @ w_o


def multiscale_pool(h, w_pool, scale):
    B, T, _ = h.shape
    hf = h.astype(F32)
    cs = jnp.pad(jnp.cumsum(hf, axis=1), ((0, 0), (1, 0), (0, 0)))
    t = jnp.arange(T)
    outs = []
    for g, win in enumerate(POOL_WINDOWS):
        sl = slice(g * POOL_GROUP, (g + 1) * POOL_GROUP)
        lo = jnp.clip(t - win // 2, 0, T)
        hi = jnp.clip(t + win // 2, 0, T)
        csg = cs[:, :, sl]
        cnt = (hi - lo).astype(F32)[None, :, None]
        mean = (jnp.take(csg, hi, axis=1) - jnp.take(csg, lo, axis=1)) / cnt
        outs.append(jnp.einsum('btc,cd->btd', mean - hf[:, :, sl], w_pool[g].astype(F32)))
    return (jnp.concatenate(outs, axis=-1) * scale.astype(F32)).astype(h.dtype)


def setup_inputs(seed: int = 0) -> dict:
    key = jax.random.key(seed)
    ks = iter(jax.random.split(key, 40))
    nrm = lambda shape, s=1.0: jax.random.normal(next(ks), shape, F32) * s
    D = D_MODEL
    u = jax.random.uniform(next(ks), (N_A_LAYERS, 2, D_RNN), F32, 0.9, 0.999)
    a_base = u ** (1.0 / LRU_C)
    a_lambda = jnp.log(a_base) - jnp.log1p(-a_base)
    return {
        "x_prompt": nrm((BATCH, SEQ, D)),
        "x_sample": nrm((DEC_BATCH, DEC_SEQ, D)),
        "state_rglru": nrm((DEC_BATCH, N_A_LAYERS, 2, D_RNN), 0.5),
        "cache_k_diff": nrm((DEC_BATCH, N_B_LAYERS, PAST_LEN, N_DIFF_HEADS, 2 * DIFF_HD)),
        "cache_v_diff": nrm((DEC_BATCH, N_B_LAYERS, PAST_LEN, N_DIFF_HEADS, 2 * DIFF_HD)),
        "c": nrm((DEC_BATCH, D)),
        "c_ctx": nrm((D,)),
        "norm_g": 1.0 + nrm((DEPTH, 3, D), 0.02),
        "w_mod": nrm((DEPTH, D, N_MOD * D), 0.5 * D ** -0.5),
        "b_mod": nrm((DEPTH, N_MOD * D), 0.02),
        "w_ffn_in": nrm((DEPTH, 2, D, 2 * D_FF), D ** -0.5),
        "w_ffn_out": nrm((DEPTH, 2, D_FF, D), D_FF ** -0.5),
        "a_w_in": nrm((N_A_LAYERS, D, 2 * D_RNN), D ** -0.5),
        "a_conv_w": nrm((N_A_LAYERS, CONV_W, D_RNN), CONV_W ** -0.5),
        "a_conv_b": nrm((N_A_LAYERS, D_RNN), 0.02),
        "a_gate_w_a": nrm((N_A_LAYERS, 2, N_LRU_BLOCKS, LRU_BLOCK, LRU_BLOCK), LRU_BLOCK ** -0.5),
        "a_gate_b_a": nrm((N_A_LAYERS, 2, D_RNN), 0.02),
        "a_gate_w_x": nrm((N_A_LAYERS, 2, N_LRU_BLOCKS, LRU_BLOCK, LRU_BLOCK), LRU_BLOCK ** -0.5),
        "a_gate_b_x": nrm((N_A_LAYERS, 2, D_RNN), 0.02),
        "a_lambda": a_lambda,
        "a_w_out": nrm((N_A_LAYERS, D_RNN, D), D_RNN ** -0.5),
        "b_w_qkv": nrm((N_B_LAYERS, D, 3 * D), D ** -0.5),
        "b_lam_q": nrm((N_B_LAYERS, 2, DIFF_HD), 0.1),
        "b_lam_k": nrm((N_B_LAYERS, 2, DIFF_HD), 0.1),
        "b_subln_g": 1.0 + nrm((N_B_LAYERS, 2 * DIFF_HD), 0.02),
        "b_w_o": nrm((N_B_LAYERS, D, D), D ** -0.5),
        "c_w_pool": nrm((N_C_LAYERS, N_POOL_GROUPS, POOL_GROUP, POOL_GROUP), POOL_GROUP ** -0.5),
        "c_scale": 1.0 + nrm((N_C_LAYERS, D), 0.1),
        "final_norm_g": 1.0 + nrm((D,), 0.02),
    }


def reference(x_prompt, x_sample, state_rglru, cache_k_diff, cache_v_diff, c, c_ctx,
              norm_g, w_mod, b_mod, w_ffn_in, w_ffn_out,
              a_w_in, a_conv_w, a_conv_b, a_gate_w_a, a_gate_b_a, a_gate_w_x, a_gate_b_x, a_lambda, a_w_out,
              b_w_qkv, b_lam_q, b_lam_k, b_subln_g, b_w_o,
              c_w_pool, c_scale, final_norm_g):
    xp, xs = x_prompt, x_sample
    new_states, new_k, new_v = [], [], []
    for l in range(DEPTH):
        mod_p = adaln(c_ctx[None], w_mod[l], b_mod[l])
        mod_s = adaln(c, w_mod[l], b_mod[l])
        xp = xp + 0.5 * gate_of(mod_p, 0) * swiglu(pre(xp, norm_g[l, 0], mod_p, 0), w_ffn_in[l, 0], w_ffn_out[l, 0])
        xs = xs + 0.5 * gate_of(mod_s, 0) * swiglu(pre(xs, norm_g[l, 0], mod_s, 0), w_ffn_in[l, 0], w_ffn_out[l, 0])
        hp = pre(xp, norm_g[l, 1], mod_p, 1)
        hs = pre(xs, norm_g[l, 1], mod_s, 1)
        kind, j = l % N_MIXERS, l // N_MIXERS
        if kind == 0:
            prm = (a_w_in[j], a_conv_w[j], a_conv_b[j], a_gate_w_a[j], a_gate_b_a[j],
                   a_gate_w_x[j], a_gate_b_x[j], a_lambda[j], a_w_out[j])
            h0 = jnp.zeros((hp.shape[0], 2, D_RNN), F32)
            mp, st = rglru_mixer(hp, h0, *prm)
            ms, _ = rglru_mixer(hs, state_rglru[:, j], *prm)
            new_states.append(st)
        elif kind == 1:
            lam_init = 0.8 - 0.6 * math.exp(-0.3 * l)
            lq, lk = b_lam_q[j].astype(F32), b_lam_k[j].astype(F32)
            lam = jnp.exp(jnp.sum(lq[0] * lk[0])) - jnp.exp(jnp.sum(lq[1] * lk[1])) + lam_init
            qp, kp, vp = diff_qkv(hp, b_w_qkv[j])
            mp = diff_out(diff_attend(qp, kp, vp, lam), lam_init, b_subln_g[j], b_w_o[j], hp.dtype)
            new_k.append(kp.reshape(kp.shape[0], kp.shape[1], N_DIFF_HEADS, 2 * DIFF_HD))
            new_v.append(vp)
            qs, ks_, vs = diff_qkv(hs, b_w_qkv[j])
            qs, ks_ = axial_rope(qs), axial_rope(ks_)
            kc = cache_k_diff[:, j].reshape(DEC_BATCH, -1, N_DIFF_HEADS, 2, DIFF_HD).astype(ks_.dtype)
            k_all = jnp.concatenate([kc, ks_], axis=1)
            v_all = jnp.concatenate([cache_v_diff[:, j].astype(vs.dtype), vs], axis=1)
            ms = diff_out(diff_attend(qs, k_all, v_all, lam), lam_init, b_subln_g[j], b_w_o[j], hs.dtype)
        else:
            mp = multiscale_pool(hp, c_w_pool[j], c_scale[j])
            ms = multiscale_pool(hs, c_w_pool[j], c_scale[j])
        xp = xp + gate_of(mod_p, 1) * mp
        xs = xs + gate_of(mod_s, 1) * ms
        xp = xp + 0.5 * gate_of(mod_p, 2) * swiglu(pre(xp, norm_g[l, 2], mod_p, 2), w_ffn_in[l, 1], w_ffn_out[l, 1])
        xs = xs + 0.5 * gate_of(mod_s, 2) * swiglu(pre(xs, norm_g[l, 2], mod_s, 2), w_ffn_in[l, 1], w_ffn_out[l, 1])
    y_prompt = rmsnorm(xp, final_norm_g)
    y_sample = rmsnorm(xs, final_norm_g)
    new_state_rglru = jnp.stack(new_states, axis=1).astype(x_prompt.dtype)
    new_cache_k_diff = jnp.stack(new_k, axis=1).astype(x_prompt.dtype)
    new_cache_v_diff = jnp.stack(new_v, axis=1).astype(x_prompt.dtype)
    return (y_prompt, y_sample, new_state_rglru, new_cache_k_diff, new_cache_v_diff)
```

```python
import functools
import math

import jax
import jax.numpy as jnp
import numpy as np
from jax import lax
from jax.experimental import pallas as pl
from jax.experimental.pallas import tpu as pltpu

F32 = jnp.float32
BF16 = jnp.bfloat16

D = 1024
DEPTH = 4
N_MOD = 9
EPS = 1e-6
D_FF = 2816
LRU_BLOCK = 64
N_LRU_BLOCKS = 16
LRU_C = 8.0
N_HEADS = 8
LANES = 128
HEAD_W = 128
MAP_W = 64
GRID_W = 64
ROPE_THETA = 10000.0
POOL_WINDOWS = (2, 4, 8, 16)
POOL_GROUP = 256
POOL_PAD = 16

GATE_GROUP = 256
N_GATE_GROUPS = D // GATE_GROUP

MIB = 1024 * 1024


def _params(sem, vmem_mib):
    return pltpu.CompilerParams(dimension_semantics=sem, vmem_limit_bytes=vmem_mib * MIB)


def _dot(a, b):
    return jnp.dot(a, b, preferred_element_type=F32)


def _dot_nt(a, b):
    return lax.dot_general(a, b, (((1,), (1,)), ((), ())), preferred_element_type=F32)


def _sigmoid(x):
    return 0.5 + 0.5 * jnp.tanh(0.5 * x)


def _silu(x):
    return x * _sigmoid(x)


def _gelu_tanh(x):
    c = math.sqrt(2.0 / math.pi)
    return 0.5 * x * (1.0 + jnp.tanh(c * (x + 0.044715 * (x * x * x))))


def _rms(x, g):
    ms = jnp.mean(x * x, axis=-1, keepdims=True)
    return (x * lax.rsqrt(ms + EPS)) * g


def _prenorm(x, g, shift, scale):
    return _rms(x, g) * (1.0 + scale) + shift


def _mod_kernel(cond_ref, w_ref, b_ref, o_ref):
    s = _silu(cond_ref[...]).astype(BF16)
    o_ref[...] = _dot(s, w_ref[...].astype(BF16)) + b_ref[...]


def _adaln_all(cond8, w_mod, b_mod):
    tn = 1536
    n = N_MOD * D
    return pl.pallas_call(
        _mod_kernel,
        grid=(DEPTH, n // tn),
        in_specs=[
            pl.BlockSpec((8, D), lambda l, j: (0, 0)),
            pl.BlockSpec((None, D, tn), lambda l, j: (l, 0, j)),
            pl.BlockSpec((None, 1, tn), lambda l, j: (l, 0, j)),
        ],
        out_specs=pl.BlockSpec((None, 8, tn), lambda l, j: (l, 0, j)),
        out_shape=jax.ShapeDtypeStruct((DEPTH, 8, n), F32),
        compiler_params=_params(("parallel", "parallel"), 32),
        name="adaln_mod",
    )(cond8, w_mod, b_mod.reshape(DEPTH, 1, n))


def _ffn_kernel(x_ref, mod_ref, g_ref, wa_ref, wb_ref, wo_ref, fg_ref, o_ref, h_ref, acc_ref, *,
                n_chunks, final_norm):
    j = pl.program_id(1)

    @pl.when(j == 0)
    def _():
        h = _prenorm(x_ref[...], g_ref[...], mod_ref[0:1, :], mod_ref[1:2, :])
        h_ref[...] = h.astype(BF16)
        acc_ref[...] = jnp.zeros_like(acc_ref)

    h = h_ref[...]
    a = _dot(h, wa_ref[...].astype(BF16))
    b = _dot(h, wb_ref[...].astype(BF16))
    u = (_silu(a) * b).astype(BF16)
    acc_ref[...] += _dot(u, wo_ref[...].astype(BF16))

    @pl.when(j == n_chunks - 1)
    def _():
        y = x_ref[...] + (0.5 * mod_ref[2:3, :]) * acc_ref[...]
        if final_norm:
            y = _rms(y, fg_ref[...])
        o_ref[...] = y


def _ffn(x, mod3, g, w_in, w_out, final_g, *, rows_per_group, final_norm):
    n = x.shape[0]
    tm, fc = 1024, 256
    n_chunks = D_FF // fc
    tiles_per_group = rows_per_group // tm
    kern = functools.partial(_ffn_kernel, n_chunks=n_chunks, final_norm=final_norm)
    return pl.pallas_call(
        kern,
        grid=(n // tm, n_chunks),
        in_specs=[
            pl.BlockSpec((tm, D), lambda i, j: (i, 0)),
            pl.BlockSpec((None, 3, D), lambda i, j: (i // tiles_per_group, 0, 0)),
            pl.BlockSpec((1, D), lambda i, j: (0, 0)),
            pl.BlockSpec((D, fc), lambda i, j: (0, j)),
            pl.BlockSpec((D, fc), lambda i, j: (0, n_chunks + j)),
            pl.BlockSpec((fc, D), lambda i, j: (j, 0)),
            pl.BlockSpec((1, D), lambda i, j: (0, 0)),
        ],
        out_specs=pl.BlockSpec((tm, D), lambda i, j: (i, 0)),
        out_shape=jax.ShapeDtypeStruct((n, D), F32),
        scratch_shapes=[pltpu.VMEM((tm, D), BF16), pltpu.VMEM((tm, D), F32)],
        compiler_params=_params(("parallel", "arbitrary"), 48),
        name="ffn",
    )(x, mod3, g, w_in, w_in, w_out, final_g)


def _proj_kernel(x_ref, mod_ref, g_ref, w_ref, *o_refs):
    h = _prenorm(x_ref[...], g_ref[...], mod_ref[0:1, :], mod_ref[1:2, :]).astype(BF16)
    for k, o_ref in enumerate(o_refs):
        o_ref[...] = _dot(h, w_ref[:, k * D:(k + 1) * D])


def _prenorm_proj(x, mod3, g, w_bf16, *, rows_per_group):
    n = x.shape[0]
    n_out = w_bf16.shape[1] // D
    tm = 512
    tiles_per_group = rows_per_group // tm
    return pl.pallas_call(
        _proj_kernel,
        grid=(n // tm,),
        in_specs=[
            pl.BlockSpec((tm, D), lambda i: (i, 0)),
            pl.BlockSpec((None, 3, D), lambda i: (i // tiles_per_group, 0, 0)),
            pl.BlockSpec((1, D), lambda i: (0, 0)),
            pl.BlockSpec((D, n_out * D), lambda i: (0, 0)),
        ],
        out_specs=[pl.BlockSpec((tm, D), lambda i: (i, 0)) for _ in range(n_out)],
        out_shape=[jax.ShapeDtypeStruct((n, D), F32) for _ in range(n_out)],
        compiler_params=_params(("parallel",), 40),
        name="prenorm_proj",
    )(x, mod3, g, w_bf16)


def _outproj_kernel(y_ref, w_ref, x_ref, mod_ref, o_ref):
    o_ref[...] = x_ref[...] + mod_ref[2:3, :] * _dot(y_ref[...], w_ref[...])


def _outproj_residual(y_bf16, w_bf16, x, mod3, *, rows_per_group):
    n = x.shape[0]
    tm = 512
    tiles_per_group = rows_per_group // tm
    return pl.pallas_call(
        _outproj_kernel,
        grid=(n // tm,),
        in_specs=[
            pl.BlockSpec((tm, D), lambda i: (i, 0)),
            pl.BlockSpec((D, D), lambda i: (0, 0)),
            pl.BlockSpec((tm, D), lambda i: (i, 0)),
            pl.BlockSpec((None, 3, D), lambda i: (i // tiles_per_group, 0, 0)),
        ],
        out_specs=pl.BlockSpec((tm, D), lambda i: (i, 0)),
        out_shape=jax.ShapeDtypeStruct((n, D), F32),
        compiler_params=_params(("parallel",), 32),
        name="outproj_residual",
    )(y_bf16, w_bf16, x, mod3)


def _rglru_kernel(gate_ref, xr_ref, cw_ref, cb_ref, wg_ref, bg_ref, lam_ref, h0_ref,
                  y_ref, fin_ref, pad_ref, xc_ref, a_ref, b_ref, ysum_ref, *, seq_len, n_seq):
    tm = seq_len * n_seq
    pad_ref[0:8, :] = jnp.zeros((8, D), F32)
    pad_ref[8 + tm:16 + tm, :] = jnp.zeros((8, D), F32)
    pad_ref[8:8 + tm, :] = xr_ref[...]
    t_loc = lax.broadcasted_iota(jnp.int32, (tm, 1), 0) % seq_len
    xm2 = jnp.where(t_loc >= 2, pad_ref[6:6 + tm, :], 0.0)
    xm1 = jnp.where(t_loc >= 1, pad_ref[7:7 + tm, :], 0.0)
    xp1 = jnp.where(t_loc <= seq_len - 2, pad_ref[9:9 + tm, :], 0.0)
    xc = xm2 * cw_ref[0:1, :] + xm1 * cw_ref[1:2, :]
    xc = xc + xr_ref[...] * cw_ref[2:3, :]
    xc = xc + xp1 * cw_ref[3:4, :]
    xc_ref[...] = xc + cb_ref[...]

    if n_seq == 1:
        rows = lambda t: pl.ds(t, 1)
    else:
        rows = lambda t: pl.ds(t, n_seq, stride=seq_len)

    for d in range(2):
        z = -lam_ref[d:d + 1, :]
        decay = -LRU_C * (jnp.maximum(z, 0.0) + jnp.log1p(jnp.exp(-jnp.abs(z))))
        for q in range(N_GATE_GROUPS):
            cols = slice(q * GATE_GROUP, (q + 1) * GATE_GROUP)
            xq = xc_ref[:, cols]
            pre = _dot(xq.astype(BF16), wg_ref[d, q]) + bg_ref[d, q]
            r = _sigmoid(pre[:, :GATE_GROUP])
            i = _sigmoid(pre[:, GATE_GROUP:])
            log_a = r * decay[:, cols]
            a = jnp.exp(log_a)
            m = jnp.sqrt(-jnp.tanh(log_a) * (1.0 + a * a))
            b = m * (i * xq)
            for c in range(GATE_GROUP // LANES):
                lg = q * (GATE_GROUP // LANES) + c
                a_ref[lg] = a[:, c * LANES:(c + 1) * LANES]
                b_ref[lg] = b[:, c * LANES:(c + 1) * LANES]

        def step(k, hs, d=d):
            t = k if d == 0 else seq_len - 1 - k
            idx = rows(t)
            out = []
            for lg in range(D // LANES):
                h = a_ref[lg, idx, :] * hs[lg] + b_ref[lg, idx, :]
                if d == 0:
                    ysum_ref[lg, idx, :] = h
                else:
                    ysum_ref[lg, idx, :] = ysum_ref[lg, idx, :] + h
                out.append(h)
            return tuple(out)

        h0 = h0_ref[d]
        h_init = tuple(h0[:, lg * LANES:(lg + 1) * LANES] for lg in range(D // LANES))
        h_fin = lax.fori_loop(0, seq_len, step, h_init, unroll=8)
        for lg in range(D // LANES):
            fin_ref[d, :, lg * LANES:(lg + 1) * LANES] = h_fin[lg]

    for lg in range(D // LANES):
        cols = slice(lg * LANES, (lg + 1) * LANES)
        y_ref[:, cols] = (ysum_ref[lg] * _gelu_tanh(gate_ref[:, cols])).astype(BF16)


def _rglru_core(gate, xr, conv_w, conv_b, wg, bg, lam, h0, *, seq_len, n_seq):
    n = gate.shape[0]
    tm = seq_len * n_seq
    n_tiles = n // tm
    kern = functools.partial(_rglru_kernel, seq_len=seq_len, n_seq=n_seq)
    return pl.pallas_call(
        kern,
        grid=(n_tiles,),
        in_specs=[
            pl.BlockSpec((tm, D), lambda i: (i, 0)),
            pl.BlockSpec((tm, D), lambda i: (i, 0)),
            pl.BlockSpec((4, D), lambda i: (0, 0)),
            pl.BlockSpec((1, D), lambda i: (0, 0)),
            pl.BlockSpec((2, N_GATE_GROUPS, GATE_GROUP, 2 * GATE_GROUP), lambda i: (0, 0, 0, 0)),
            pl.BlockSpec((2, N_GATE_GROUPS, 1, 2 * GATE_GROUP), lambda i: (0, 0, 0, 0)),
            pl.BlockSpec((2, D), lambda i: (0, 0)),
            pl.BlockSpec((2, None, n_seq, D), lambda i: (0, i, 0, 0)),
        ],
        out_specs=[
            pl.BlockSpec((tm, D), lambda i: (i, 0)),
            pl.BlockSpec((2, None, n_seq, D), lambda i: (0, i, 0, 0)),
        ],
        out_shape=[
            jax.ShapeDtypeStruct((n, D), BF16),
            jax.ShapeDtypeStruct((2, n_tiles, n_seq, D), F32),
        ],
        scratch_shapes=[
            pltpu.VMEM((tm + 16, D), F32),
            pltpu.VMEM((tm, D), F32),
            pltpu.VMEM((D // LANES, tm, LANES), F32),
            pltpu.VMEM((D // LANES, tm, LANES), F32),
            pltpu.VMEM((D // LANES, tm, LANES), F32),
        ],
        compiler_params=_params(("parallel",), 48),
        name="rglru_core",
    )(gate, xr, conv_w, conv_b, wg, bg, lam, h0)


def _gate_weights(gw_a, gb_a, gw_x, gb_x):
    per = GATE_GROUP // LRU_BLOCK

    def dense(w):
        w = w.reshape(2, N_GATE_GROUPS, per, LRU_BLOCK, LRU_BLOCK)
        eye = jnp.eye(per, dtype=w.dtype)
        full = jnp.einsum('dqpkj,pr->dqpkrj', w, eye)
        return full.reshape(2, N_GATE_GROUPS, GATE_GROUP, GATE_GROUP)

    wg = jnp.concatenate([dense(gw_a), dense(gw_x)], axis=-1).astype(BF16)
    ba = gb_a.reshape(2, N_GATE_GROUPS, 1, GATE_GROUP)
    bx = gb_x.reshape(2, N_GATE_GROUPS, 1, GATE_GROUP)
    return wg, jnp.concatenate([ba, bx], axis=-1)


def _lam_value(lq_ref, lk_ref, lam_init):
    s = jnp.sum(lq_ref[...] * lk_ref[...], axis=-1, keepdims=True)
    e = jnp.exp(s)
    return e[0:1, :] - e[1:2, :] + lam_init


def _softmax_rows(s):
    e = jnp.exp(s - jnp.max(s, axis=-1, keepdims=True))
    return e * (1.0 / jnp.sum(e, axis=-1, keepdims=True))


def _diff_head(qh, kh_bf16, vh_bf16, lam, g_row, out_scale):
    lane = lax.broadcasted_iota(jnp.int32, qh.shape, 1)
    scale = MAP_W ** -0.5
    q0 = jnp.where(lane < MAP_W, qh, 0.0).astype(BF16)
    q1 = jnp.where(lane >= MAP_W, qh, 0.0).astype(BF16)
    p0 = _softmax_rows(_dot_nt(q0, kh_bf16) * scale)
    p1 = _softmax_rows(_dot_nt(q1, kh_bf16) * scale)
    w = (p0 - lam * p1).astype(BF16)
    o = _dot(w, vh_bf16)
    return _rms(o, g_row) * out_scale


def _attn_ctx_kernel(q_ref, k_ref, v_ref, lq_ref, lk_ref, g_ref, o_ref, *, lam_init):
    lam = _lam_value(lq_ref, lk_ref, lam_init)
    for hd in range(N_HEADS):
        cols = slice(hd * HEAD_W, (hd + 1) * HEAD_W)
        o = _diff_head(q_ref[:, cols], k_ref[:, cols].astype(BF16), v_ref[:, cols].astype(BF16),
                       lam, g_ref[...], 1.0 - lam_init)
        o_ref[:, cols] = o.astype(BF16)


def _attn_ctx(q, k, v, lq, lk, g, *, seq_len, lam_init):
    n = q.shape[0]
    kern = functools.partial(_attn_ctx_kernel, lam_init=lam_init)
    tok = pl.BlockSpec((seq_len, D), lambda i: (i, 0))
    small = lambda shape: pl.BlockSpec(shape, lambda i: (0, 0))
    return pl.pallas_call(
        kern,
        grid=(n // seq_len,),
        in_specs=[tok, tok, tok, small((2, MAP_W)), small((2, MAP_W)), small((1, HEAD_W))],
        out_specs=tok,
        out_shape=jax.ShapeDtypeStruct((n, D), BF16),
        compiler_params=_params(("parallel",), 32),
        name="attn_context",
    )(q, k, v, lq, lk, g)


def _rope(x, cos, sin_signed):
    lane = lax.broadcasted_iota(jnp.int32, x.shape, 1)
    partner = jnp.where(lane % 32 < 16, pltpu.roll(x, HEAD_W - 16, 1), pltpu.roll(x, 16, 1))
    return x * cos + partner * sin_signed


def _attn_lat_kernel(q_ref, k_ref, v_ref, ck_ref, cv_ref, cos_ref, sin_ref, lq_ref, lk_ref, g_ref,
                     o_ref, kall_ref, vall_ref, *, lam_init, past_len, seq_len, tq):
    qb = pl.program_id(1)

    @pl.when(qb == 0)
    def _():
        kall_ref[0:past_len, :] = ck_ref[...].astype(BF16)
        vall_ref[0:past_len, :] = cv_ref[...].astype(BF16)
        vall_ref[past_len:past_len + seq_len, :] = v_ref[...].astype(BF16)
        for hd in range(N_HEADS):
            cols = slice(hd * HEAD_W, (hd + 1) * HEAD_W)
            kr = _rope(k_ref[:, cols], cos_ref[...], sin_ref[...])
            kall_ref[past_len:past_len + seq_len, cols] = kr.astype(BF16)

    lam = _lam_value(lq_ref, lk_ref, lam_init)
    row0 = pl.multiple_of(qb * tq, tq)
    cos_q = cos_ref[pl.ds(row0, tq), :]
    sin_q = sin_ref[pl.ds(row0, tq), :]
    for hd in range(N_HEADS):
        cols = slice(hd * HEAD_W, (hd + 1) * HEAD_W)
        qh = _rope(q_ref[:, cols], cos_q, sin_q)
        o = _diff_head(qh, kall_ref[:, cols], vall_ref[:, cols], lam, g_ref[...], 1.0 - lam_init)
        o_ref[:, cols] = o.astype(BF16)


def _attn_lat(q, k, v, cache_k, cache_v, cos, sin_signed, lq, lk, g, *, seq_len, lam_init):
    n = q.shape[0]
    n_batch = n // seq_len
    past_len = cache_k.shape[1]
    tq = 256
    n_qb = seq_len // tq
    kern = functools.partial(_attn_lat_kernel, lam_init=lam_init, past_len=past_len,
                             seq_len=seq_len, tq=tq)
    seq = pl.BlockSpec((seq_len, D), lambda b, j: (b, 0))
    cache = pl.BlockSpec((None, past_len, D), lambda b, j: (b, 0, 0))
    small = lambda shape: pl.BlockSpec(shape, lambda b, j: (0, 0))
    qblk = pl.BlockSpec((tq, D), lambda b, j: (b * n_qb + j, 0))
    return pl.pallas_call(
        kern,
        grid=(n_batch, n_qb),
        in_specs=[qblk, seq, seq, cache, cache, small((seq_len, HEAD_W)), small((seq_len, HEAD_W)),
                  small((2, MAP_W)), small((2, MAP_W)), small((1, HEAD_W))],
        out_specs=qblk,
        out_shape=jax.ShapeDtypeStruct((n, D), BF16),
        scratch_shapes=[pltpu.VMEM((past_len + seq_len, D), BF16),
                        pltpu.VMEM((past_len + seq_len, D), BF16)],
        compiler_params=_params(("parallel", "arbitrary"), 48),
        name="attn_latent",
    )(q, k, v, cache_k, cache_v, cos, sin_signed, lq, lk, g)


def _rope_tables(seq_len):
    t = np.arange(seq_len)
    pos = np.stack([t // GRID_W, t % GRID_W], axis=-1).astype(np.float32)
    nf = MAP_W // 4
    inv = (np.float32(ROPE_THETA) ** (-np.arange(nf, dtype=np.float32) / nf)).astype(np.float32)
    lane = np.arange(HEAD_W)
    axis = (lane % MAP_W) // 32
    freq = lane % nf
    sign = np.where(lane % 32 < nf, -1.0, 1.0).astype(np.float32)
    pos_l = jnp.asarray(pos)[:, axis]
    ang = pos_l * jnp.asarray(inv)[freq][None, :]
    return jnp.cos(ang), jnp.sin(ang) * jnp.asarray(sign)[None, :]


def _pool_kernel(x_ref, mod_ref, g_ref, w_ref, sc_ref, o_ref, h_ref, lvl_ref, *, seq_len):
    T = seq_len
    R = T + 2 * POOL_PAD
    h = _prenorm(x_ref[...], g_ref[...], mod_ref[0:1, :], mod_ref[1:2, :])
    h_ref[...] = h
    lvl_ref[0:POOL_PAD, :] = jnp.zeros((POOL_PAD, D), F32)
    lvl_ref[POOL_PAD + T:R, :] = jnp.zeros((POOL_PAD, D), F32)
    lvl_ref[POOL_PAD:POOL_PAD + T, :] = h
    t = lax.broadcasted_iota(jnp.int32, (T, 1), 0)
    for gi, win in enumerate(POOL_WINDOWS):
        cols = slice(gi * POOL_GROUP, (gi + 1) * POOL_GROUP)
        half = win // 2
        s = 1
        while s < win:
            lvl_ref[POOL_PAD:R, cols] = lvl_ref[POOL_PAD - s:R - s, cols] + lvl_ref[POOL_PAD:R, cols]
            s *= 2
        wsum = lvl_ref[POOL_PAD + half - 1:POOL_PAD + half - 1 + T, cols]
        cnt = (jnp.minimum(t + half, T) - jnp.maximum(t - half, 0)).astype(F32)
        diff = wsum * (1.0 / cnt) - h_ref[:, cols]
        m = _dot(diff.astype(BF16), w_ref[gi]) * sc_ref[:, cols]
        o_ref[:, cols] = x_ref[:, cols] + mod_ref[2:3, cols] * m


def _pool_mixer(x, mod3, g, w_bf16, scale, *, seq_len):
    n = x.shape[0]
    kern = functools.partial(_pool_kernel, seq_len=seq_len)
    tok = pl.BlockSpec((seq_len, D), lambda i: (i, 0))
    return pl.pallas_call(
        kern,
        grid=(n // seq_len,),
        in_specs=[
            tok,
            pl.BlockSpec((None, 3, D), lambda i: (i * seq_len // (n // mod3.shape[0]), 0, 0)),
            pl.BlockSpec((1, D), lambda i: (0, 0)),
            pl.BlockSpec((len(POOL_WINDOWS), POOL_GROUP, POOL_GROUP), lambda i: (0, 0, 0)),
            pl.BlockSpec((1, D), lambda i: (0, 0)),
        ],
        out_specs=tok,
        out_shape=jax.ShapeDtypeStruct((n, D), F32),
        scratch_shapes=[pltpu.VMEM((seq_len, D), F32),
                        pltpu.VMEM((seq_len + 2 * POOL_PAD, D), F32)],
        compiler_params=_params(("parallel",), 40),
        name="pool_mixer",
    )(x, mod3, g, w_bf16, scale)


def kernel(x_prompt, x_sample, state_rglru, cache_k_diff, cache_v_diff, c, c_ctx, norm_g, w_mod, b_mod, w_ffn_in, w_ffn_out, a_w_in, a_conv_w, a_conv_b, a_gate_w_a, a_gate_b_a, a_gate_w_x, a_gate_b_x, a_lambda, a_w_out, b_w_qkv, b_lam_q, b_lam_k, b_subln_g, b_w_o, c_w_pool, c_scale, final_norm_g):
    n_ctx_seq, ctx_len, _ = x_prompt.shape
    n_lat_seq, lat_len, _ = x_sample.shape
    n_ctx, n_lat = n_ctx_seq * ctx_len, n_lat_seq * lat_len
    xp = x_prompt.reshape(n_ctx, D)
    xs = x_sample.reshape(n_lat, D)

    cond8 = jnp.concatenate([c_ctx[None], c, jnp.zeros((8 - 1 - n_lat_seq, D), F32)], axis=0)
    mod = _adaln_all(cond8, w_mod, b_mod).reshape(DEPTH, 8, N_MOD, D)
    final_g = final_norm_g.reshape(1, D)

    new_states, new_k, new_v = [], [], []
    for l in range(DEPTH):
        mod_p = mod[l, 0:1]
        mod_s = mod[l, 1:1 + n_lat_seq]
        g = norm_g[l]
        kind, j = l % 3, l // 3

        def ffn_both(xp, xs, k, half, final):
            sl = slice(3 * k, 3 * k + 3)
            xp = _ffn(xp, mod_p[:, sl], g[k:k + 1], w_ffn_in[l, half], w_ffn_out[l, half], final_g,
                      rows_per_group=n_ctx, final_norm=final)
            xs = _ffn(xs, mod_s[:, sl], g[k:k + 1], w_ffn_in[l, half], w_ffn_out[l, half], final_g,
                      rows_per_group=lat_len, final_norm=final)
            return xp, xs

        xp, xs = ffn_both(xp, xs, 0, 0, False)

        mp3, ms3, g1 = mod_p[:, 3:6], mod_s[:, 3:6], g[1:2]
        if kind == 0:
            w_in = a_w_in[j].astype(BF16)
            w_out = a_w_out[j].astype(BF16)
            wg, bg = _gate_weights(a_gate_w_a[j], a_gate_b_a[j], a_gate_w_x[j], a_gate_b_x[j])
            conv_b = a_conv_b[j].reshape(1, D)
            seqs_per_tile = 4
            gate_p, xr_p = _prenorm_proj(xp, mp3, g1, w_in, rows_per_group=n_ctx)
            h0 = jnp.zeros((2, n_ctx_seq // seqs_per_tile, seqs_per_tile, D), F32)
            y_p, fin = _rglru_core(gate_p, xr_p, a_conv_w[j], conv_b, wg, bg, a_lambda[j], h0,
                                   seq_len=ctx_len, n_seq=seqs_per_tile)
            xp = _outproj_residual(y_p, w_out, xp, mp3, rows_per_group=n_ctx)
            new_states.append(jnp.transpose(fin.reshape(2, n_ctx_seq, D), (1, 0, 2)))
            gate_s, xr_s = _prenorm_proj(xs, ms3, g1, w_in, rows_per_group=lat_len)
            h0 = jnp.transpose(state_rglru[:, j].astype(F32), (1, 0, 2)).reshape(2, n_lat_seq, 1, D)
            y_s, _ = _rglru_core(gate_s, xr_s, a_conv_w[j], conv_b, wg, bg, a_lambda[j], h0,
                                 seq_len=lat_len, n_seq=1)
            xs = _outproj_residual(y_s, w_out, xs, ms3, rows_per_group=lat_len)
        elif kind == 1:
            lam_init = 0.8 - 0.6 * math.exp(-0.3 * l)
            w_qkv = b_w_qkv[j].astype(BF16)
            w_o = b_w_o[j].astype(BF16)
            sub_g = b_subln_g[j].reshape(1, HEAD_W)
            q_p, k_p, v_p = _prenorm_proj(xp, mp3, g1, w_qkv, rows_per_group=n_ctx)
            o_p = _attn_ctx(q_p, k_p, v_p, b_lam_q[j], b_lam_k[j], sub_g, seq_len=ctx_len,
                            lam_init=lam_init)
            xp = _outproj_residual(o_p, w_o, xp, mp3, rows_per_group=n_ctx)
            new_k.append(k_p.reshape(n_ctx_seq, ctx_len, N_HEADS, HEAD_W))
            new_v.append(v_p.reshape(n_ctx_seq, ctx_len, N_HEADS, HEAD_W))
            q_s, k_s, v_s = _prenorm_proj(xs, ms3, g1, w_qkv, rows_per_group=lat_len)
            cos, sin_signed = _rope_tables(lat_len)
            past_len = cache_k_diff.shape[2]
            o_s = _attn_lat(q_s, k_s, v_s,
                            cache_k_diff[:, j].reshape(n_lat_seq, past_len, D),
                            cache_v_diff[:, j].reshape(n_lat_seq, past_len, D),
                            cos, sin_signed, b_lam_q[j], b_lam_k[j], sub_g,
                            seq_len=lat_len, lam_init=lam_init)
            xs = _outproj_residual(o_s, w_o, xs, ms3, rows_per_group=lat_len)
        else:
            w_pool = c_w_pool[j].astype(BF16)
            sc = c_scale[j].reshape(1, D)
            xp = _pool_mixer(xp, mp3, g1, w_pool, sc, seq_len=ctx_len)
            xs = _pool_mixer(xs, ms3, g1, w_pool, sc, seq_len=lat_len)

        xp, xs = ffn_both(xp, xs, 2, 1, l == DEPTH - 1)

    y_prompt = xp.reshape(x_prompt.shape)
    y_sample = xs.reshape(x_sample.shape)
    new_state = jnp.stack(new_states, axis=1).astype(x_prompt.dtype)
    new_cache_k = jnp.stack(new_k, axis=1).astype(x_prompt.dtype)
    new_cache_v = jnp.stack(new_v, axis=1).astype(x_prompt.dtype)
    return (y_prompt, y_sample, new_state, new_cache_k, new_cache_v)
```

```python
import functools
import math

import jax
import jax.numpy as jnp
import numpy as np
from jax import lax
from jax.experimental import pallas as pl
from jax.experimental.pallas import tpu as pltpu

F32 = jnp.float32
BF16 = jnp.bfloat16

D = 1024
DEPTH = 4
N_MOD = 9
EPS = 1e-6
D_FF = 2816
LRU_BLOCK = 64
N_LRU_BLOCKS = 16
LRU_C = 8.0
N_HEADS = 8
LANES = 128
HEAD_W = 128
MAP_W = 64
GRID_W = 64
ROPE_THETA = 10000.0
POOL_WINDOWS = (2, 4, 8, 16)
POOL_GROUP = 256
POOL_PAD = 16

GATE_GROUP = 256
N_GATE_GROUPS = D // GATE_GROUP

MIB = 1024 * 1024


def _params(sem, vmem_mib):
    return pltpu.CompilerParams(dimension_semantics=sem, vmem_limit_bytes=vmem_mib * MIB)


def _dot(a, b):
    return jnp.dot(a, b, preferred_element_type=F32)


def _dot_nt(a, b):
    return lax.dot_general(a, b, (((1,), (1,)), ((), ())), preferred_element_type=F32)


def _sigmoid(x):
    return 0.5 + 0.5 * jnp.tanh(0.5 * x)


def _silu(x):
    return x * _sigmoid(x)


def _gelu_tanh(x):
    c = math.sqrt(2.0 / math.pi)
    return 0.5 * x * (1.0 + jnp.tanh(c * (x + 0.044715 * (x * x * x))))


def _rms(x, g):
    ms = jnp.mean(x * x, axis=-1, keepdims=True)
    return (x * lax.rsqrt(ms + EPS)) * g


def _prenorm(x, g, shift, scale):
    return _rms(x, g) * (1.0 + scale) + shift


def _mod_kernel(cond_ref, w_ref, b_ref, o_ref):
    s = _silu(cond_ref[...]).astype(BF16)
    o_ref[...] = _dot(s, w_ref[...].astype(BF16)) + b_ref[...]


def _adaln_all(cond8, w_mod, b_mod):
    tn = 1536
    n = N_MOD * D
    return pl.pallas_call(
        _mod_kernel,
        grid=(DEPTH, n // tn),
        in_specs=[
            pl.BlockSpec((8, D), lambda l, j: (0, 0)),
            pl.BlockSpec((None, D, tn), lambda l, j: (l, 0, j)),
            pl.BlockSpec((None, 1, tn), lambda l, j: (l, 0, j)),
        ],
        out_specs=pl.BlockSpec((None, 8, tn), lambda l, j: (l, 0, j)),
        out_shape=jax.ShapeDtypeStruct((DEPTH, 8, n), F32),
        compiler_params=_params(("parallel", "parallel"), 32),
        name="adaln_mod",
    )(cond8, w_mod, b_mod.reshape(DEPTH, 1, n))


FFN_TM = 1024
FFN_FC = 256
FFN_CHUNKS = D_FF // FFN_FC


def _ffn_kernel(x_ref, mod_ref, g_ref, wa_ref, wb_ref, wo_ref, fg_ref, o_ref,
                wa_s, wb_s, wo_s, h_ref, u_ref, *, final_norm):
    s = pl.program_id(0)
    fc = FFN_FC

    def hidden():
        return _prenorm(x_ref[...], g_ref[...], mod_ref[0:1, :], mod_ref[1:2, :]).astype(BF16)

    def finish(acc):
        y = x_ref[...] + (0.5 * mod_ref[2:3, :]) * acc
        if final_norm:
            y = _rms(y, fg_ref[...])
        o_ref[...] = y

    @pl.when(s < FFN_CHUNKS)
    def _():
        wa = wa_ref[...].astype(BF16)
        wb = wb_ref[...].astype(BF16)
        wo = wo_ref[...].astype(BF16)
        wa_s[s] = wa
        wb_s[s] = wb
        wo_s[pl.ds(pl.multiple_of(s * fc, fc), fc), :] = wo

        @pl.when(s == 0)
        def _():
            h_ref[...] = hidden()
            o_ref[...] = jnp.zeros_like(o_ref)

        h = h_ref[...]
        u = (_silu(_dot(h, wa)) * _dot(h, wb)).astype(BF16)
        o_ref[...] += _dot(u, wo)

        @pl.when(s == FFN_CHUNKS - 1)
        def _():
            finish(o_ref[...])

    @pl.when(s >= FFN_CHUNKS)
    def _():
        h_ref[...] = hidden()
        for j in range(FFN_CHUNKS):
            h = h_ref[...]
            u = _silu(_dot(h, wa_s[j])) * _dot(h, wb_s[j])
            u_ref[:, j * fc:(j + 1) * fc] = u.astype(BF16)
        finish(_dot(u_ref[...], wo_s[...]))


def _ffn(x, mod3, g, w_in, w_out, final_g, *, layer, half, rows_per_group, final_norm):
    n = x.shape[0]
    tm, fc, nc = FFN_TM, FFN_FC, FFN_CHUNKS
    tiles_per_group = rows_per_group // tm
    tile = lambda s: jnp.maximum(s - (nc - 1), 0)
    chunk = lambda s: jnp.minimum(s, nc - 1)
    kern = functools.partial(_ffn_kernel, final_norm=final_norm)
    return pl.pallas_call(
        kern,
        grid=(nc - 1 + n // tm,),
        in_specs=[
            pl.BlockSpec((tm, D), lambda s: (tile(s), 0)),
            pl.BlockSpec((None, 3, D), lambda s: (tile(s) // tiles_per_group, 0, 0)),
            pl.BlockSpec((1, D), lambda s: (0, 0)),
            pl.BlockSpec((None, None, D, fc), lambda s: (layer, half, 0, chunk(s))),
            pl.BlockSpec((None, None, D, fc), lambda s: (layer, half, 0, nc + chunk(s))),
            pl.BlockSpec((None, None, fc, D), lambda s: (layer, half, chunk(s), 0)),
            pl.BlockSpec((1, D), lambda s: (0, 0)),
        ],
        out_specs=pl.BlockSpec((tm, D), lambda s: (tile(s), 0)),
        out_shape=jax.ShapeDtypeStruct((n, D), F32),
        scratch_shapes=[
            pltpu.VMEM((nc, D, fc), BF16),
            pltpu.VMEM((nc, D, fc), BF16),
            pltpu.VMEM((D_FF, D), BF16),
            pltpu.VMEM((tm, D), BF16),
            pltpu.VMEM((tm, D_FF), BF16),
        ],
        compiler_params=_params(("arbitrary",), 60),
        name="ffn",
    )(x, mod3, g, w_in, w_in, w_out, final_g)


def _proj_kernel(x_ref, mod_ref, g_ref, w_ref, *o_refs):
    h = _prenorm(x_ref[...], g_ref[...], mod_ref[0:1, :], mod_ref[1:2, :]).astype(BF16)
    for k, o_ref in enumerate(o_refs):
        o_ref[...] = _dot(h, w_ref[:, k * D:(k + 1) * D])


def _prenorm_proj(x, mod3, g, w_bf16, *, rows_per_group):
    n = x.shape[0]
    n_out = w_bf16.shape[1] // D
    tm = 512
    tiles_per_group = rows_per_group // tm
    return pl.pallas_call(
        _proj_kernel,
        grid=(n // tm,),
        in_specs=[
            pl.BlockSpec((tm, D), lambda i: (i, 0)),
            pl.BlockSpec((None, 3, D), lambda i: (i // tiles_per_group, 0, 0)),
            pl.BlockSpec((1, D), lambda i: (0, 0)),
            pl.BlockSpec((D, n_out * D), lambda i: (0, 0)),
        ],
        out_specs=[pl.BlockSpec((tm, D), lambda i: (i, 0)) for _ in range(n_out)],
        out_shape=[jax.ShapeDtypeStruct((n, D), F32) for _ in range(n_out)],
        compiler_params=_params(("parallel",), 40),
        name="prenorm_proj",
    )(x, mod3, g, w_bf16)


def _outproj_kernel(y_ref, w_ref, x_ref, mod_ref, o_ref):
    o_ref[...] = x_ref[...] + mod_ref[2:3, :] * _dot(y_ref[...], w_ref[...])


def _outproj_residual(y_bf16, w_bf16, x, mod3, *, rows_per_group):
    n = x.shape[0]
    tm = 512
    tiles_per_group = rows_per_group // tm
    return pl.pallas_call(
        _outproj_kernel,
        grid=(n // tm,),
        in_specs=[
            pl.BlockSpec((tm, D), lambda i: (i, 0)),
            pl.BlockSpec((D, D), lambda i: (0, 0)),
            pl.BlockSpec((tm, D), lambda i: (i, 0)),
            pl.BlockSpec((None, 3, D), lambda i: (i // tiles_per_group, 0, 0)),
        ],
        out_specs=pl.BlockSpec((tm, D), lambda i: (i, 0)),
        out_shape=jax.ShapeDtypeStruct((n, D), F32),
        compiler_params=_params(("parallel",), 32),
        name="outproj_residual",
    )(y_bf16, w_bf16, x, mod3)


def _rglru_kernel(gate_ref, xr_ref, cw_ref, cb_ref, wg_ref, bg_ref, lam_ref, h0_ref,
                  y_ref, fin_ref, pad_ref, xc_ref, a_ref, b_ref, ysum_ref, *, seq_len, n_seq):
    tm = seq_len * n_seq
    pad_ref[0:8, :] = jnp.zeros((8, D), F32)
    pad_ref[8 + tm:16 + tm, :] = jnp.zeros((8, D), F32)
    pad_ref[8:8 + tm, :] = xr_ref[...]
    t_loc = lax.broadcasted_iota(jnp.int32, (tm, 1), 0) % seq_len
    xm2 = jnp.where(t_loc >= 2, pad_ref[6:6 + tm, :], 0.0)
    xm1 = jnp.where(t_loc >= 1, pad_ref[7:7 + tm, :], 0.0)
    xp1 = jnp.where(t_loc <= seq_len - 2, pad_ref[9:9 + tm, :], 0.0)
    xc = xm2 * cw_ref[0:1, :] + xm1 * cw_ref[1:2, :]
    xc = xc + xr_ref[...] * cw_ref[2:3, :]
    xc = xc + xp1 * cw_ref[3:4, :]
    xc_ref[...] = xc + cb_ref[...]

    if n_seq == 1:
        rows = lambda t: pl.ds(t, 1)
    else:
        rows = lambda t: pl.ds(t, n_seq, stride=seq_len)

    for d in range(2):
        z = -lam_ref[d:d + 1, :]
        decay = -LRU_C * (jnp.maximum(z, 0.0) + jnp.log1p(jnp.exp(-jnp.abs(z))))
        for q in range(N_GATE_GROUPS):
            cols = slice(q * GATE_GROUP, (q + 1) * GATE_GROUP)
            xq = xc_ref[:, cols]
            pre = _dot(xq.astype(BF16), wg_ref[d, q]) + bg_ref[d, q]
            r = _sigmoid(pre[:, :GATE_GROUP])
            i = _sigmoid(pre[:, GATE_GROUP:])
            log_a = r * decay[:, cols]
            a = jnp.exp(log_a)
            m = jnp.sqrt(-jnp.tanh(log_a) * (1.0 + a * a))
            b = m * (i * xq)
            for c in range(GATE_GROUP // LANES):
                lg = q * (GATE_GROUP // LANES) + c
                a_ref[lg] = a[:, c * LANES:(c + 1) * LANES]
                b_ref[lg] = b[:, c * LANES:(c + 1) * LANES]

        def step(k, hs, d=d):
            t = k if d == 0 else seq_len - 1 - k
            idx = rows(t)
            out = []
            for lg in range(D // LANES):
                h = a_ref[lg, idx, :] * hs[lg] + b_ref[lg, idx, :]
                if d == 0:
                    ysum_ref[lg, idx, :] = h
                else:
                    ysum_ref[lg, idx, :] = ysum_ref[lg, idx, :] + h
                out.append(h)
            return tuple(out)

        h0 = h0_ref[d]
        h_init = tuple(h0[:, lg * LANES:(lg + 1) * LANES] for lg in range(D // LANES))
        h_fin = lax.fori_loop(0, seq_len, step, h_init, unroll=8)
        for lg in range(D // LANES):
            fin_ref[d, :, lg * LANES:(lg + 1) * LANES] = h_fin[lg]

    for lg in range(D // LANES):
        cols = slice(lg * LANES, (lg + 1) * LANES)
        y_ref[:, cols] = (ysum_ref[lg] * _gelu_tanh(gate_ref[:, cols])).astype(BF16)


def _rglru_core(gate, xr, conv_w, conv_b, wg, bg, lam, h0, *, seq_len, n_seq):
    n = gate.shape[0]
    tm = seq_len * n_seq
    n_tiles = n // tm
    kern = functools.partial(_rglru_kernel, seq_len=seq_len, n_seq=n_seq)
    return pl.pallas_call(
        kern,
        grid=(n_tiles,),
        in_specs=[
            pl.BlockSpec((tm, D), lambda i: (i, 0)),
            pl.BlockSpec((tm, D), lambda i: (i, 0)),
            pl.BlockSpec((4, D), lambda i: (0, 0)),
            pl.BlockSpec((1, D), lambda i: (0, 0)),
            pl.BlockSpec((2, N_GATE_GROUPS, GATE_GROUP, 2 * GATE_GROUP), lambda i: (0, 0, 0, 0)),
            pl.BlockSpec((2, N_GATE_GROUPS, 1, 2 * GATE_GROUP), lambda i: (0, 0, 0, 0)),
            pl.BlockSpec((2, D), lambda i: (0, 0)),
            pl.BlockSpec((2, None, n_seq, D), lambda i: (0, i, 0, 0)),
        ],
        out_specs=[
            pl.BlockSpec((tm, D), lambda i: (i, 0)),
            pl.BlockSpec((2, None, n_seq, D), lambda i: (0, i, 0, 0)),
        ],
        out_shape=[
            jax.ShapeDtypeStruct((n, D), BF16),
            jax.ShapeDtypeStruct((2, n_tiles, n_seq, D), F32),
        ],
        scratch_shapes=[
            pltpu.VMEM((tm + 16, D), F32),
            pltpu.VMEM((tm, D), F32),
            pltpu.VMEM((D // LANES, tm, LANES), F32),
            pltpu.VMEM((D // LANES, tm, LANES), F32),
            pltpu.VMEM((D // LANES, tm, LANES), F32),
        ],
        compiler_params=_params(("parallel",), 48),
        name="rglru_core",
    )(gate, xr, conv_w, conv_b, wg, bg, lam, h0)


def _gate_weights(gw_a, gb_a, gw_x, gb_x):
    per = GATE_GROUP // LRU_BLOCK

    def dense(w):
        w = w.reshape(2, N_GATE_GROUPS, per, LRU_BLOCK, LRU_BLOCK)
        eye = jnp.eye(per, dtype=w.dtype)
        full = jnp.einsum('dqpkj,pr->dqpkrj', w, eye)
        return full.reshape(2, N_GATE_GROUPS, GATE_GROUP, GATE_GROUP)

    wg = jnp.concatenate([dense(gw_a), dense(gw_x)], axis=-1).astype(BF16)
    ba = gb_a.reshape(2, N_GATE_GROUPS, 1, GATE_GROUP)
    bx = gb_x.reshape(2, N_GATE_GROUPS, 1, GATE_GROUP)
    return wg, jnp.concatenate([ba, bx], axis=-1)


def _lam_value(lq_ref, lk_ref, lam_init):
    s = jnp.sum(lq_ref[...] * lk_ref[...], axis=-1, keepdims=True)
    e = jnp.exp(s)
    return e[0:1, :] - e[1:2, :] + lam_init


def _softmax_rows(s):
    e = jnp.exp(s - jnp.max(s, axis=-1, keepdims=True))
    return e * (1.0 / jnp.sum(e, axis=-1, keepdims=True))


def _diff_head(qh, kh_bf16, vh_bf16, lam, g_row, out_scale):
    lane = lax.broadcasted_iota(jnp.int32, qh.shape, 1)
    scale = MAP_W ** -0.5
    q0 = jnp.where(lane < MAP_W, qh, 0.0).astype(BF16)
    q1 = jnp.where(lane >= MAP_W, qh, 0.0).astype(BF16)
    p0 = _softmax_rows(_dot_nt(q0, kh_bf16) * scale)
    p1 = _softmax_rows(_dot_nt(q1, kh_bf16) * scale)
    w = (p0 - lam * p1).astype(BF16)
    o = _dot(w, vh_bf16)
    return _rms(o, g_row) * out_scale


def _attn_ctx_kernel(q_ref, k_ref, v_ref, lq_ref, lk_ref, g_ref, o_ref, *, lam_init):
    lam = _lam_value(lq_ref, lk_ref, lam_init)
    for hd in range(N_HEADS):
        cols = slice(hd * HEAD_W, (hd + 1) * HEAD_W)
        o = _diff_head(q_ref[:, cols], k_ref[:, cols].astype(BF16), v_ref[:, cols].astype(BF16),
                       lam, g_ref[...], 1.0 - lam_init)
        o_ref[:, cols] = o.astype(BF16)


def _attn_ctx(q, k, v, lq, lk, g, *, seq_len, lam_init):
    n = q.shape[0]
    kern = functools.partial(_attn_ctx_kernel, lam_init=lam_init)
    tok = pl.BlockSpec((seq_len, D), lambda i: (i, 0))
    small = lambda shape: pl.BlockSpec(shape, lambda i: (0, 0))
    return pl.pallas_call(
        kern,
        grid=(n // seq_len,),
        in_specs=[tok, tok, tok, small((2, MAP_W)), small((2, MAP_W)), small((1, HEAD_W))],
        out_specs=tok,
        out_shape=jax.ShapeDtypeStruct((n, D), BF16),
        compiler_params=_params(("parallel",), 32),
        name="attn_context",
    )(q, k, v, lq, lk, g)


def _rope(x, cos, sin_signed):
    lane = lax.broadcasted_iota(jnp.int32, x.shape, 1)
    partner = jnp.where(lane % 32 < 16, pltpu.roll(x, HEAD_W - 16, 1), pltpu.roll(x, 16, 1))
    return x * cos + partner * sin_signed


def _attn_lat_kernel(q_ref, k_ref, v_ref, ck_ref, cv_ref, cos_ref, sin_ref, lq_ref, lk_ref, g_ref,
                     o_ref, kall_ref, vall_ref, *, lam_init, past_len, seq_len, tq):
    qb = pl.program_id(1)

    @pl.when(qb == 0)
    def _():
        kall_ref[0:past_len, :] = ck_ref[...].astype(BF16)
        vall_ref[0:past_len, :] = cv_ref[...].astype(BF16)
        vall_ref[past_len:past_len + seq_len, :] = v_ref[...].astype(BF16)
        for hd in range(N_HEADS):
            cols = slice(hd * HEAD_W, (hd + 1) * HEAD_W)
            kr = _rope(k_ref[:, cols], cos_ref[...], sin_ref[...])
            kall_ref[past_len:past_len + seq_len, cols] = kr.astype(BF16)

    lam = _lam_value(lq_ref, lk_ref, lam_init)
    row0 = pl.multiple_of(qb * tq, tq)
    cos_q = cos_ref[pl.ds(row0, tq), :]
    sin_q = sin_ref[pl.ds(row0, tq), :]
    for hd in range(N_HEADS):
        cols = slice(hd * HEAD_W, (hd + 1) * HEAD_W)
        qh = _rope(q_ref[:, cols], cos_q, sin_q)
        o = _diff_head(qh, kall_ref[:, cols], vall_ref[:, cols], lam, g_ref[...], 1.0 - lam_init)
        o_ref[:, cols] = o.astype(BF16)


def _attn_lat(q, k, v, cache_k, cache_v, cos, sin_signed, lq, lk, g, *, seq_len, lam_init):
    n = q.shape[0]
    n_batch = n // seq_len
    past_len = cache_k.shape[1]
    tq = 256
    n_qb = seq_len // tq
    kern = functools.partial(_attn_lat_kernel, lam_init=lam_init, past_len=past_len,
                             seq_len=seq_len, tq=tq)
    seq = pl.BlockSpec((seq_len, D), lambda b, j: (b, 0))
    cache = pl.BlockSpec((None, past_len, D), lambda b, j: (b, 0, 0))
    small = lambda shape: pl.BlockSpec(shape, lambda b, j: (0, 0))
    qblk = pl.BlockSpec((tq, D), lambda b, j: (b * n_qb + j, 0))
    return pl.pallas_call(
        kern,
        grid=(n_batch, n_qb),
        in_specs=[qblk, seq, seq, cache, cache, small((seq_len, HEAD_W)), small((seq_len, HEAD_W)),
                  small((2, MAP_W)), small((2, MAP_W)), small((1, HEAD_W))],
        out_specs=qblk,
        out_shape=jax.ShapeDtypeStruct((n, D), BF16),
        scratch_shapes=[pltpu.VMEM((past_len + seq_len, D), BF16),
                        pltpu.VMEM((past_len + seq_len, D), BF16)],
        compiler_params=_params(("parallel", "arbitrary"), 48),
        name="attn_latent",
    )(q, k, v, cache_k, cache_v, cos, sin_signed, lq, lk, g)


def _rope_tables(seq_len):
    t = np.arange(seq_len)
    pos = np.stack([t // GRID_W, t % GRID_W], axis=-1).astype(np.float32)
    nf = MAP_W // 4
    inv = (np.float32(ROPE_THETA) ** (-np.arange(nf, dtype=np.float32) / nf)).astype(np.float32)
    lane = np.arange(HEAD_W)
    axis = (lane % MAP_W) // 32
    freq = lane % nf
    sign = np.where(lane % 32 < nf, -1.0, 1.0).astype(np.float32)
    pos_l = jnp.asarray(pos)[:, axis]
    ang = pos_l * jnp.asarray(inv)[freq][None, :]
    return jnp.cos(ang), jnp.sin(ang) * jnp.asarray(sign)[None, :]


def _pool_kernel(x_ref, mod_ref, g_ref, w_ref, sc_ref, o_ref, h_ref, lvl_ref, *, seq_len):
    T = seq_len
    R = T + 2 * POOL_PAD
    h = _prenorm(x_ref[...], g_ref[...], mod_ref[0:1, :], mod_ref[1:2, :])
    h_ref[...] = h
    lvl_ref[0:POOL_PAD, :] = jnp.zeros((POOL_PAD, D), F32)
    lvl_ref[POOL_PAD + T:R, :] = jnp.zeros((POOL_PAD, D), F32)
    lvl_ref[POOL_PAD:POOL_PAD + T, :] = h
    t = lax.broadcasted_iota(jnp.int32, (T, 1), 0)
    for gi, win in enumerate(POOL_WINDOWS):
        cols = slice(gi * POOL_GROUP, (gi + 1) * POOL_GROUP)
        half = win // 2
        s = 1
        while s < win:
            lvl_ref[POOL_PAD:R, cols] = lvl_ref[POOL_PAD - s:R - s, cols] + lvl_ref[POOL_PAD:R, cols]
            s *= 2
        wsum = lvl_ref[POOL_PAD + half - 1:POOL_PAD + half - 1 + T, cols]
        cnt = (jnp.minimum(t + half, T) - jnp.maximum(t - half, 0)).astype(F32)
        diff = wsum * (1.0 / cnt) - h_ref[:, cols]
        m = _dot(diff.astype(BF16), w_ref[gi]) * sc_ref[:, cols]
        o_ref[:, cols] = x_ref[:, cols] + mod_ref[2:3, cols] * m


def _pool_mixer(x, mod3, g, w_bf16, scale, *, seq_len):
    n = x.shape[0]
    kern = functools.partial(_pool_kernel, seq_len=seq_len)
    tok = pl.BlockSpec((seq_len, D), lambda i: (i, 0))
    return pl.pallas_call(
        kern,
        grid=(n // seq_len,),
        in_specs=[
            tok,
            pl.BlockSpec((None, 3, D), lambda i: (i * seq_len // (n // mod3.shape[0]), 0, 0)),
            pl.BlockSpec((1, D), lambda i: (0, 0)),
            pl.BlockSpec((len(POOL_WINDOWS), POOL_GROUP, POOL_GROUP), lambda i: (0, 0, 0)),
            pl.BlockSpec((1, D), lambda i: (0, 0)),
        ],
        out_specs=tok,
        out_shape=jax.ShapeDtypeStruct((n, D), F32),
        scratch_shapes=[pltpu.VMEM((seq_len, D), F32),
                        pltpu.VMEM((seq_len + 2 * POOL_PAD, D), F32)],
        compiler_params=_params(("parallel",), 40),
        name="pool_mixer",
    )(x, mod3, g, w_bf16, scale)


def kernel(x_prompt, x_sample, state_rglru, cache_k_diff, cache_v_diff, c, c_ctx, norm_g, w_mod, b_mod, w_ffn_in, w_ffn_out, a_w_in, a_conv_w, a_conv_b, a_gate_w_a, a_gate_b_a, a_gate_w_x, a_gate_b_x, a_lambda, a_w_out, b_w_qkv, b_lam_q, b_lam_k, b_subln_g, b_w_o, c_w_pool, c_scale, final_norm_g):
    n_ctx_seq, ctx_len, _ = x_prompt.shape
    n_lat_seq, lat_len, _ = x_sample.shape
    n_ctx, n_lat = n_ctx_seq * ctx_len, n_lat_seq * lat_len
    xp = x_prompt.reshape(n_ctx, D)
    xs = x_sample.reshape(n_lat, D)

    cond8 = jnp.concatenate([c_ctx[None], c, jnp.zeros((8 - 1 - n_lat_seq, D), F32)], axis=0)
    mod = _adaln_all(cond8, w_mod, b_mod).reshape(DEPTH, 8, N_MOD, D)
    final_g = final_norm_g.reshape(1, D)

    new_states, new_k, new_v = [], [], []
    for l in range(DEPTH):
        mod_p = mod[l, 0:1]
        mod_s = mod[l, 1:1 + n_lat_seq]
        g = norm_g[l]
        kind, j = l % 3, l // 3

        def ffn_both(xp, xs, k, half, final):
            sl = slice(3 * k, 3 * k + 3)
            xp = _ffn(xp, mod_p[:, sl], g[k:k + 1], w_ffn_in, w_ffn_out, final_g,
                      layer=l, half=half, rows_per_group=n_ctx, final_norm=final)
            xs = _ffn(xs, mod_s[:, sl], g[k:k + 1], w_ffn_in, w_ffn_out, final_g,
                      layer=l, half=half, rows_per_group=lat_len, final_norm=final)
            return xp, xs

        xp, xs = ffn_both(xp, xs, 0, 0, False)

        mp3, ms3, g1 = mod_p[:, 3:6], mod_s[:, 3:6], g[1:2]
        if kind == 0:
            w_in = a_w_in[j].astype(BF16)
            w_out = a_w_out[j].astype(BF16)
            wg, bg = _gate_weights(a_gate_w_a[j], a_gate_b_a[j], a_gate_w_x[j], a_gate_b_x[j])
            conv_b = a_conv_b[j].reshape(1, D)
            seqs_per_tile = 4
            gate_p, xr_p = _prenorm_proj(xp, mp3, g1, w_in, rows_per_group=n_ctx)
            h0 = jnp.zeros((2, n_ctx_seq // seqs_per_tile, seqs_per_tile, D), F32)
            y_p, fin = _rglru_core(gate_p, xr_p, a_conv_w[j], conv_b, wg, bg, a_lambda[j], h0,
                                   seq_len=ctx_len, n_seq=seqs_per_tile)
            xp = _outproj_residual(y_p, w_out, xp, mp3, rows_per_group=n_ctx)
            new_states.append(jnp.transpose(fin.reshape(2, n_ctx_seq, D), (1, 0, 2)))
            gate_s, xr_s = _prenorm_proj(xs, ms3, g1, w_in, rows_per_group=lat_len)
            h0 = jnp.transpose(state_rglru[:, j].astype(F32), (1, 0, 2)).reshape(2, n_lat_seq, 1, D)
            y_s, _ = _rglru_core(gate_s, xr_s, a_conv_w[j], conv_b, wg, bg, a_lambda[j], h0,
                                 seq_len=lat_len, n_seq=1)
            xs = _outproj_residual(y_s, w_out, xs, ms3, rows_per_group=lat_len)
        elif kind == 1:
            lam_init = 0.8 - 0.6 * math.exp(-0.3 * l)
            w_qkv = b_w_qkv[j].astype(BF16)
            w_o = b_w_o[j].astype(BF16)
            sub_g = b_subln_g[j].reshape(1, HEAD_W)
            q_p, k_p, v_p = _prenorm_proj(xp, mp3, g1, w_qkv, rows_per_group=n_ctx)
            o_p = _attn_ctx(q_p, k_p, v_p, b_lam_q[j], b_lam_k[j], sub_g, seq_len=ctx_len,
                            lam_init=lam_init)
            xp = _outproj_residual(o_p, w_o, xp, mp3, rows_per_group=n_ctx)
            new_k.append(k_p.reshape(n_ctx_seq, ctx_len, N_HEADS, HEAD_W))
            new_v.append(v_p.reshape(n_ctx_seq, ctx_len, N_HEADS, HEAD_W))
            q_s, k_s, v_s = _prenorm_proj(xs, ms3, g1, w_qkv, rows_per_group=lat_len)
            cos, sin_signed = _rope_tables(lat_len)
            past_len = cache_k_diff.shape[2]
            o_s = _attn_lat(q_s, k_s, v_s,
                            cache_k_diff[:, j].reshape(n_lat_seq, past_len, D),
                            cache_v_diff[:, j].reshape(n_lat_seq, past_len, D),
                            cos, sin_signed, b_lam_q[j], b_lam_k[j], sub_g,
                            seq_len=lat_len, lam_init=lam_init)
            xs = _outproj_residual(o_s, w_o, xs, ms3, rows_per_group=lat_len)
        else:
            w_pool = c_w_pool[j].astype(BF16)
            sc = c_scale[j].reshape(1, D)
            xp = _pool_mixer(xp, mp3, g1, w_pool, sc, seq_len=ctx_len)
            xs = _pool_mixer(xs, ms3, g1, w_pool, sc, seq_len=lat_len)

        xp, xs = ffn_both(xp, xs, 2, 1, l == DEPTH - 1)

    y_prompt = xp.reshape(x_prompt.shape)
    y_sample = xs.reshape(x_sample.shape)
    new_state = jnp.stack(new_states, axis=1).astype(x_prompt.dtype)
    new_cache_k = jnp.stack(new_k, axis=1).astype(x_prompt.dtype)
    new_cache_v = jnp.stack(new_v, axis=1).astype(x_prompt.dtype)
    return (y_prompt, y_sample, new_state, new_cache_k, new_cache_v)
```

```python
import functools
import math

import jax
import jax.numpy as jnp
import numpy as np
from jax import lax
from jax.experimental import pallas as pl
from jax.experimental.pallas import tpu as pltpu

F32 = jnp.float32
BF16 = jnp.bfloat16

D = 1024
DEPTH = 4
N_MOD = 9
EPS = 1e-6
D_FF = 2816
LRU_BLOCK = 64
N_LRU_BLOCKS = 16
LRU_C = 8.0
N_HEADS = 8
LANES = 128
HEAD_W = 128
MAP_W = 64
assert MAP_W ** -0.5 == 0.125
GRID_W = 64
ROPE_THETA = 10000.0
POOL_WINDOWS = (2, 4, 8, 16)
POOL_GROUP = 256
POOL_PAD = 16

GATE_GROUP = 256
N_GATE_GROUPS = D // GATE_GROUP

MIB = 1024 * 1024


def _params(sem, vmem_mib):
    return pltpu.CompilerParams(dimension_semantics=sem, vmem_limit_bytes=vmem_mib * MIB)


def _dot(a, b):
    return jnp.dot(a, b, preferred_element_type=F32)


def _dot_nt(a, b):
    return lax.dot_general(a, b, (((1,), (1,)), ((), ())), preferred_element_type=F32)


def _sigmoid(x):
    return 0.5 + 0.5 * jnp.tanh(0.5 * x)


def _silu(x):
    return x * _sigmoid(x)


def _gelu_tanh(x):
    c = math.sqrt(2.0 / math.pi)
    hx = 0.5 * x
    return hx + hx * jnp.tanh(x * (c + (0.044715 * c) * (x * x)))


def _rms(x, g):
    ms = jnp.mean(x * x, axis=-1, keepdims=True)
    return (x * lax.rsqrt(ms + EPS)) * g


def _prenorm(x, g, shift, scale):
    return _rms(x, g) * (1.0 + scale) + shift


def _mod_kernel(cond_ref, w_ref, b_ref, o_ref):
    s = _silu(cond_ref[...]).astype(BF16)
    o_ref[...] = _dot(s, w_ref[...].astype(BF16)) + b_ref[...]


def _adaln_all(cond8, w_mod, b_mod):
    tn = 1536
    n = N_MOD * D
    return pl.pallas_call(
        _mod_kernel,
        grid=(DEPTH, n // tn),
        in_specs=[
            pl.BlockSpec((8, D), lambda l, j: (0, 0)),
            pl.BlockSpec((None, D, tn), lambda l, j: (l, 0, j)),
            pl.BlockSpec((None, 1, tn), lambda l, j: (l, 0, j)),
        ],
        out_specs=pl.BlockSpec((None, 8, tn), lambda l, j: (l, 0, j)),
        out_shape=jax.ShapeDtypeStruct((DEPTH, 8, n), F32),
        compiler_params=_params(("parallel", "parallel"), 32),
        name="adaln_mod",
    )(cond8, w_mod, b_mod.reshape(DEPTH, 1, n))


FFN_TM = 1024
FFN_FC = 256
FFN_CHUNKS = D_FF // FFN_FC


FFN_SEQS = 4
TM_GROUP = 8


def _ffn_kernel(x_ref, mod_ref, g_ref, wa_ref, wb_ref, wo_ref, fg_ref, o_ref,
                wa_s, wb_s, wo_s, h_ref, u_ref, *, in_tm, out_tm, final_norm):
    s = pl.program_id(0)
    fc = FFN_FC
    P = FFN_TM // FFN_SEQS

    def piece(ref, k, time_major):
        return ref.at[:, k * D:(k + 1) * D] if time_major else ref.at[k * P:(k + 1) * P, :]

    def hidden():
        for k in range(FFN_SEQS):
            x = piece(x_ref, k, in_tm)[...]
            h = _prenorm(x, g_ref[...], mod_ref[0:1, :], mod_ref[1:2, :])
            h_ref[k * P:(k + 1) * P, :] = h.astype(BF16)

    def finish(acc_piece):
        for k in range(FFN_SEQS):
            y = piece(x_ref, k, in_tm)[...] + (0.5 * mod_ref[2:3, :]) * acc_piece(k)
            if final_norm:
                y = _rms(y, fg_ref[...])
            piece(o_ref, k, out_tm)[...] = y

    @pl.when(s < FFN_CHUNKS)
    def _():
        wa = wa_ref[...].astype(BF16)
        wb = wb_ref[...].astype(BF16)
        wo = wo_ref[...].astype(BF16)
        wa_s[s] = wa
        wb_s[s] = wb
        wo_s[pl.ds(pl.multiple_of(s * fc, fc), fc), :] = wo

        @pl.when(s == 0)
        def _():
            hidden()
            o_ref[...] = jnp.zeros_like(o_ref)

        h = h_ref[...]
        u = (_silu(_dot(h, wa)) * _dot(h, wb)).astype(BF16)
        part = _dot(u, wo)
        for k in range(FFN_SEQS):
            piece(o_ref, k, out_tm)[...] += part[k * P:(k + 1) * P, :]

        @pl.when(s == FFN_CHUNKS - 1)
        def _():
            finish(lambda k: piece(o_ref, k, out_tm)[...])

    @pl.when(s >= FFN_CHUNKS)
    def _():
        hidden()
        for j in range(FFN_CHUNKS):
            h = h_ref[...]
            u = _silu(_dot(h, wa_s[j])) * _dot(h, wb_s[j])
            u_ref[:, j * fc:(j + 1) * fc] = u.astype(BF16)
        acc = _dot(u_ref[...], wo_s[...])
        finish(lambda k: acc[k * P:(k + 1) * P, :])


def _ffn(x, mod3, g, w_in, w_out, final_g, *, layer, half, rows_per_group, final_norm,
         in_tm=False, out_tm=False):
    n = x.shape[0]
    tm, fc, nc = FFN_TM, FFN_FC, FFN_CHUNKS
    tiles_per_group = rows_per_group // tm
    tile = lambda s: jnp.maximum(s - (nc - 1), 0)
    chunk = lambda s: jnp.minimum(s, nc - 1)
    P = tm // FFN_SEQS
    halves = TM_GROUP // FFN_SEQS
    sm_spec = pl.BlockSpec((tm, D), lambda s: (tile(s), 0))
    tm_spec = pl.BlockSpec((None, P, FFN_SEQS * D), lambda s: (tile(s) // halves, 0, tile(s) % halves))
    tm_shape = (n // (P * TM_GROUP), P, TM_GROUP * D)
    kern = functools.partial(_ffn_kernel, in_tm=in_tm, out_tm=out_tm, final_norm=final_norm)
    out = pl.pallas_call(
        kern,
        grid=(nc - 1 + n // tm,),
        in_specs=[
            tm_spec if in_tm else sm_spec,
            pl.BlockSpec((None, 3, D), lambda s: (tile(s) // tiles_per_group, 0, 0)),
            pl.BlockSpec((1, D), lambda s: (0, 0)),
            pl.BlockSpec((None, None, D, fc), lambda s: (layer, half, 0, chunk(s))),
            pl.BlockSpec((None, None, D, fc), lambda s: (layer, half, 0, nc + chunk(s))),
            pl.BlockSpec((None, None, fc, D), lambda s: (layer, half, chunk(s), 0)),
            pl.BlockSpec((1, D), lambda s: (0, 0)),
        ],
        out_specs=tm_spec if out_tm else sm_spec,
        out_shape=jax.ShapeDtypeStruct(tm_shape if out_tm else (n, D), F32),
        scratch_shapes=[
            pltpu.VMEM((nc, D, fc), BF16),
            pltpu.VMEM((nc, D, fc), BF16),
            pltpu.VMEM((D_FF, D), BF16),
            pltpu.VMEM((tm, D), BF16),
            pltpu.VMEM((tm, D_FF), BF16),
        ],
        compiler_params=_params(("arbitrary",), 60),
        name="ffn",
    )(x.reshape(tm_shape) if in_tm else x, mod3, g, w_in, w_in, w_out, final_g)
    return out.reshape(n, D)


def _proj_kernel(x_ref, mod_ref, g_ref, w_ref, *o_refs):
    h = _prenorm(x_ref[...], g_ref[...], mod_ref[0:1, :], mod_ref[1:2, :]).astype(BF16)
    for k, o_ref in enumerate(o_refs):
        o_ref[...] = _dot(h, w_ref[:, k * D:(k + 1) * D])


def _prenorm_proj(x, mod3, g, w_bf16, *, rows_per_group):
    n = x.shape[0]
    n_out = w_bf16.shape[1] // D
    tm = 512
    tiles_per_group = rows_per_group // tm
    return pl.pallas_call(
        _proj_kernel,
        grid=(n // tm,),
        in_specs=[
            pl.BlockSpec((tm, D), lambda i: (i, 0)),
            pl.BlockSpec((None, 3, D), lambda i: (i // tiles_per_group, 0, 0)),
            pl.BlockSpec((1, D), lambda i: (0, 0)),
            pl.BlockSpec((D, n_out * D), lambda i: (0, 0)),
        ],
        out_specs=[pl.BlockSpec((tm, D), lambda i: (i, 0)) for _ in range(n_out)],
        out_shape=[jax.ShapeDtypeStruct((n, D), F32) for _ in range(n_out)],
        compiler_params=_params(("parallel",), 40),
        name="prenorm_proj",
    )(x, mod3, g, w_bf16)


def _outproj_kernel(y_ref, w_ref, x_ref, mod_ref, o_ref):
    o_ref[...] = x_ref[...] + mod_ref[2:3, :] * _dot(y_ref[...], w_ref[...])


def _outproj_residual(y_bf16, w_bf16, x, mod3, *, rows_per_group):
    n = x.shape[0]
    tm = 512
    tiles_per_group = rows_per_group // tm
    return pl.pallas_call(
        _outproj_kernel,
        grid=(n // tm,),
        in_specs=[
            pl.BlockSpec((tm, D), lambda i: (i, 0)),
            pl.BlockSpec((D, D), lambda i: (0, 0)),
            pl.BlockSpec((tm, D), lambda i: (i, 0)),
            pl.BlockSpec((None, 3, D), lambda i: (i // tiles_per_group, 0, 0)),
        ],
        out_specs=pl.BlockSpec((tm, D), lambda i: (i, 0)),
        out_shape=jax.ShapeDtypeStruct((n, D), F32),
        compiler_params=_params(("parallel",), 32),
        name="outproj_residual",
    )(y_bf16, w_bf16, x, mod3)


def _rglru_kernel(gate_ref, xr_ref, cw_ref, cb_ref, wg_ref, bg_ref, lam_ref, h0_ref,
                  y_ref, fin_ref, pad_ref, xc_ref, a_ref, b_ref, yf_ref, *, seq_len, n_seq):
    tm = seq_len * n_seq
    pad_ref[0:8, :] = jnp.zeros((8, D), F32)
    pad_ref[8 + tm:16 + tm, :] = jnp.zeros((8, D), F32)
    pad_ref[8:8 + tm, :] = xr_ref[...]
    t_loc = lax.broadcasted_iota(jnp.int32, (tm, 1), 0) % seq_len
    xm2 = jnp.where(t_loc >= 2, pad_ref[6:6 + tm, :], 0.0)
    xm1 = jnp.where(t_loc >= 1, pad_ref[7:7 + tm, :], 0.0)
    xp1 = jnp.where(t_loc <= seq_len - 2, pad_ref[9:9 + tm, :], 0.0)
    xc = xm2 * cw_ref[0:1, :] + xm1 * cw_ref[1:2, :]
    xc = xc + xr_ref[...] * cw_ref[2:3, :]
    xc = xc + xp1 * cw_ref[3:4, :]
    xc_ref[...] = xc + cb_ref[...]

    if n_seq == 1:
        rows = lambda t: pl.ds(t, 1)
    else:
        rows = lambda t: pl.ds(t, n_seq, stride=seq_len)

    for d in range(2):
        half_decay = _lru_half_decay(lam_ref[d:d + 1, :])
        for q in range(N_GATE_GROUPS):
            cols = slice(q * GATE_GROUP, (q + 1) * GATE_GROUP)
            xq = xc_ref[:, cols]
            half_pre = _dot(xq.astype(BF16), wg_ref[d, q]) + bg_ref[d, q]
            a, b = _lru_coeffs(half_pre, 0.5 * xq, half_decay[:, cols])
            for c in range(GATE_GROUP // LANES):
                lg = q * (GATE_GROUP // LANES) + c
                a_ref[lg] = a[:, c * LANES:(c + 1) * LANES]
                b_ref[lg] = b[:, c * LANES:(c + 1) * LANES]

        h_dst = yf_ref if d == 0 else b_ref

        def step(k, hs, d=d, h_dst=h_dst):
            t = k if d == 0 else seq_len - 1 - k
            idx = rows(t)
            out = []
            for lg in range(D // LANES):
                h = a_ref[lg, idx, :] * hs[lg] + b_ref[lg, idx, :]
                h_dst[lg, idx, :] = h
                out.append(h)
            return tuple(out)

        h0 = h0_ref[d]
        h_init = tuple(h0[:, lg * LANES:(lg + 1) * LANES] for lg in range(D // LANES))
        h_fin = lax.fori_loop(0, seq_len, step, h_init, unroll=8)
        for lg in range(D // LANES):
            fin_ref[d, :, lg * LANES:(lg + 1) * LANES] = h_fin[lg]

    for lg in range(D // LANES):
        cols = slice(lg * LANES, (lg + 1) * LANES)
        y_ref[:, cols] = ((yf_ref[lg] + b_ref[lg]) * _gelu_tanh(gate_ref[:, cols])).astype(BF16)


def _rglru_core(gate, xr, conv_w, conv_b, wg, bg, lam, h0, *, seq_len, n_seq):
    n = gate.shape[0]
    tm = seq_len * n_seq
    n_tiles = n // tm
    kern = functools.partial(_rglru_kernel, seq_len=seq_len, n_seq=n_seq)
    return pl.pallas_call(
        kern,
        grid=(n_tiles,),
        in_specs=[
            pl.BlockSpec((tm, D), lambda i: (i, 0)),
            pl.BlockSpec((tm, D), lambda i: (i, 0)),
            pl.BlockSpec((4, D), lambda i: (0, 0)),
            pl.BlockSpec((1, D), lambda i: (0, 0)),
            pl.BlockSpec((2, N_GATE_GROUPS, GATE_GROUP, 2 * GATE_GROUP), lambda i: (0, 0, 0, 0)),
            pl.BlockSpec((2, N_GATE_GROUPS, 1, 2 * GATE_GROUP), lambda i: (0, 0, 0, 0)),
            pl.BlockSpec((2, D), lambda i: (0, 0)),
            pl.BlockSpec((2, None, n_seq, D), lambda i: (0, i, 0, 0)),
        ],
        out_specs=[
            pl.BlockSpec((tm, D), lambda i: (i, 0)),
            pl.BlockSpec((2, None, n_seq, D), lambda i: (0, i, 0, 0)),
        ],
        out_shape=[
            jax.ShapeDtypeStruct((n, D), BF16),
            jax.ShapeDtypeStruct((2, n_tiles, n_seq, D), F32),
        ],
        scratch_shapes=[
            pltpu.VMEM((tm + 16, D), F32),
            pltpu.VMEM((tm, D), F32),
            pltpu.VMEM((D // LANES, tm, LANES), F32),
            pltpu.VMEM((D // LANES, tm, LANES), F32),
            pltpu.VMEM((D // LANES, tm, LANES), F32),
        ],
        compiler_params=_params(("parallel",), 48),
        name="rglru_core",
    )(gate, xr, conv_w, conv_b, wg, bg, lam, h0)


def _lru_coeffs(half_pre, half_x, half_decay):
    tr = jnp.tanh(half_pre[:, :GATE_GROUP])
    ti = jnp.tanh(half_pre[:, GATE_GROUP:])
    log_a = half_decay * tr + half_decay
    a = jnp.exp(log_a)
    v = -jnp.tanh(log_a) * (1.0 + a * a)
    m = jnp.where(v > 0.0, v * lax.rsqrt(v), 0.0)
    return a, (m * half_x) * (1.0 + ti)


def _lru_half_decay(lam_row):
    z = -lam_row
    return (-0.5 * LRU_C) * (jnp.maximum(z, 0.0) + jnp.log1p(jnp.exp(-jnp.abs(z))))


def _rglru_tm_kernel(gate_ref, xr_ref, cw_ref, cb_ref, wg_ref, bg_ref, lam_ref, y_ref, fin_ref,
                     pad_ref, xc_ref, a0_ref, b0_ref, a1_ref, b1_ref, *, seq_len):
    S = TM_GROUP
    R = seq_len * S
    lead = 2 * S
    pad_ref[0:lead, :] = jnp.zeros((lead, GATE_GROUP), F32)
    pad_ref[lead + R:lead + R + S, :] = jnp.zeros((S, GATE_GROUP), F32)
    pad_ref[lead:lead + R, :] = xr_ref[...]
    xc = pad_ref[0:R, :] * cw_ref[0:1, :] + pad_ref[S:S + R, :] * cw_ref[1:2, :]
    xc = xc + xr_ref[...] * cw_ref[2:3, :]
    xc = xc + pad_ref[lead + S:lead + S + R, :] * cw_ref[3:4, :]
    xc = xc + cb_ref[...]
    xc_ref[...] = 0.5 * xc
    xcb = xc.astype(BF16)
    for d, (a_ref, b_ref) in enumerate(((a0_ref, b0_ref), (a1_ref, b1_ref))):
        half_pre = _dot(xcb, wg_ref[d]) + bg_ref[d]
        a, b = _lru_coeffs(half_pre, xc_ref[...], _lru_half_decay(lam_ref[d:d + 1, :]))
        a_ref[...] = a
        b_ref[...] = b

    def step(k, carry):
        hf, hb = carry
        rf = pl.ds(pl.multiple_of(k * S, S), S)
        rb = pl.ds(pl.multiple_of((seq_len - 1 - k) * S, S), S)
        hf = a0_ref[rf, :] * hf + b0_ref[rf, :]
        hb = a1_ref[rb, :] * hb + b1_ref[rb, :]
        b0_ref[rf, :] = hf
        b1_ref[rb, :] = hb
        return hf, hb

    zero = jnp.zeros((S, GATE_GROUP), F32)
    hf, hb = lax.fori_loop(0, seq_len, step, (zero, zero), unroll=8)
    fin_ref[0] = hf
    fin_ref[1] = hb
    y_ref[...] = ((b0_ref[...] + b1_ref[...]) * _gelu_tanh(gate_ref[...])).astype(BF16)


def _rglru_core_tm(gate, xr, conv_w, conv_b, wg, bg, lam, *, seq_len):
    n = gate.shape[0]
    rows = seq_len * TM_GROUP
    n_groups = n // rows
    kern = functools.partial(_rglru_tm_kernel, seq_len=seq_len)
    slab = pl.BlockSpec((rows, GATE_GROUP), lambda gi, q: (gi, q))
    return pl.pallas_call(
        kern,
        grid=(n_groups, N_GATE_GROUPS),
        in_specs=[
            slab,
            slab,
            pl.BlockSpec((4, GATE_GROUP), lambda gi, q: (0, q)),
            pl.BlockSpec((1, GATE_GROUP), lambda gi, q: (0, q)),
            pl.BlockSpec((2, None, GATE_GROUP, 2 * GATE_GROUP), lambda gi, q: (0, q, 0, 0)),
            pl.BlockSpec((2, None, 1, 2 * GATE_GROUP), lambda gi, q: (0, q, 0, 0)),
            pl.BlockSpec((2, GATE_GROUP), lambda gi, q: (0, q)),
        ],
        out_specs=[
            slab,
            pl.BlockSpec((2, None, TM_GROUP, GATE_GROUP), lambda gi, q: (0, gi, 0, q)),
        ],
        out_shape=[
            jax.ShapeDtypeStruct((n, D), BF16),
            jax.ShapeDtypeStruct((2, n_groups, TM_GROUP, D), F32),
        ],
        scratch_shapes=[pltpu.VMEM((rows + 3 * TM_GROUP, GATE_GROUP), F32)]
        + [pltpu.VMEM((rows, GATE_GROUP), F32) for _ in range(5)],
        compiler_params=_params(("parallel", "parallel"), 40),
        name="rglru_core_tm",
    )(gate, xr, conv_w, conv_b, wg, bg, lam)


def _gate_weights(gw_a, gb_a, gw_x, gb_x):
    per = GATE_GROUP // LRU_BLOCK

    def dense(w):
        w = w.reshape(2, N_GATE_GROUPS, per, LRU_BLOCK, LRU_BLOCK)
        eye = jnp.eye(per, dtype=w.dtype)
        full = jnp.einsum('dqpkj,pr->dqpkrj', w, eye)
        return full.reshape(2, N_GATE_GROUPS, GATE_GROUP, GATE_GROUP)

    wg = (0.5 * jnp.concatenate([dense(gw_a), dense(gw_x)], axis=-1)).astype(BF16)
    ba = gb_a.reshape(2, N_GATE_GROUPS, 1, GATE_GROUP)
    bx = gb_x.reshape(2, N_GATE_GROUPS, 1, GATE_GROUP)
    return wg, 0.5 * jnp.concatenate([ba, bx], axis=-1)


def _lam_value(lq_ref, lk_ref, lam_init):
    s = jnp.sum(lq_ref[...] * lk_ref[...], axis=-1, keepdims=True)
    e = jnp.exp(s)
    return e[0:1, :] - e[1:2, :] + lam_init


def _softmax_rows(s):
    e = jnp.exp(s - jnp.max(s, axis=-1, keepdims=True))
    return e * (1.0 / jnp.sum(e, axis=-1, keepdims=True))


def _diff_head(qh, kh_bf16, vh_bf16, lam, g_row, out_scale):
    lane = lax.broadcasted_iota(jnp.int32, qh.shape, 1)
    qs = qh * (MAP_W ** -0.5)
    q0 = jnp.where(lane < MAP_W, qs, 0.0).astype(BF16)
    q1 = jnp.where(lane >= MAP_W, qs, 0.0).astype(BF16)
    p0 = _softmax_rows(_dot_nt(q0, kh_bf16))
    p1 = _softmax_rows(_dot_nt(q1, kh_bf16))
    w = (p0 - lam * p1).astype(BF16)
    o = _dot(w, vh_bf16)
    return _rms(o, g_row) * out_scale


def _attn_ctx_kernel(q_ref, k_ref, v_ref, lq_ref, lk_ref, g_ref, o_ref, *, lam_init):
    lam = _lam_value(lq_ref, lk_ref, lam_init)
    for hd in range(N_HEADS):
        cols = slice(hd * HEAD_W, (hd + 1) * HEAD_W)
        o = _diff_head(q_ref[:, cols], k_ref[:, cols].astype(BF16), v_ref[:, cols].astype(BF16),
                       lam, g_ref[...], 1.0 - lam_init)
        o_ref[:, cols] = o.astype(BF16)


def _attn_ctx(q, k, v, lq, lk, g, *, seq_len, lam_init):
    n = q.shape[0]
    kern = functools.partial(_attn_ctx_kernel, lam_init=lam_init)
    tok = pl.BlockSpec((seq_len, D), lambda i: (i, 0))
    small = lambda shape: pl.BlockSpec(shape, lambda i: (0, 0))
    return pl.pallas_call(
        kern,
        grid=(n // seq_len,),
        in_specs=[tok, tok, tok, small((2, MAP_W)), small((2, MAP_W)), small((1, HEAD_W))],
        out_specs=tok,
        out_shape=jax.ShapeDtypeStruct((n, D), BF16),
        compiler_params=_params(("parallel",), 32),
        name="attn_context",
    )(q, k, v, lq, lk, g)


def _rope(x, cos, sin_signed):
    lane = lax.broadcasted_iota(jnp.int32, x.shape, 1)
    partner = jnp.where(lane % 32 < 16, pltpu.roll(x, HEAD_W - 16, 1), pltpu.roll(x, 16, 1))
    return x * cos + partner * sin_signed


def _attn_lat_kernel(q_ref, k_ref, v_ref, ck_ref, cv_ref, cos_ref, sin_ref, lq_ref, lk_ref, g_ref,
                     o_ref, kall_ref, vall_ref, *, lam_init, past_len, seq_len, tq):
    qb = pl.program_id(1)

    @pl.when(qb == 0)
    def _():
        kall_ref[0:past_len, :] = ck_ref[...].astype(BF16)
        vall_ref[0:past_len, :] = cv_ref[...].astype(BF16)
        vall_ref[past_len:past_len + seq_len, :] = v_ref[...].astype(BF16)
        for hd in range(N_HEADS):
            cols = slice(hd * HEAD_W, (hd + 1) * HEAD_W)
            kr = _rope(k_ref[:, cols], cos_ref[...], sin_ref[...])
            kall_ref[past_len:past_len + seq_len, cols] = kr.astype(BF16)

    lam = _lam_value(lq_ref, lk_ref, lam_init)
    row0 = pl.multiple_of(qb * tq, tq)
    cos_q = cos_ref[pl.ds(row0, tq), :]
    sin_q = sin_ref[pl.ds(row0, tq), :]
    for hd in range(N_HEADS):
        cols = slice(hd * HEAD_W, (hd + 1) * HEAD_W)
        qh = _rope(q_ref[:, cols], cos_q, sin_q)
        o = _diff_head(qh, kall_ref[:, cols], vall_ref[:, cols], lam, g_ref[...], 1.0 - lam_init)
        o_ref[:, cols] = o.astype(BF16)


def _attn_lat(q, k, v, cache_k, cache_v, cos, sin_signed, lq, lk, g, *, seq_len, lam_init):
    n = q.shape[0]
    n_batch = n // seq_len
    past_len = cache_k.shape[1]
    tq = 256
    n_qb = seq_len // tq
    kern = functools.partial(_attn_lat_kernel, lam_init=lam_init, past_len=past_len,
                             seq_len=seq_len, tq=tq)
    seq = pl.BlockSpec((seq_len, D), lambda b, j: (b, 0))
    cache = pl.BlockSpec((None, past_len, D), lambda b, j: (b, 0, 0))
    small = lambda shape: pl.BlockSpec(shape, lambda b, j: (0, 0))
    qblk = pl.BlockSpec((tq, D), lambda b, j: (b * n_qb + j, 0))
    return pl.pallas_call(
        kern,
        grid=(n_batch, n_qb),
        in_specs=[qblk, seq, seq, cache, cache, small((seq_len, HEAD_W)), small((seq_len, HEAD_W)),
                  small((2, MAP_W)), small((2, MAP_W)), small((1, HEAD_W))],
        out_specs=qblk,
        out_shape=jax.ShapeDtypeStruct((n, D), BF16),
        scratch_shapes=[pltpu.VMEM((past_len + seq_len, D), BF16),
                        pltpu.VMEM((past_len + seq_len, D), BF16)],
        compiler_params=_params(("parallel", "arbitrary"), 48),
        name="attn_latent",
    )(q, k, v, cache_k, cache_v, cos, sin_signed, lq, lk, g)


def _rope_tables(seq_len):
    t = np.arange(seq_len)
    pos = np.stack([t // GRID_W, t % GRID_W], axis=-1).astype(np.float32)
    nf = MAP_W // 4
    inv = (np.float32(ROPE_THETA) ** (-np.arange(nf, dtype=np.float32) / nf)).astype(np.float32)
    lane = np.arange(HEAD_W)
    axis = (lane % MAP_W) // 32
    freq = lane % nf
    sign = np.where(lane % 32 < nf, -1.0, 1.0).astype(np.float32)
    pos_l = jnp.asarray(pos)[:, axis]
    ang = pos_l * jnp.asarray(inv)[freq][None, :]
    return jnp.cos(ang), jnp.sin(ang) * jnp.asarray(sign)[None, :]


def _pool_kernel(x_ref, mod_ref, g_ref, w_ref, sc_ref, o_ref, h_ref, lvl_ref, *, seq_len):
    T = seq_len
    R = T + 2 * POOL_PAD
    h = _prenorm(x_ref[...], g_ref[...], mod_ref[0:1, :], mod_ref[1:2, :])
    h_ref[...] = h
    lvl_ref[0:POOL_PAD, :] = jnp.zeros((POOL_PAD, D), F32)
    lvl_ref[POOL_PAD + T:R, :] = jnp.zeros((POOL_PAD, D), F32)
    lvl_ref[POOL_PAD:POOL_PAD + T, :] = h
    t = lax.broadcasted_iota(jnp.int32, (T, 1), 0)
    for gi, win in enumerate(POOL_WINDOWS):
        cols = slice(gi * POOL_GROUP, (gi + 1) * POOL_GROUP)
        half = win // 2
        s = 1
        while s < win:
            lvl_ref[POOL_PAD:R, cols] = lvl_ref[POOL_PAD - s:R - s, cols] + lvl_ref[POOL_PAD:R, cols]
            s *= 2
        wsum = lvl_ref[POOL_PAD + half - 1:POOL_PAD + half - 1 + T, cols]
        cnt = (jnp.minimum(t + half, T) - jnp.maximum(t - half, 0)).astype(F32)
        diff = wsum * (1.0 / cnt) - h_ref[:, cols]
        m = _dot(diff.astype(BF16), w_ref[gi]) * sc_ref[:, cols]
        o_ref[:, cols] = x_ref[:, cols] + mod_ref[2:3, cols] * m


def _pool_mixer(x, mod3, g, w_bf16, scale, *, seq_len):
    n = x.shape[0]
    kern = functools.partial(_pool_kernel, seq_len=seq_len)
    tok = pl.BlockSpec((seq_len, D), lambda i: (i, 0))
    return pl.pallas_call(
        kern,
        grid=(n // seq_len,),
        in_specs=[
            tok,
            pl.BlockSpec((None, 3, D), lambda i: (i * seq_len // (n // mod3.shape[0]), 0, 0)),
            pl.BlockSpec((1, D), lambda i: (0, 0)),
            pl.BlockSpec((len(POOL_WINDOWS), POOL_GROUP, POOL_GROUP), lambda i: (0, 0, 0)),
            pl.BlockSpec((1, D), lambda i: (0, 0)),
        ],
        out_specs=tok,
        out_shape=jax.ShapeDtypeStruct((n, D), F32),
        scratch_shapes=[pltpu.VMEM((seq_len, D), F32),
                        pltpu.VMEM((seq_len + 2 * POOL_PAD, D), F32)],
        compiler_params=_params(("parallel",), 40),
        name="pool_mixer",
    )(x, mod3, g, w_bf16, scale)


def kernel(x_prompt, x_sample, state_rglru, cache_k_diff, cache_v_diff, c, c_ctx, norm_g, w_mod, b_mod, w_ffn_in, w_ffn_out, a_w_in, a_conv_w, a_conv_b, a_gate_w_a, a_gate_b_a, a_gate_w_x, a_gate_b_x, a_lambda, a_w_out, b_w_qkv, b_lam_q, b_lam_k, b_subln_g, b_w_o, c_w_pool, c_scale, final_norm_g):
    n_ctx_seq, ctx_len, _ = x_prompt.shape
    n_lat_seq, lat_len, _ = x_sample.shape
    n_ctx, n_lat = n_ctx_seq * ctx_len, n_lat_seq * lat_len
    xp = x_prompt.reshape(n_ctx, D)
    xs = x_sample.reshape(n_lat, D)

    cond8 = jnp.concatenate([c_ctx[None], c, jnp.zeros((8 - 1 - n_lat_seq, D), F32)], axis=0)
    mod = _adaln_all(cond8, w_mod, b_mod).reshape(DEPTH, 8, N_MOD, D)
    final_g = final_norm_g.reshape(1, D)

    assert ctx_len * FFN_SEQS == FFN_TM and n_ctx_seq % TM_GROUP == 0
    new_states, new_k, new_v = [], [], []
    xp_tm = False
    for l in range(DEPTH):
        mod_p = mod[l, 0:1]
        mod_s = mod[l, 1:1 + n_lat_seq]
        g = norm_g[l]
        kind, j = l % 3, l // 3

        def ffn_both(xp, xs, k, half, final, in_tm, out_tm):
            sl = slice(3 * k, 3 * k + 3)
            xp = _ffn(xp, mod_p[:, sl], g[k:k + 1], w_ffn_in, w_ffn_out, final_g,
                      layer=l, half=half, rows_per_group=n_ctx, final_norm=final,
                      in_tm=in_tm, out_tm=out_tm)
            xs = _ffn(xs, mod_s[:, sl], g[k:k + 1], w_ffn_in, w_ffn_out, final_g,
                      layer=l, half=half, rows_per_group=lat_len, final_norm=final)
            return xp, xs

        want_tm = kind == 0
        xp, xs = ffn_both(xp, xs, 0, 0, False, xp_tm, want_tm)
        xp_tm = want_tm

        mp3, ms3, g1 = mod_p[:, 3:6], mod_s[:, 3:6], g[1:2]
        if kind == 0:
            w_in = a_w_in[j].astype(BF16)
            w_out = a_w_out[j].astype(BF16)
            wg, bg = _gate_weights(a_gate_w_a[j], a_gate_b_a[j], a_gate_w_x[j], a_gate_b_x[j])
            conv_b = a_conv_b[j].reshape(1, D)
            gate_p, xr_p = _prenorm_proj(xp, mp3, g1, w_in, rows_per_group=n_ctx)
            y_p, fin = _rglru_core_tm(gate_p, xr_p, a_conv_w[j], conv_b, wg, bg, a_lambda[j],
                                      seq_len=ctx_len)
            xp = _outproj_residual(y_p, w_out, xp, mp3, rows_per_group=n_ctx)
            new_states.append(jnp.transpose(fin.reshape(2, n_ctx_seq, D), (1, 0, 2)))
            gate_s, xr_s = _prenorm_proj(xs, ms3, g1, w_in, rows_per_group=lat_len)
            h0 = jnp.transpose(state_rglru[:, j].astype(F32), (1, 0, 2)).reshape(2, n_lat_seq, 1, D)
            y_s, _ = _rglru_core(gate_s, xr_s, a_conv_w[j], conv_b, wg, bg, a_lambda[j], h0,
                                 seq_len=lat_len, n_seq=1)
            xs = _outproj_residual(y_s, w_out, xs, ms3, rows_per_group=lat_len)
        elif kind == 1:
            lam_init = 0.8 - 0.6 * math.exp(-0.3 * l)
            w_qkv = b_w_qkv[j].astype(BF16)
            w_o = b_w_o[j].astype(BF16)
            sub_g = b_subln_g[j].reshape(1, HEAD_W)
            q_p, k_p, v_p = _prenorm_proj(xp, mp3, g1, w_qkv, rows_per_group=n_ctx)
            o_p = _attn_ctx(q_p, k_p, v_p, b_lam_q[j], b_lam_k[j], sub_g, seq_len=ctx_len,
                            lam_init=lam_init)
            xp = _outproj_residual(o_p, w_o, xp, mp3, rows_per_group=n_ctx)
            new_k.append(k_p.reshape(n_ctx_seq, ctx_len, N_HEADS, HEAD_W))
            new_v.append(v_p.reshape(n_ctx_seq, ctx_len, N_HEADS, HEAD_W))
            q_s, k_s, v_s = _prenorm_proj(xs, ms3, g1, w_qkv, rows_per_group=lat_len)
            cos, sin_signed = _rope_tables(lat_len)
            past_len = cache_k_diff.shape[2]
            o_s = _attn_lat(q_s, k_s, v_s,
                            cache_k_diff[:, j].reshape(n_lat_seq, past_len, D),
                            cache_v_diff[:, j].reshape(n_lat_seq, past_len, D),
                            cos, sin_signed, b_lam_q[j], b_lam_k[j], sub_g,
                            seq_len=lat_len, lam_init=lam_init)
            xs = _outproj_residual(o_s, w_o, xs, ms3, rows_per_group=lat_len)
        else:
            w_pool = c_w_pool[j].astype(BF16)
            sc = c_scale[j].reshape(1, D)
            xp = _pool_mixer(xp, mp3, g1, w_pool, sc, seq_len=ctx_len)
            xs = _pool_mixer(xs, ms3, g1, w_pool, sc, seq_len=lat_len)

        last = l == DEPTH - 1
        out_tm = xp_tm and not last
        xp, xs = ffn_both(xp, xs, 2, 1, last, xp_tm, out_tm)
        xp_tm = out_tm

    y_prompt = xp.reshape(x_prompt.shape)
    y_sample = xs.reshape(x_sample.shape)
    new_state = jnp.stack(new_states, axis=1).astype(x_prompt.dtype)
    new_cache_k = jnp.stack(new_k, axis=1).astype(x_prompt.dtype)
    new_cache_v = jnp.stack(new_v, axis=1).astype(x_prompt.dtype)
    return (y_prompt, y_sample, new_state, new_cache_k, new_cache_v)
```

```python
import functools
import math

import jax
import jax.numpy as jnp
import numpy as np
from jax import lax
from jax.experimental import pallas as pl
from jax.experimental.pallas import tpu as pltpu

F32 = jnp.float32
BF16 = jnp.bfloat16

D = 1024
DEPTH = 4
N_MOD = 9
EPS = 1e-6
D_FF = 2816
LRU_BLOCK = 64
N_LRU_BLOCKS = 16
LRU_C = 8.0
N_HEADS = 8
LANES = 128
HEAD_W = 128
MAP_W = 64
assert MAP_W ** -0.5 == 0.125
GRID_W = 64
ROPE_THETA = 10000.0
POOL_WINDOWS = (2, 4, 8, 16)
POOL_GROUP = 256
POOL_PAD = 16

GATE_GROUP = 256
N_GATE_GROUPS = D // GATE_GROUP

MIB = 1024 * 1024


def _params(sem, vmem_mib):
    return pltpu.CompilerParams(dimension_semantics=sem, vmem_limit_bytes=vmem_mib * MIB)


def _dot(a, b):
    return jnp.dot(a, b, preferred_element_type=F32)


def _dot_nt(a, b):
    return lax.dot_general(a, b, (((1,), (1,)), ((), ())), preferred_element_type=F32)


def _sigmoid(x):
    return 0.5 + 0.5 * jnp.tanh(0.5 * x)


def _silu(x):
    return x * _sigmoid(x)


def _gelu_tanh(x):
    c = math.sqrt(2.0 / math.pi)
    hx = 0.5 * x
    return hx + hx * jnp.tanh(x * (c + (0.044715 * c) * (x * x)))


def _rms(x, g):
    ms = jnp.mean(x * x, axis=-1, keepdims=True)
    return (x * lax.rsqrt(ms + EPS)) * g


def _prenorm(x, g, shift, scale):
    return _rms(x, g) * (1.0 + scale) + shift


def _mod_kernel(cond_ref, w_ref, b_ref, o_ref):
    s = _silu(cond_ref[...]).astype(BF16)
    o_ref[...] = _dot(s, w_ref[...].astype(BF16)) + b_ref[...]


def _adaln_all(cond8, w_mod, b_mod):
    tn = 1536
    n = N_MOD * D
    return pl.pallas_call(
        _mod_kernel,
        grid=(DEPTH, n // tn),
        in_specs=[
            pl.BlockSpec((8, D), lambda l, j: (0, 0)),
            pl.BlockSpec((None, D, tn), lambda l, j: (l, 0, j)),
            pl.BlockSpec((None, 1, tn), lambda l, j: (l, 0, j)),
        ],
        out_specs=pl.BlockSpec((None, 8, tn), lambda l, j: (l, 0, j)),
        out_shape=jax.ShapeDtypeStruct((DEPTH, 8, n), F32),
        compiler_params=_params(("parallel", "parallel"), 32),
        name="adaln_mod",
    )(cond8, w_mod, b_mod.reshape(DEPTH, 1, n))


FFN_TM = 1024
FFN_FC = 256
FFN_CHUNKS = D_FF // FFN_FC


TM_GROUP = 8


def _ffn_hidden(x_ref, mod_ref, g_ref, h_ref):
    h = _prenorm(x_ref[...], g_ref[...], mod_ref[0:1, :], mod_ref[1:2, :])
    h_ref[...] = h.astype(BF16)


def _ffn_finish(x_ref, mod_ref, fg_ref, o_ref, acc, final_norm):
    y = x_ref[...] + (0.5 * mod_ref[2:3, :]) * acc
    if final_norm:
        y = _rms(y, fg_ref[...])
    o_ref[...] = y


def _ffn_tile(x_ref, mod_ref, g_ref, fg_ref, o_ref, wa, wb, wo, h_ref, u_ref, final_norm):
    _ffn_hidden(x_ref, mod_ref, g_ref, h_ref)
    for j in range(FFN_CHUNKS):
        h = h_ref[...]
        u = _silu(_dot(h, wa[j])) * _dot(h, wb[j])
        u_ref[:, j * FFN_FC:(j + 1) * FFN_FC] = u.astype(BF16)
    _ffn_finish(x_ref, mod_ref, fg_ref, o_ref, _dot(u_ref[...], wo[...]), final_norm)


def _ffn_cast_kernel(x_ref, mod_ref, g_ref, wa_ref, wb_ref, wo_ref, fg_ref, o_ref, wa_o, wb_o, wo_o,
                     wa_s, wb_s, wo_s, h_ref, u_ref, *, final_norm):
    s = pl.program_id(0)
    fc = FFN_FC

    @pl.when(s < FFN_CHUNKS)
    def _():
        wa = wa_ref[...].astype(BF16)
        wb = wb_ref[...].astype(BF16)
        wo = wo_ref[...].astype(BF16)
        wa_s[s] = wa
        wb_s[s] = wb
        wo_s[pl.ds(pl.multiple_of(s * fc, fc), fc), :] = wo
        wa_o[...] = wa
        wb_o[...] = wb
        wo_o[...] = wo

        @pl.when(s == 0)
        def _():
            _ffn_hidden(x_ref, mod_ref, g_ref, h_ref)
            o_ref[...] = jnp.zeros_like(o_ref)

        h = h_ref[...]
        u = (_silu(_dot(h, wa)) * _dot(h, wb)).astype(BF16)
        o_ref[...] += _dot(u, wo)

        @pl.when(s == FFN_CHUNKS - 1)
        def _():
            _ffn_finish(x_ref, mod_ref, fg_ref, o_ref, o_ref[...], final_norm)

    @pl.when(s >= FFN_CHUNKS)
    def _():
        _ffn_tile(x_ref, mod_ref, g_ref, fg_ref, o_ref, wa_s, wb_s, wo_s, h_ref, u_ref, final_norm)


def _ffn_cast(x, mod3, g, w_in, w_out, final_g, *, layer, half, rows_per_group, final_norm):
    n = x.shape[0]
    tm, fc, nc = FFN_TM, FFN_FC, FFN_CHUNKS
    tiles_per_group = rows_per_group // tm
    tile = lambda s: jnp.maximum(s - (nc - 1), 0)
    chunk = lambda s: jnp.minimum(s, nc - 1)
    kern = functools.partial(_ffn_cast_kernel, final_norm=final_norm)
    return pl.pallas_call(
        kern,
        grid=(nc - 1 + n // tm,),
        in_specs=[
            pl.BlockSpec((tm, D), lambda s: (tile(s), 0)),
            pl.BlockSpec((None, 3, D), lambda s: (tile(s) // tiles_per_group, 0, 0)),
            pl.BlockSpec((1, D), lambda s: (0, 0)),
            pl.BlockSpec((None, None, D, fc), lambda s: (layer, half, 0, chunk(s))),
            pl.BlockSpec((None, None, D, fc), lambda s: (layer, half, 0, nc + chunk(s))),
            pl.BlockSpec((None, None, fc, D), lambda s: (layer, half, chunk(s), 0)),
            pl.BlockSpec((1, D), lambda s: (0, 0)),
        ],
        out_specs=[
            pl.BlockSpec((tm, D), lambda s: (tile(s), 0)),
            pl.BlockSpec((None, D, fc), lambda s: (chunk(s), 0, 0)),
            pl.BlockSpec((None, D, fc), lambda s: (chunk(s), 0, 0)),
            pl.BlockSpec((fc, D), lambda s: (chunk(s), 0)),
        ],
        out_shape=[
            jax.ShapeDtypeStruct((n, D), F32),
            jax.ShapeDtypeStruct((nc, D, fc), BF16),
            jax.ShapeDtypeStruct((nc, D, fc), BF16),
            jax.ShapeDtypeStruct((D_FF, D), BF16),
        ],
        scratch_shapes=[
            pltpu.VMEM((nc, D, fc), BF16),
            pltpu.VMEM((nc, D, fc), BF16),
            pltpu.VMEM((D_FF, D), BF16),
            pltpu.VMEM((tm, D), BF16),
            pltpu.VMEM((tm, D_FF), BF16),
        ],
        compiler_params=_params(("arbitrary",), 60),
        name="ffn_cast",
    )(x, mod3, g, w_in, w_in, w_out, final_g)


def _ffn_bf16_kernel(x_ref, mod_ref, g_ref, wa_ref, wb_ref, wo_ref, fg_ref, o_ref, h_ref, u_ref, *,
                     final_norm):
    _ffn_tile(x_ref, mod_ref, g_ref, fg_ref, o_ref, wa_ref, wb_ref, wo_ref, h_ref, u_ref, final_norm)


def _ffn_bf16(x, mod3, g, wa, wb, wo, final_g, *, rows_per_group, final_norm):
    n = x.shape[0]
    tm, fc, nc = FFN_TM, FFN_FC, FFN_CHUNKS
    tiles_per_group = rows_per_group // tm
    kern = functools.partial(_ffn_bf16_kernel, final_norm=final_norm)
    return pl.pallas_call(
        kern,
        grid=(n // tm,),
        in_specs=[
            pl.BlockSpec((tm, D), lambda i: (i, 0)),
            pl.BlockSpec((None, 3, D), lambda i: (i // tiles_per_group, 0, 0)),
            pl.BlockSpec((1, D), lambda i: (0, 0)),
            pl.BlockSpec((nc, D, fc), lambda i: (0, 0, 0)),
            pl.BlockSpec((nc, D, fc), lambda i: (0, 0, 0)),
            pl.BlockSpec((D_FF, D), lambda i: (0, 0)),
            pl.BlockSpec((1, D), lambda i: (0, 0)),
        ],
        out_specs=pl.BlockSpec((tm, D), lambda i: (i, 0)),
        out_shape=jax.ShapeDtypeStruct((n, D), F32),
        scratch_shapes=[
            pltpu.VMEM((tm, D), BF16),
            pltpu.VMEM((tm, D_FF), BF16),
        ],
        compiler_params=_params(("arbitrary",), 60),
        name="ffn_bf16",
    )(x, mod3, g, wa, wb, wo, final_g)


def _proj_kernel(x_ref, mod_ref, g_ref, w_ref, *o_refs):
    h = _prenorm(x_ref[...], g_ref[...], mod_ref[0:1, :], mod_ref[1:2, :]).astype(BF16)
    for k, o_ref in enumerate(o_refs):
        o_ref[...] = _dot(h, w_ref[:, k * D:(k + 1) * D])


def _prenorm_proj(x, mod3, g, w_bf16, *, rows_per_group):
    n = x.shape[0]
    n_out = w_bf16.shape[1] // D
    tm = 512
    tiles_per_group = rows_per_group // tm
    return pl.pallas_call(
        _proj_kernel,
        grid=(n // tm,),
        in_specs=[
            pl.BlockSpec((tm, D), lambda i: (i, 0)),
            pl.BlockSpec((None, 3, D), lambda i: (i // tiles_per_group, 0, 0)),
            pl.BlockSpec((1, D), lambda i: (0, 0)),
            pl.BlockSpec((D, n_out * D), lambda i: (0, 0)),
        ],
        out_specs=[pl.BlockSpec((tm, D), lambda i: (i, 0)) for _ in range(n_out)],
        out_shape=[jax.ShapeDtypeStruct((n, D), F32) for _ in range(n_out)],
        compiler_params=_params(("parallel",), 40),
        name="prenorm_proj",
    )(x, mod3, g, w_bf16)


def _outproj_kernel(y_ref, w_ref, x_ref, mod_ref, o_ref):
    o_ref[...] = x_ref[...] + mod_ref[2:3, :] * _dot(y_ref[...], w_ref[...])


def _outproj_residual(y_bf16, w_bf16, x, mod3, *, rows_per_group):
    n = x.shape[0]
    tm = 512
    tiles_per_group = rows_per_group // tm
    return pl.pallas_call(
        _outproj_kernel,
        grid=(n // tm,),
        in_specs=[
            pl.BlockSpec((tm, D), lambda i: (i, 0)),
            pl.BlockSpec((D, D), lambda i: (0, 0)),
            pl.BlockSpec((tm, D), lambda i: (i, 0)),
            pl.BlockSpec((None, 3, D), lambda i: (i // tiles_per_group, 0, 0)),
        ],
        out_specs=pl.BlockSpec((tm, D), lambda i: (i, 0)),
        out_shape=jax.ShapeDtypeStruct((n, D), F32),
        compiler_params=_params(("parallel",), 32),
        name="outproj_residual",
    )(y_bf16, w_bf16, x, mod3)


def _rglru_kernel(gate_ref, xr_ref, cw_ref, cb_ref, wg_ref, bg_ref, lam_ref, h0_ref,
                  y_ref, fin_ref, pad_ref, xc_ref, a_ref, b_ref, yf_ref, *, seq_len, n_seq):
    tm = seq_len * n_seq
    pad_ref[0:8, :] = jnp.zeros((8, D), F32)
    pad_ref[8 + tm:16 + tm, :] = jnp.zeros((8, D), F32)
    pad_ref[8:8 + tm, :] = xr_ref[...]
    t_loc = lax.broadcasted_iota(jnp.int32, (tm, 1), 0) % seq_len
    xm2 = jnp.where(t_loc >= 2, pad_ref[6:6 + tm, :], 0.0)
    xm1 = jnp.where(t_loc >= 1, pad_ref[7:7 + tm, :], 0.0)
    xp1 = jnp.where(t_loc <= seq_len - 2, pad_ref[9:9 + tm, :], 0.0)
    xc = xm2 * cw_ref[0:1, :] + xm1 * cw_ref[1:2, :]
    xc = xc + xr_ref[...] * cw_ref[2:3, :]
    xc = xc + xp1 * cw_ref[3:4, :]
    xc_ref[...] = xc + cb_ref[...]

    if n_seq == 1:
        rows = lambda t: pl.ds(t, 1)
    else:
        rows = lambda t: pl.ds(t, n_seq, stride=seq_len)

    for d in range(2):
        half_decay = _lru_half_decay(lam_ref[d:d + 1, :])
        for q in range(N_GATE_GROUPS):
            cols = slice(q * GATE_GROUP, (q + 1) * GATE_GROUP)
            xq = xc_ref[:, cols]
            half_pre = _dot(xq.astype(BF16), wg_ref[d, q]) + bg_ref[d, q]
            a, b = _lru_coeffs(half_pre, 0.5 * xq, half_decay[:, cols])
            for c in range(GATE_GROUP // LANES):
                lg = q * (GATE_GROUP // LANES) + c
                a_ref[lg] = a[:, c * LANES:(c + 1) * LANES]
                b_ref[lg] = b[:, c * LANES:(c + 1) * LANES]

        h_dst = yf_ref if d == 0 else b_ref

        def step(k, hs, d=d, h_dst=h_dst):
            t = k if d == 0 else seq_len - 1 - k
            idx = rows(t)
            out = []
            for lg in range(D // LANES):
                h = a_ref[lg, idx, :] * hs[lg] + b_ref[lg, idx, :]
                h_dst[lg, idx, :] = h
                out.append(h)
            return tuple(out)

        h0 = h0_ref[d]
        h_init = tuple(h0[:, lg * LANES:(lg + 1) * LANES] for lg in range(D // LANES))
        h_fin = lax.fori_loop(0, seq_len, step, h_init, unroll=8)
        for lg in range(D // LANES):
            fin_ref[d, :, lg * LANES:(lg + 1) * LANES] = h_fin[lg]

    for lg in range(D // LANES):
        cols = slice(lg * LANES, (lg + 1) * LANES)
        y_ref[:, cols] = ((yf_ref[lg] + b_ref[lg]) * _gelu_tanh(gate_ref[:, cols])).astype(BF16)


def _rglru_core(gate, xr, conv_w, conv_b, wg, bg, lam, h0, *, seq_len, n_seq):
    n = gate.shape[0]
    tm = seq_len * n_seq
    n_tiles = n // tm
    kern = functools.partial(_rglru_kernel, seq_len=seq_len, n_seq=n_seq)
    return pl.pallas_call(
        kern,
        grid=(n_tiles,),
        in_specs=[
            pl.BlockSpec((tm, D), lambda i: (i, 0)),
            pl.BlockSpec((tm, D), lambda i: (i, 0)),
            pl.BlockSpec((4, D), lambda i: (0, 0)),
            pl.BlockSpec((1, D), lambda i: (0, 0)),
            pl.BlockSpec((2, N_GATE_GROUPS, GATE_GROUP, 2 * GATE_GROUP), lambda i: (0, 0, 0, 0)),
            pl.BlockSpec((2, N_GATE_GROUPS, 1, 2 * GATE_GROUP), lambda i: (0, 0, 0, 0)),
            pl.BlockSpec((2, D), lambda i: (0, 0)),
            pl.BlockSpec((2, None, n_seq, D), lambda i: (0, i, 0, 0)),
        ],
        out_specs=[
            pl.BlockSpec((tm, D), lambda i: (i, 0)),
            pl.BlockSpec((2, None, n_seq, D), lambda i: (0, i, 0, 0)),
        ],
        out_shape=[
            jax.ShapeDtypeStruct((n, D), BF16),
            jax.ShapeDtypeStruct((2, n_tiles, n_seq, D), F32),
        ],
        scratch_shapes=[
            pltpu.VMEM((tm + 16, D), F32),
            pltpu.VMEM((tm, D), F32),
            pltpu.VMEM((D // LANES, tm, LANES), F32),
            pltpu.VMEM((D // LANES, tm, LANES), F32),
            pltpu.VMEM((D // LANES, tm, LANES), F32),
        ],
        compiler_params=_params(("parallel",), 48),
        name="rglru_core",
    )(gate, xr, conv_w, conv_b, wg, bg, lam, h0)


def _lru_coeffs(half_pre, half_x, half_decay):
    tr = jnp.tanh(half_pre[:, :GATE_GROUP])
    ti = jnp.tanh(half_pre[:, GATE_GROUP:])
    log_a = half_decay * tr + half_decay
    a = jnp.exp(log_a)
    v = -jnp.tanh(log_a) * (1.0 + a * a)
    m = jnp.where(v > 0.0, v * lax.rsqrt(v), 0.0)
    return a, (m * half_x) * (1.0 + ti)


def _lru_half_decay(lam_row):
    z = -lam_row
    return (-0.5 * LRU_C) * (jnp.maximum(z, 0.0) + jnp.log1p(jnp.exp(-jnp.abs(z))))


def _rglru_tm_kernel(gate_ref, xr_ref, cw_ref, cb_ref, wg_ref, bg_ref, lam_ref, y_ref, fin_ref,
                     pad_ref, xc_ref, a0_ref, b0_ref, a1_ref, b1_ref, ys_ref, *, seq_len):
    S = TM_GROUP
    R = seq_len * S
    lead = 2 * S
    n_lg = GATE_GROUP // LANES
    for lg in range(n_lg):
        cols = slice(lg * LANES, (lg + 1) * LANES)
        pad_ref[lg, 0:lead, :] = jnp.zeros((lead, LANES), F32)
        pad_ref[lg, lead + R:lead + R + S, :] = jnp.zeros((S, LANES), F32)
        for s in range(S):
            pad_ref[lg, pl.ds(lead + s, seq_len, stride=S), :] = xr_ref[s * seq_len:(s + 1) * seq_len, cols]
        xc = pad_ref[lg, 0:R, :] * cw_ref[0:1, cols] + pad_ref[lg, S:S + R, :] * cw_ref[1:2, cols]
        xc = xc + pad_ref[lg, lead:lead + R, :] * cw_ref[2:3, cols]
        xc = xc + pad_ref[lg, lead + S:lead + S + R, :] * cw_ref[3:4, cols]
        xc_ref[:, cols] = xc + cb_ref[:, cols]
    xcb = xc_ref[...].astype(BF16)
    for d, (a_ref, b_ref) in enumerate(((a0_ref, b0_ref), (a1_ref, b1_ref))):
        half_pre = _dot(xcb, wg_ref[d]) + bg_ref[d]
        a, b = _lru_coeffs(half_pre, 0.5 * xc_ref[...], _lru_half_decay(lam_ref[d:d + 1, :]))
        a_ref[...] = a
        b_ref[...] = b

    def step(k, carry):
        hf, hb = carry
        rf = pl.ds(pl.multiple_of(k * S, S), S)
        rb = pl.ds(pl.multiple_of((seq_len - 1 - k) * S, S), S)
        hf = a0_ref[rf, :] * hf + b0_ref[rf, :]
        hb = a1_ref[rb, :] * hb + b1_ref[rb, :]
        b0_ref[rf, :] = hf
        b1_ref[rb, :] = hb
        return hf, hb

    zero = jnp.zeros((S, GATE_GROUP), F32)
    hf, hb = lax.fori_loop(0, seq_len, step, (zero, zero), unroll=8)
    fin_ref[0] = hf
    fin_ref[1] = hb
    for lg in range(n_lg):
        cols = slice(lg * LANES, (lg + 1) * LANES)
        ys_ref[lg] = b0_ref[:, cols] + b1_ref[:, cols]
        for s in range(S):
            rows = slice(s * seq_len, (s + 1) * seq_len)
            ysum = ys_ref[lg, pl.ds(s, seq_len, stride=S), :]
            y_ref[rows, cols] = (ysum * _gelu_tanh(gate_ref[rows, cols])).astype(BF16)


def _rglru_core_tm(gate, xr, conv_w, conv_b, wg, bg, lam, *, seq_len):
    n = gate.shape[0]
    rows = seq_len * TM_GROUP
    n_groups = n // rows
    kern = functools.partial(_rglru_tm_kernel, seq_len=seq_len)
    slab = pl.BlockSpec((rows, GATE_GROUP), lambda gi, q: (gi, q))
    return pl.pallas_call(
        kern,
        grid=(n_groups, N_GATE_GROUPS),
        in_specs=[
            slab,
            slab,
            pl.BlockSpec((4, GATE_GROUP), lambda gi, q: (0, q)),
            pl.BlockSpec((1, GATE_GROUP), lambda gi, q: (0, q)),
            pl.BlockSpec((2, None, GATE_GROUP, 2 * GATE_GROUP), lambda gi, q: (0, q, 0, 0)),
            pl.BlockSpec((2, None, 1, 2 * GATE_GROUP), lambda gi, q: (0, q, 0, 0)),
            pl.BlockSpec((2, GATE_GROUP), lambda gi, q: (0, q)),
        ],
        out_specs=[
            slab,
            pl.BlockSpec((2, None, TM_GROUP, GATE_GROUP), lambda gi, q: (0, gi, 0, q)),
        ],
        out_shape=[
            jax.ShapeDtypeStruct((n, D), BF16),
            jax.ShapeDtypeStruct((2, n_groups, TM_GROUP, D), F32),
        ],
        scratch_shapes=[pltpu.VMEM((GATE_GROUP // LANES, rows + 3 * TM_GROUP, LANES), F32)]
        + [pltpu.VMEM((rows, GATE_GROUP), F32) for _ in range(5)]
        + [pltpu.VMEM((GATE_GROUP // LANES, rows, LANES), F32)],
        compiler_params=_params(("parallel", "parallel"), 40),
        name="rglru_core_tm",
    )(gate, xr, conv_w, conv_b, wg, bg, lam)


def _gate_weights(gw_a, gb_a, gw_x, gb_x):
    per = GATE_GROUP // LRU_BLOCK

    def dense(w):
        w = w.reshape(2, N_GATE_GROUPS, per, LRU_BLOCK, LRU_BLOCK)
        eye = jnp.eye(per, dtype=w.dtype)
        full = jnp.einsum('dqpkj,pr->dqpkrj', w, eye)
        return full.reshape(2, N_GATE_GROUPS, GATE_GROUP, GATE_GROUP)

    wg = (0.5 * jnp.concatenate([dense(gw_a), dense(gw_x)], axis=-1)).astype(BF16)
    ba = gb_a.reshape(2, N_GATE_GROUPS, 1, GATE_GROUP)
    bx = gb_x.reshape(2, N_GATE_GROUPS, 1, GATE_GROUP)
    return wg, 0.5 * jnp.concatenate([ba, bx], axis=-1)


def _lam_value(lq_ref, lk_ref, lam_init):
    s = jnp.sum(lq_ref[...] * lk_ref[...], axis=-1, keepdims=True)
    e = jnp.exp(s)
    return e[0:1, :] - e[1:2, :] + lam_init


def _softmax_rows(s):
    e = jnp.exp(s - jnp.max(s, axis=-1, keepdims=True))
    return e * (1.0 / jnp.sum(e, axis=-1, keepdims=True))


def _diff_head(qh, kh_bf16, vh_bf16, lam, g_row, out_scale):
    lane = lax.broadcasted_iota(jnp.int32, qh.shape, 1)
    qs = qh * (MAP_W ** -0.5)
    q0 = jnp.where(lane < MAP_W, qs, 0.0).astype(BF16)
    q1 = jnp.where(lane >= MAP_W, qs, 0.0).astype(BF16)
    p0 = _softmax_rows(_dot_nt(q0, kh_bf16))
    p1 = _softmax_rows(_dot_nt(q1, kh_bf16))
    w = (p0 - lam * p1).astype(BF16)
    o = _dot(w, vh_bf16)
    return _rms(o, g_row) * out_scale


def _attn_ctx_kernel(q_ref, k_ref, v_ref, lq_ref, lk_ref, g_ref, o_ref, *, lam_init):
    lam = _lam_value(lq_ref, lk_ref, lam_init)
    for hd in range(N_HEADS):
        cols = slice(hd * HEAD_W, (hd + 1) * HEAD_W)
        o = _diff_head(q_ref[:, cols], k_ref[:, cols].astype(BF16), v_ref[:, cols].astype(BF16),
                       lam, g_ref[...], 1.0 - lam_init)
        o_ref[:, cols] = o.astype(BF16)


def _attn_ctx(q, k, v, lq, lk, g, *, seq_len, lam_init):
    n = q.shape[0]
    kern = functools.partial(_attn_ctx_kernel, lam_init=lam_init)
    tok = pl.BlockSpec((seq_len, D), lambda i: (i, 0))
    small = lambda shape: pl.BlockSpec(shape, lambda i: (0, 0))
    return pl.pallas_call(
        kern,
        grid=(n // seq_len,),
        in_specs=[tok, tok, tok, small((2, MAP_W)), small((2, MAP_W)), small((1, HEAD_W))],
        out_specs=tok,
        out_shape=jax.ShapeDtypeStruct((n, D), BF16),
        compiler_params=_params(("parallel",), 32),
        name="attn_context",
    )(q, k, v, lq, lk, g)


def _rope(x, cos, sin_signed):
    lane = lax.broadcasted_iota(jnp.int32, x.shape, 1)
    partner = jnp.where(lane % 32 < 16, pltpu.roll(x, HEAD_W - 16, 1), pltpu.roll(x, 16, 1))
    return x * cos + partner * sin_signed


def _attn_lat_kernel(q_ref, k_ref, v_ref, ck_ref, cv_ref, cos_ref, sin_ref, lq_ref, lk_ref, g_ref,
                     o_ref, kall_ref, vall_ref, *, lam_init, past_len, seq_len, tq):
    qb = pl.program_id(1)

    @pl.when(qb == 0)
    def _():
        kall_ref[0:past_len, :] = ck_ref[...].astype(BF16)
        vall_ref[0:past_len, :] = cv_ref[...].astype(BF16)
        vall_ref[past_len:past_len + seq_len, :] = v_ref[...].astype(BF16)
        for hd in range(N_HEADS):
            cols = slice(hd * HEAD_W, (hd + 1) * HEAD_W)
            kr = _rope(k_ref[:, cols], cos_ref[...], sin_ref[...])
            kall_ref[past_len:past_len + seq_len, cols] = kr.astype(BF16)

    lam = _lam_value(lq_ref, lk_ref, lam_init)
    row0 = pl.multiple_of(qb * tq, tq)
    cos_q = cos_ref[pl.ds(row0, tq), :]
    sin_q = sin_ref[pl.ds(row0, tq), :]
    for hd in range(N_HEADS):
        cols = slice(hd * HEAD_W, (hd + 1) * HEAD_W)
        qh = _rope(q_ref[:, cols], cos_q, sin_q)
        o = _diff_head(qh, kall_ref[:, cols], vall_ref[:, cols], lam, g_ref[...], 1.0 - lam_init)
        o_ref[:, cols] = o.astype(BF16)


def _attn_lat(q, k, v, cache_k, cache_v, cos, sin_signed, lq, lk, g, *, seq_len, lam_init):
    n = q.shape[0]
    n_batch = n // seq_len
    past_len = cache_k.shape[1]
    tq = 256
    n_qb = seq_len // tq
    kern = functools.partial(_attn_lat_kernel, lam_init=lam_init, past_len=past_len,
                             seq_len=seq_len, tq=tq)
    seq = pl.BlockSpec((seq_len, D), lambda b, j: (b, 0))
    cache = pl.BlockSpec((None, past_len, D), lambda b, j: (b, 0, 0))
    small = lambda shape: pl.BlockSpec(shape, lambda b, j: (0, 0))
    qblk = pl.BlockSpec((tq, D), lambda b, j: (b * n_qb + j, 0))
    return pl.pallas_call(
        kern,
        grid=(n_batch, n_qb),
        in_specs=[qblk, seq, seq, cache, cache, small((seq_len, HEAD_W)), small((seq_len, HEAD_W)),
                  small((2, MAP_W)), small((2, MAP_W)), small((1, HEAD_W))],
        out_specs=qblk,
        out_shape=jax.ShapeDtypeStruct((n, D), BF16),
        scratch_shapes=[pltpu.VMEM((past_len + seq_len, D), BF16),
                        pltpu.VMEM((past_len + seq_len, D), BF16)],
        compiler_params=_params(("parallel", "arbitrary"), 48),
        name="attn_latent",
    )(q, k, v, cache_k, cache_v, cos, sin_signed, lq, lk, g)


def _rope_tables(seq_len):
    t = np.arange(seq_len)
    pos = np.stack([t // GRID_W, t % GRID_W], axis=-1).astype(np.float32)
    nf = MAP_W // 4
    inv = (np.float32(ROPE_THETA) ** (-np.arange(nf, dtype=np.float32) / nf)).astype(np.float32)
    lane = np.arange(HEAD_W)
    axis = (lane % MAP_W) // 32
    freq = lane % nf
    sign = np.where(lane % 32 < nf, -1.0, 1.0).astype(np.float32)
    pos_l = jnp.asarray(pos)[:, axis]
    ang = pos_l * jnp.asarray(inv)[freq][None, :]
    return jnp.cos(ang), jnp.sin(ang) * jnp.asarray(sign)[None, :]


def _pool_kernel(x_ref, mod_ref, g_ref, w_ref, sc_ref, o_ref, h_ref, lvl_ref, *, seq_len):
    T = seq_len
    R = T + 2 * POOL_PAD
    h = _prenorm(x_ref[...], g_ref[...], mod_ref[0:1, :], mod_ref[1:2, :])
    h_ref[...] = h
    lvl_ref[0:POOL_PAD, :] = jnp.zeros((POOL_PAD, D), F32)
    lvl_ref[POOL_PAD + T:R, :] = jnp.zeros((POOL_PAD, D), F32)
    lvl_ref[POOL_PAD:POOL_PAD + T, :] = h
    t = lax.broadcasted_iota(jnp.int32, (T, 1), 0)
    for gi, win in enumerate(POOL_WINDOWS):
        cols = slice(gi * POOL_GROUP, (gi + 1) * POOL_GROUP)
        half = win // 2
        s = 1
        while s < win:
            lvl_ref[POOL_PAD:R, cols] = lvl_ref[POOL_PAD - s:R - s, cols] + lvl_ref[POOL_PAD:R, cols]
            s *= 2
        wsum = lvl_ref[POOL_PAD + half - 1:POOL_PAD + half - 1 + T, cols]
        cnt = (jnp.minimum(t + half, T) - jnp.maximum(t - half, 0)).astype(F32)
        diff = wsum * (1.0 / cnt) - h_ref[:, cols]
        m = _dot(diff.astype(BF16), w_ref[gi]) * sc_ref[:, cols]
        o_ref[:, cols] = x_ref[:, cols] + mod_ref[2:3, cols] * m


def _pool_mixer(x, mod3, g, w_bf16, scale, *, seq_len):
    n = x.shape[0]
    kern = functools.partial(_pool_kernel, seq_len=seq_len)
    tok = pl.BlockSpec((seq_len, D), lambda i: (i, 0))
    return pl.pallas_call(
        kern,
        grid=(n // seq_len,),
        in_specs=[
            tok,
            pl.BlockSpec((None, 3, D), lambda i: (i * seq_len // (n // mod3.shape[0]), 0, 0)),
            pl.BlockSpec((1, D), lambda i: (0, 0)),
            pl.BlockSpec((len(POOL_WINDOWS), POOL_GROUP, POOL_GROUP), lambda i: (0, 0, 0)),
            pl.BlockSpec((1, D), lambda i: (0, 0)),
        ],
        out_specs=tok,
        out_shape=jax.ShapeDtypeStruct((n, D), F32),
        scratch_shapes=[pltpu.VMEM((seq_len, D), F32),
                        pltpu.VMEM((seq_len + 2 * POOL_PAD, D), F32)],
        compiler_params=_params(("parallel",), 40),
        name="pool_mixer",
    )(x, mod3, g, w_bf16, scale)


def kernel(x_prompt, x_sample, state_rglru, cache_k_diff, cache_v_diff, c, c_ctx, norm_g, w_mod, b_mod, w_ffn_in, w_ffn_out, a_w_in, a_conv_w, a_conv_b, a_gate_w_a, a_gate_b_a, a_gate_w_x, a_gate_b_x, a_lambda, a_w_out, b_w_qkv, b_lam_q, b_lam_k, b_subln_g, b_w_o, c_w_pool, c_scale, final_norm_g):
    n_ctx_seq, ctx_len, _ = x_prompt.shape
    n_lat_seq, lat_len, _ = x_sample.shape
    n_ctx, n_lat = n_ctx_seq * ctx_len, n_lat_seq * lat_len
    xp = x_prompt.reshape(n_ctx, D)
    xs = x_sample.reshape(n_lat, D)

    cond8 = jnp.concatenate([c_ctx[None], c, jnp.zeros((8 - 1 - n_lat_seq, D), F32)], axis=0)
    mod = _adaln_all(cond8, w_mod, b_mod).reshape(DEPTH, 8, N_MOD, D)
    final_g = final_norm_g.reshape(1, D)

    assert n_ctx_seq % TM_GROUP == 0
    new_states, new_k, new_v = [], [], []
    for l in range(DEPTH):
        mod_p = mod[l, 0:1]
        mod_s = mod[l, 1:1 + n_lat_seq]
        g = norm_g[l]
        kind, j = l % 3, l // 3

        def ffn_both(xp, xs, k, half, final):
            sl = slice(3 * k, 3 * k + 3)
            xp, wa, wb, wo = _ffn_cast(xp, mod_p[:, sl], g[k:k + 1], w_ffn_in, w_ffn_out, final_g,
                                       layer=l, half=half, rows_per_group=n_ctx, final_norm=final)
            xs = _ffn_bf16(xs, mod_s[:, sl], g[k:k + 1], wa, wb, wo, final_g,
                           rows_per_group=lat_len, final_norm=final)
            return xp, xs

        xp, xs = ffn_both(xp, xs, 0, 0, False)

        mp3, ms3, g1 = mod_p[:, 3:6], mod_s[:, 3:6], g[1:2]
        if kind == 0:
            w_in = a_w_in[j].astype(BF16)
            w_out = a_w_out[j].astype(BF16)
            wg, bg = _gate_weights(a_gate_w_a[j], a_gate_b_a[j], a_gate_w_x[j], a_gate_b_x[j])
            conv_b = a_conv_b[j].reshape(1, D)
            gate_p, xr_p = _prenorm_proj(xp, mp3, g1, w_in, rows_per_group=n_ctx)
            y_p, fin = _rglru_core_tm(gate_p, xr_p, a_conv_w[j], conv_b, wg, bg, a_lambda[j],
                                      seq_len=ctx_len)
            xp = _outproj_residual(y_p, w_out, xp, mp3, rows_per_group=n_ctx)
            new_states.append(jnp.transpose(fin.reshape(2, n_ctx_seq, D), (1, 0, 2)))
            gate_s, xr_s = _prenorm_proj(xs, ms3, g1, w_in, rows_per_group=lat_len)
            h0 = jnp.transpose(state_rglru[:, j].astype(F32), (1, 0, 2)).reshape(2, n_lat_seq, 1, D)
            y_s, _ = _rglru_core(gate_s, xr_s, a_conv_w[j], conv_b, wg, bg, a_lambda[j], h0,
                                 seq_len=lat_len, n_seq=1)
            xs = _outproj_residual(y_s, w_out, xs, ms3, rows_per_group=lat_len)
        elif kind == 1:
            lam_init = 0.8 - 0.6 * math.exp(-0.3 * l)
            w_qkv = b_w_qkv[j].astype(BF16)
            w_o = b_w_o[j].astype(BF16)
            sub_g = b_subln_g[j].reshape(1, HEAD_W)
            q_p, k_p, v_p = _prenorm_proj(xp, mp3, g1, w_qkv, rows_per_group=n_ctx)
            o_p = _attn_ctx(q_p, k_p, v_p, b_lam_q[j], b_lam_k[j], sub_g, seq_len=ctx_len,
                            lam_init=lam_init)
            xp = _outproj_residual(o_p, w_o, xp, mp3, rows_per_group=n_ctx)
            new_k.append(k_p.reshape(n_ctx_seq, ctx_len, N_HEADS, HEAD_W))
            new_v.append(v_p.reshape(n_ctx_seq, ctx_len, N_HEADS, HEAD_W))
            q_s, k_s, v_s = _prenorm_proj(xs, ms3, g1, w_qkv, rows_per_group=lat_len)
            cos, sin_signed = _rope_tables(lat_len)
            past_len = cache_k_diff.shape[2]
            o_s = _attn_lat(q_s, k_s, v_s,
                            cache_k_diff[:, j].reshape(n_lat_seq, past_len, D),
                            cache_v_diff[:, j].reshape(n_lat_seq, past_len, D),
                            cos, sin_signed, b_lam_q[j], b_lam_k[j], sub_g,
                            seq_len=lat_len, lam_init=lam_init)
            xs = _outproj_residual(o_s, w_o, xs, ms3, rows_per_group=lat_len)
        else:
            w_pool = c_w_pool[j].astype(BF16)
            sc = c_scale[j].reshape(1, D)
            xp = _pool_mixer(xp, mp3, g1, w_pool, sc, seq_len=ctx_len)
            xs = _pool_mixer(xs, ms3, g1, w_pool, sc, seq_len=lat_len)

        xp, xs = ffn_both(xp, xs, 2, 1, l == DEPTH - 1)

    y_prompt = xp.reshape(x_prompt.shape)
    y_sample = xs.reshape(x_sample.shape)
    new_state = jnp.stack(new_states, axis=1).astype(x_prompt.dtype)
    new_cache_k = jnp.stack(new_k, axis=1).astype(x_prompt.dtype)
    new_cache_v = jnp.stack(new_v, axis=1).astype(x_prompt.dtype)
    return (y_prompt, y_sample, new_state, new_cache_k, new_cache_v)
```

```python
import functools
import math

import jax
import jax.numpy as jnp
import numpy as np
from jax import lax
from jax.experimental import pallas as pl
from jax.experimental.pallas import tpu as pltpu

F32 = jnp.float32
BF16 = jnp.bfloat16

D = 1024
DEPTH = 4
N_MOD = 9
EPS = 1e-6
D_FF = 2816
LRU_BLOCK = 64
N_LRU_BLOCKS = 16
LRU_C = 8.0
N_HEADS = 8
LANES = 128
HEAD_W = 128
MAP_W = 64
assert MAP_W ** -0.5 == 0.125
GRID_W = 64
ROPE_THETA = 10000.0
POOL_WINDOWS = (2, 4, 8, 16)
POOL_GROUP = 256
POOL_PAD = 16

GATE_GROUP = 256
N_GATE_GROUPS = D // GATE_GROUP

MIB = 1024 * 1024


def _params(sem, vmem_mib):
    return pltpu.CompilerParams(dimension_semantics=sem, vmem_limit_bytes=vmem_mib * MIB)


def _dot(a, b):
    return jnp.dot(a, b, preferred_element_type=F32)


def _dot_nt(a, b):
    return lax.dot_general(a, b, (((1,), (1,)), ((), ())), preferred_element_type=F32)


def _sigmoid(x):
    return 0.5 + 0.5 * jnp.tanh(0.5 * x)


def _silu(x):
    return x * _sigmoid(x)


def _gelu_tanh(x):
    c = math.sqrt(2.0 / math.pi)
    hx = 0.5 * x
    return hx + hx * jnp.tanh(x * (c + (0.044715 * c) * (x * x)))


def _rms(x, g):
    ms = jnp.mean(x * x, axis=-1, keepdims=True)
    return (x * lax.rsqrt(ms + EPS)) * g


def _prenorm(x, g, shift, scale):
    return _rms(x, g) * (1.0 + scale) + shift


def _mod_kernel(cond_ref, w_ref, b_ref, o_ref):
    s = _silu(cond_ref[...]).astype(BF16)
    o_ref[...] = _dot(s, w_ref[...].astype(BF16)) + b_ref[...]


def _adaln_all(cond8, w_mod, b_mod):
    tn = 1536
    n = N_MOD * D
    return pl.pallas_call(
        _mod_kernel,
        grid=(DEPTH, n // tn),
        in_specs=[
            pl.BlockSpec((8, D), lambda l, j: (0, 0)),
            pl.BlockSpec((None, D, tn), lambda l, j: (l, 0, j)),
            pl.BlockSpec((None, 1, tn), lambda l, j: (l, 0, j)),
        ],
        out_specs=pl.BlockSpec((None, 8, tn), lambda l, j: (l, 0, j)),
        out_shape=jax.ShapeDtypeStruct((DEPTH, 8, n), F32),
        compiler_params=_params(("parallel", "parallel"), 32),
        name="adaln_mod",
    )(cond8, w_mod, b_mod.reshape(DEPTH, 1, n))


FFN_TM = 1024
FFN_FC = 256
FFN_CHUNKS = D_FF // FFN_FC


TM_GROUP = 8


def _ffn_hidden(x_ref, mod_ref, g_ref, h_ref):
    h = _prenorm(x_ref[...], g_ref[...], mod_ref[0:1, :], mod_ref[1:2, :])
    h_ref[...] = h.astype(BF16)


def _ffn_finish(x_ref, mod_ref, fg_ref, o_ref, acc, final_norm):
    y = x_ref[...] + (0.5 * mod_ref[2:3, :]) * acc
    if final_norm:
        y = _rms(y, fg_ref[...])
    o_ref[...] = y


def _ffn_tile(x_ref, mod_ref, g_ref, fg_ref, o_ref, wa, wb, wo, h_ref, u_ref, final_norm):
    _ffn_hidden(x_ref, mod_ref, g_ref, h_ref)
    for j in range(FFN_CHUNKS):
        h = h_ref[...]
        u = _silu(_dot(h, wa[j])) * _dot(h, wb[j])
        u_ref[:, j * FFN_FC:(j + 1) * FFN_FC] = u.astype(BF16)
    _ffn_finish(x_ref, mod_ref, fg_ref, o_ref, _dot(u_ref[...], wo[...]), final_norm)


def _ffn_cast_kernel(x_ref, mod_ref, g_ref, wa_ref, wb_ref, wo_ref, fg_ref, o_ref, wa_o, wb_o, wo_o,
                     wa_s, wb_s, wo_s, h_ref, u_ref, *, final_norm):
    s = pl.program_id(0)
    fc = FFN_FC

    @pl.when(s < FFN_CHUNKS)
    def _():
        wa = wa_ref[...].astype(BF16)
        wb = wb_ref[...].astype(BF16)
        wo = wo_ref[...].astype(BF16)
        wa_s[s] = wa
        wb_s[s] = wb
        wo_s[pl.ds(pl.multiple_of(s * fc, fc), fc), :] = wo
        wa_o[...] = wa
        wb_o[...] = wb
        wo_o[...] = wo

        @pl.when(s == 0)
        def _():
            _ffn_hidden(x_ref, mod_ref, g_ref, h_ref)
            o_ref[...] = jnp.zeros_like(o_ref)

        h = h_ref[...]
        u = (_silu(_dot(h, wa)) * _dot(h, wb)).astype(BF16)
        o_ref[...] += _dot(u, wo)

        @pl.when(s == FFN_CHUNKS - 1)
        def _():
            _ffn_finish(x_ref, mod_ref, fg_ref, o_ref, o_ref[...], final_norm)

    @pl.when(s >= FFN_CHUNKS)
    def _():
        _ffn_tile(x_ref, mod_ref, g_ref, fg_ref, o_ref, wa_s, wb_s, wo_s, h_ref, u_ref, final_norm)


def _ffn_cast(x, mod3, g, w_in, w_out, final_g, *, layer, half, rows_per_group, final_norm):
    n = x.shape[0]
    tm, fc, nc = FFN_TM, FFN_FC, FFN_CHUNKS
    tiles_per_group = rows_per_group // tm
    tile = lambda s: jnp.maximum(s - (nc - 1), 0)
    chunk = lambda s: jnp.minimum(s, nc - 1)
    kern = functools.partial(_ffn_cast_kernel, final_norm=final_norm)
    return pl.pallas_call(
        kern,
        grid=(nc - 1 + n // tm,),
        in_specs=[
            pl.BlockSpec((tm, D), lambda s: (tile(s), 0)),
            pl.BlockSpec((None, 3, D), lambda s: (tile(s) // tiles_per_group, 0, 0)),
            pl.BlockSpec((1, D), lambda s: (0, 0)),
            pl.BlockSpec((None, None, D, fc), lambda s: (layer, half, 0, chunk(s))),
            pl.BlockSpec((None, None, D, fc), lambda s: (layer, half, 0, nc + chunk(s))),
            pl.BlockSpec((None, None, fc, D), lambda s: (layer, half, chunk(s), 0)),
            pl.BlockSpec((1, D), lambda s: (0, 0)),
        ],
        out_specs=[
            pl.BlockSpec((tm, D), lambda s: (tile(s), 0)),
            pl.BlockSpec((None, D, fc), lambda s: (chunk(s), 0, 0)),
            pl.BlockSpec((None, D, fc), lambda s: (chunk(s), 0, 0)),
            pl.BlockSpec((fc, D), lambda s: (chunk(s), 0)),
        ],
        out_shape=[
            jax.ShapeDtypeStruct((n, D), F32),
            jax.ShapeDtypeStruct((nc, D, fc), BF16),
            jax.ShapeDtypeStruct((nc, D, fc), BF16),
            jax.ShapeDtypeStruct((D_FF, D), BF16),
        ],
        scratch_shapes=[
            pltpu.VMEM((nc, D, fc), BF16),
            pltpu.VMEM((nc, D, fc), BF16),
            pltpu.VMEM((D_FF, D), BF16),
            pltpu.VMEM((tm, D), BF16),
            pltpu.VMEM((tm, D_FF), BF16),
        ],
        compiler_params=_params(("arbitrary",), 60),
        name="ffn_cast",
    )(x, mod3, g, w_in, w_in, w_out, final_g)


def _ffn_bf16_kernel(x_ref, mod_ref, g_ref, wa_ref, wb_ref, wo_ref, fg_ref, o_ref, h_ref, u_ref, *,
                     final_norm):
    _ffn_tile(x_ref, mod_ref, g_ref, fg_ref, o_ref, wa_ref, wb_ref, wo_ref, h_ref, u_ref, final_norm)


def _ffn_bf16(x, mod3, g, wa, wb, wo, final_g, *, rows_per_group, final_norm):
    n = x.shape[0]
    tm, fc, nc = FFN_TM, FFN_FC, FFN_CHUNKS
    tiles_per_group = rows_per_group // tm
    kern = functools.partial(_ffn_bf16_kernel, final_norm=final_norm)
    return pl.pallas_call(
        kern,
        grid=(n // tm,),
        in_specs=[
            pl.BlockSpec((tm, D), lambda i: (i, 0)),
            pl.BlockSpec((None, 3, D), lambda i: (i // tiles_per_group, 0, 0)),
            pl.BlockSpec((1, D), lambda i: (0, 0)),
            pl.BlockSpec((nc, D, fc), lambda i: (0, 0, 0)),
            pl.BlockSpec((nc, D, fc), lambda i: (0, 0, 0)),
            pl.BlockSpec((D_FF, D), lambda i: (0, 0)),
            pl.BlockSpec((1, D), lambda i: (0, 0)),
        ],
        out_specs=pl.BlockSpec((tm, D), lambda i: (i, 0)),
        out_shape=jax.ShapeDtypeStruct((n, D), F32),
        scratch_shapes=[
            pltpu.VMEM((tm, D), BF16),
            pltpu.VMEM((tm, D_FF), BF16),
        ],
        compiler_params=_params(("arbitrary",), 60),
        name="ffn_bf16",
    )(x, mod3, g, wa, wb, wo, final_g)


def _proj_kernel(x_ref, mod_ref, g_ref, w_ref, *o_refs):
    h = _prenorm(x_ref[...], g_ref[...], mod_ref[0:1, :], mod_ref[1:2, :]).astype(BF16)
    for k, o_ref in enumerate(o_refs):
        o_ref[...] = _dot(h, w_ref[:, k * D:(k + 1) * D])


def _prenorm_proj(x, mod3, g, w_bf16, *, rows_per_group):
    n = x.shape[0]
    n_out = w_bf16.shape[1] // D
    tm = 512
    tiles_per_group = rows_per_group // tm
    return pl.pallas_call(
        _proj_kernel,
        grid=(n // tm,),
        in_specs=[
            pl.BlockSpec((tm, D), lambda i: (i, 0)),
            pl.BlockSpec((None, 3, D), lambda i: (i // tiles_per_group, 0, 0)),
            pl.BlockSpec((1, D), lambda i: (0, 0)),
            pl.BlockSpec((D, n_out * D), lambda i: (0, 0)),
        ],
        out_specs=[pl.BlockSpec((tm, D), lambda i: (i, 0)) for _ in range(n_out)],
        out_shape=[jax.ShapeDtypeStruct((n, D), F32) for _ in range(n_out)],
        compiler_params=_params(("parallel",), 40),
        name="prenorm_proj",
    )(x, mod3, g, w_bf16)


def _outproj_kernel(y_ref, w_ref, x_ref, mod_ref, o_ref):
    o_ref[...] = x_ref[...] + mod_ref[2:3, :] * _dot(y_ref[...], w_ref[...])


def _outproj_residual(y_bf16, w_bf16, x, mod3, *, rows_per_group):
    n = x.shape[0]
    tm = 512
    tiles_per_group = rows_per_group // tm
    return pl.pallas_call(
        _outproj_kernel,
        grid=(n // tm,),
        in_specs=[
            pl.BlockSpec((tm, D), lambda i: (i, 0)),
            pl.BlockSpec((D, D), lambda i: (0, 0)),
            pl.BlockSpec((tm, D), lambda i: (i, 0)),
            pl.BlockSpec((None, 3, D), lambda i: (i // tiles_per_group, 0, 0)),
        ],
        out_specs=pl.BlockSpec((tm, D), lambda i: (i, 0)),
        out_shape=jax.ShapeDtypeStruct((n, D), F32),
        compiler_params=_params(("parallel",), 32),
        name="outproj_residual",
    )(y_bf16, w_bf16, x, mod3)


def _rglru_kernel(gate_ref, xr_ref, cw_ref, cb_ref, wg_ref, bg_ref, lam_ref, h0_ref,
                  y_ref, fin_ref, pad_ref, xc_ref, a_ref, b_ref, yf_ref, *, seq_len, n_seq):
    tm = seq_len * n_seq
    pad_ref[0:8, :] = jnp.zeros((8, D), F32)
    pad_ref[8 + tm:16 + tm, :] = jnp.zeros((8, D), F32)
    pad_ref[8:8 + tm, :] = xr_ref[...]
    t_loc = lax.broadcasted_iota(jnp.int32, (tm, 1), 0) % seq_len
    xm2 = jnp.where(t_loc >= 2, pad_ref[6:6 + tm, :], 0.0)
    xm1 = jnp.where(t_loc >= 1, pad_ref[7:7 + tm, :], 0.0)
    xp1 = jnp.where(t_loc <= seq_len - 2, pad_ref[9:9 + tm, :], 0.0)
    xc = xm2 * cw_ref[0:1, :] + xm1 * cw_ref[1:2, :]
    xc = xc + xr_ref[...] * cw_ref[2:3, :]
    xc = xc + xp1 * cw_ref[3:4, :]
    xc_ref[...] = xc + cb_ref[...]

    if n_seq == 1:
        rows = lambda t: pl.ds(t, 1)
    else:
        rows = lambda t: pl.ds(t, n_seq, stride=seq_len)

    for d in range(2):
        half_decay = _lru_half_decay(lam_ref[d:d + 1, :])
        for q in range(N_GATE_GROUPS):
            cols = slice(q * GATE_GROUP, (q + 1) * GATE_GROUP)
            xq = xc_ref[:, cols]
            half_pre = _dot(xq.astype(BF16), wg_ref[d, q]) + bg_ref[d, q]
            a, b = _lru_coeffs(half_pre, 0.5 * xq, half_decay[:, cols])
            for c in range(GATE_GROUP // LANES):
                lg = q * (GATE_GROUP // LANES) + c
                a_ref[lg] = a[:, c * LANES:(c + 1) * LANES]
                b_ref[lg] = b[:, c * LANES:(c + 1) * LANES]

        h_dst = yf_ref if d == 0 else b_ref

        def step(k, hs, d=d, h_dst=h_dst):
            t = k if d == 0 else seq_len - 1 - k
            idx = rows(t)
            out = []
            for lg in range(D // LANES):
                h = a_ref[lg, idx, :] * hs[lg] + b_ref[lg, idx, :]
                h_dst[lg, idx, :] = h
                out.append(h)
            return tuple(out)

        h0 = h0_ref[d]
        h_init = tuple(h0[:, lg * LANES:(lg + 1) * LANES] for lg in range(D // LANES))
        h_fin = lax.fori_loop(0, seq_len, step, h_init, unroll=8)
        for lg in range(D // LANES):
            fin_ref[d, :, lg * LANES:(lg + 1) * LANES] = h_fin[lg]

    for lg in range(D // LANES):
        cols = slice(lg * LANES, (lg + 1) * LANES)
        y_ref[:, cols] = ((yf_ref[lg] + b_ref[lg]) * _gelu_tanh(gate_ref[:, cols])).astype(BF16)


def _rglru_core(gate, xr, conv_w, conv_b, wg, bg, lam, h0, *, seq_len, n_seq):
    n = gate.shape[0]
    tm = seq_len * n_seq
    n_tiles = n // tm
    kern = functools.partial(_rglru_kernel, seq_len=seq_len, n_seq=n_seq)
    return pl.pallas_call(
        kern,
        grid=(n_tiles,),
        in_specs=[
            pl.BlockSpec((tm, D), lambda i: (i, 0)),
            pl.BlockSpec((tm, D), lambda i: (i, 0)),
            pl.BlockSpec((4, D), lambda i: (0, 0)),
            pl.BlockSpec((1, D), lambda i: (0, 0)),
            pl.BlockSpec((2, N_GATE_GROUPS, GATE_GROUP, 2 * GATE_GROUP), lambda i: (0, 0, 0, 0)),
            pl.BlockSpec((2, N_GATE_GROUPS, 1, 2 * GATE_GROUP), lambda i: (0, 0, 0, 0)),
            pl.BlockSpec((2, D), lambda i: (0, 0)),
            pl.BlockSpec((2, None, n_seq, D), lambda i: (0, i, 0, 0)),
        ],
        out_specs=[
            pl.BlockSpec((tm, D), lambda i: (i, 0)),
            pl.BlockSpec((2, None, n_seq, D), lambda i: (0, i, 0, 0)),
        ],
        out_shape=[
            jax.ShapeDtypeStruct((n, D), BF16),
            jax.ShapeDtypeStruct((2, n_tiles, n_seq, D), F32),
        ],
        scratch_shapes=[
            pltpu.VMEM((tm + 16, D), F32),
            pltpu.VMEM((tm, D), F32),
            pltpu.VMEM((D // LANES, tm, LANES), F32),
            pltpu.VMEM((D // LANES, tm, LANES), F32),
            pltpu.VMEM((D // LANES, tm, LANES), F32),
        ],
        compiler_params=_params(("parallel",), 48),
        name="rglru_core",
    )(gate, xr, conv_w, conv_b, wg, bg, lam, h0)


def _lru_coeffs(half_pre, half_x, half_decay):
    tr = jnp.tanh(half_pre[:, :GATE_GROUP])
    ti = jnp.tanh(half_pre[:, GATE_GROUP:])
    log_a = half_decay * tr + half_decay
    a = jnp.exp(log_a)
    v = -jnp.tanh(log_a) * (1.0 + a * a)
    m = jnp.where(v > 0.0, v * lax.rsqrt(v), 0.0)
    return a, (m * half_x) * (1.0 + ti)


def _lru_half_decay(lam_row):
    z = -lam_row
    return (-0.5 * LRU_C) * (jnp.maximum(z, 0.0) + jnp.log1p(jnp.exp(-jnp.abs(z))))


def _rglru_tm_kernel(x_ref, mod_ref, g_ref, win_ref, cw_ref, cb_ref, wg_ref, bg_ref, lam_ref,
                     y_ref, fin_ref,
                     h_ref, gate_ref, pad_ref, xc_ref, a0_ref, b0_ref, a1_ref, b1_ref, *, seq_len):
    S = TM_GROUP
    R = seq_len * S
    lead = 2 * S
    n_lg = GATE_GROUP // LANES

    @pl.when(pl.program_id(1) == 0)
    def _():
        h = _prenorm(x_ref[...], g_ref[...], mod_ref[0:1, :], mod_ref[1:2, :])
        h_ref[...] = h.astype(BF16)

    gate_ref[...] = _dot(h_ref[...], win_ref[0])
    xr = _dot(h_ref[...], win_ref[1])
    for lg in range(n_lg):
        cols = slice(lg * LANES, (lg + 1) * LANES)
        pad_ref[lg, 0:lead, :] = jnp.zeros((lead, LANES), F32)
        pad_ref[lg, lead + R:lead + R + S, :] = jnp.zeros((S, LANES), F32)
        for s in range(S):
            pad_ref[lg, pl.ds(lead + s, seq_len, stride=S), :] = xr[s * seq_len:(s + 1) * seq_len, cols]
        xc = pad_ref[lg, 0:R, :] * cw_ref[0:1, cols] + pad_ref[lg, S:S + R, :] * cw_ref[1:2, cols]
        xc = xc + pad_ref[lg, lead:lead + R, :] * cw_ref[2:3, cols]
        xc = xc + pad_ref[lg, lead + S:lead + S + R, :] * cw_ref[3:4, cols]
        xc_ref[:, cols] = xc + cb_ref[:, cols]
    xcb = xc_ref[...].astype(BF16)
    for d, (a_ref, b_ref) in enumerate(((a0_ref, b0_ref), (a1_ref, b1_ref))):
        half_pre = _dot(xcb, wg_ref[d]) + bg_ref[d]
        a, b = _lru_coeffs(half_pre, 0.5 * xc_ref[...], _lru_half_decay(lam_ref[d:d + 1, :]))
        a_ref[...] = a
        b_ref[...] = b

    def step(k, carry):
        hf, hb = carry
        rf = pl.ds(pl.multiple_of(k * S, S), S)
        rb = pl.ds(pl.multiple_of((seq_len - 1 - k) * S, S), S)
        hf = a0_ref[rf, :] * hf + b0_ref[rf, :]
        hb = a1_ref[rb, :] * hb + b1_ref[rb, :]
        b0_ref[rf, :] = hf
        b1_ref[rb, :] = hb
        return hf, hb

    zero = jnp.zeros((S, GATE_GROUP), F32)
    hf, hb = lax.fori_loop(0, seq_len, step, (zero, zero), unroll=8)
    fin_ref[0] = hf
    fin_ref[1] = hb
    for lg in range(n_lg):
        cols = slice(lg * LANES, (lg + 1) * LANES)
        pad_ref[lg, 0:R, :] = b0_ref[:, cols] + b1_ref[:, cols]
        for s in range(S):
            rows = slice(s * seq_len, (s + 1) * seq_len)
            ysum = pad_ref[lg, pl.ds(s, seq_len, stride=S), :]
            y_ref[rows, cols] = (ysum * _gelu_tanh(gate_ref[rows, cols])).astype(BF16)


def _rglru_core_tm(x, mod3, g, w_in_bf16, conv_w, conv_b, wg, bg, lam, *, seq_len):
    n = x.shape[0]
    rows = seq_len * TM_GROUP
    n_groups = n // rows
    kern = functools.partial(_rglru_tm_kernel, seq_len=seq_len)
    w_slabs = jnp.transpose(w_in_bf16.reshape(D, 2, N_GATE_GROUPS, GATE_GROUP), (1, 2, 0, 3))
    return pl.pallas_call(
        kern,
        grid=(n_groups, N_GATE_GROUPS),
        in_specs=[
            pl.BlockSpec((rows, D), lambda gi, q: (gi, 0)),
            pl.BlockSpec((None, 3, D), lambda gi, q: (0, 0, 0)),
            pl.BlockSpec((1, D), lambda gi, q: (0, 0)),
            pl.BlockSpec((2, None, D, GATE_GROUP), lambda gi, q: (0, q, 0, 0)),
            pl.BlockSpec((4, GATE_GROUP), lambda gi, q: (0, q)),
            pl.BlockSpec((1, GATE_GROUP), lambda gi, q: (0, q)),
            pl.BlockSpec((2, None, GATE_GROUP, 2 * GATE_GROUP), lambda gi, q: (0, q, 0, 0)),
            pl.BlockSpec((2, None, 1, 2 * GATE_GROUP), lambda gi, q: (0, q, 0, 0)),
            pl.BlockSpec((2, GATE_GROUP), lambda gi, q: (0, q)),
        ],
        out_specs=[
            pl.BlockSpec((rows, GATE_GROUP), lambda gi, q: (gi, q)),
            pl.BlockSpec((2, None, TM_GROUP, GATE_GROUP), lambda gi, q: (0, gi, 0, q)),
        ],
        out_shape=[
            jax.ShapeDtypeStruct((n, D), BF16),
            jax.ShapeDtypeStruct((2, n_groups, TM_GROUP, D), F32),
        ],
        scratch_shapes=[pltpu.VMEM((rows, D), BF16), pltpu.VMEM((rows, GATE_GROUP), F32)]
        + [pltpu.VMEM((GATE_GROUP // LANES, rows + 3 * TM_GROUP, LANES), F32)]
        + [pltpu.VMEM((rows, GATE_GROUP), F32) for _ in range(5)],
        compiler_params=_params(("parallel", "arbitrary"), 56),
        name="rglru_core_tm",
    )(x, mod3, g, w_slabs, conv_w, conv_b, wg, bg, lam)


def _gate_weights(gw_a, gb_a, gw_x, gb_x):
    per = GATE_GROUP // LRU_BLOCK

    def dense(w):
        w = w.reshape(2, N_GATE_GROUPS, per, LRU_BLOCK, LRU_BLOCK)
        eye = jnp.eye(per, dtype=w.dtype)
        full = jnp.einsum('dqpkj,pr->dqpkrj', w, eye)
        return full.reshape(2, N_GATE_GROUPS, GATE_GROUP, GATE_GROUP)

    wg = (0.5 * jnp.concatenate([dense(gw_a), dense(gw_x)], axis=-1)).astype(BF16)
    ba = gb_a.reshape(2, N_GATE_GROUPS, 1, GATE_GROUP)
    bx = gb_x.reshape(2, N_GATE_GROUPS, 1, GATE_GROUP)
    return wg, 0.5 * jnp.concatenate([ba, bx], axis=-1)


def _lam_value(lq_ref, lk_ref, lam_init):
    s = jnp.sum(lq_ref[...] * lk_ref[...], axis=-1, keepdims=True)
    e = jnp.exp(s)
    return e[0:1, :] - e[1:2, :] + lam_init


def _softmax_av(s, v_ones):
    e = jnp.exp(s - jnp.max(s, axis=-1, keepdims=True)).astype(BF16)
    ov = _dot(e, v_ones)
    return ov[:, :HEAD_W] * (1.0 / ov[:, HEAD_W:])


def _diff_head(qh, kh_bf16, vh_bf16, lam, g_row, out_scale):
    lane = lax.broadcasted_iota(jnp.int32, qh.shape, 1)
    qs = qh * (MAP_W ** -0.5)
    q0 = jnp.where(lane < MAP_W, qs, 0.0).astype(BF16)
    q1 = jnp.where(lane >= MAP_W, qs, 0.0).astype(BF16)
    v_ones = jnp.concatenate([vh_bf16, jnp.ones_like(vh_bf16)], axis=1)
    o = _softmax_av(_dot_nt(q0, kh_bf16), v_ones) - lam * _softmax_av(_dot_nt(q1, kh_bf16), v_ones)
    return _rms(o, g_row) * out_scale


def _attn_ctx_kernel(q_ref, k_ref, v_ref, lq_ref, lk_ref, g_ref, o_ref, *, lam_init):
    lam = _lam_value(lq_ref, lk_ref, lam_init)
    for hd in range(N_HEADS):
        cols = slice(hd * HEAD_W, (hd + 1) * HEAD_W)
        o = _diff_head(q_ref[:, cols], k_ref[:, cols].astype(BF16), v_ref[:, cols].astype(BF16),
                       lam, g_ref[...], 1.0 - lam_init)
        o_ref[:, cols] = o.astype(BF16)


def _attn_ctx(q, k, v, lq, lk, g, *, seq_len, lam_init):
    n = q.shape[0]
    kern = functools.partial(_attn_ctx_kernel, lam_init=lam_init)
    tok = pl.BlockSpec((seq_len, D), lambda i: (i, 0))
    small = lambda shape: pl.BlockSpec(shape, lambda i: (0, 0))
    return pl.pallas_call(
        kern,
        grid=(n // seq_len,),
        in_specs=[tok, tok, tok, small((2, MAP_W)), small((2, MAP_W)), small((1, HEAD_W))],
        out_specs=tok,
        out_shape=jax.ShapeDtypeStruct((n, D), BF16),
        compiler_params=_params(("parallel",), 32),
        name="attn_context",
    )(q, k, v, lq, lk, g)


def _rope(x, cos, sin_signed):
    lane = lax.broadcasted_iota(jnp.int32, x.shape, 1)
    partner = jnp.where(lane % 32 < 16, pltpu.roll(x, HEAD_W - 16, 1), pltpu.roll(x, 16, 1))
    return x * cos + partner * sin_signed


def _attn_lat_kernel(q_ref, k_ref, v_ref, ck_ref, cv_ref, cos_ref, sin_ref, lq_ref, lk_ref, g_ref,
                     o_ref, kall_ref, vall_ref, *, lam_init, past_len, seq_len, tq):
    qb = pl.program_id(1)

    @pl.when(qb == 0)
    def _():
        kall_ref[0:past_len, :] = ck_ref[...].astype(BF16)
        vall_ref[0:past_len, :] = cv_ref[...].astype(BF16)
        vall_ref[past_len:past_len + seq_len, :] = v_ref[...].astype(BF16)
        for hd in range(N_HEADS):
            cols = slice(hd * HEAD_W, (hd + 1) * HEAD_W)
            kr = _rope(k_ref[:, cols], cos_ref[...], sin_ref[...])
            kall_ref[past_len:past_len + seq_len, cols] = kr.astype(BF16)

    lam = _lam_value(lq_ref, lk_ref, lam_init)
    row0 = pl.multiple_of(qb * tq, tq)
    cos_q = cos_ref[pl.ds(row0, tq), :]
    sin_q = sin_ref[pl.ds(row0, tq), :]
    for hd in range(N_HEADS):
        cols = slice(hd * HEAD_W, (hd + 1) * HEAD_W)
        qh = _rope(q_ref[:, cols], cos_q, sin_q)
        o = _diff_head(qh, kall_ref[:, cols], vall_ref[:, cols], lam, g_ref[...], 1.0 - lam_init)
        o_ref[:, cols] = o.astype(BF16)


def _attn_lat(q, k, v, cache_k, cache_v, cos, sin_signed, lq, lk, g, *, seq_len, lam_init):
    n = q.shape[0]
    n_batch = n // seq_len
    past_len = cache_k.shape[1]
    tq = 256
    n_qb = seq_len // tq
    kern = functools.partial(_attn_lat_kernel, lam_init=lam_init, past_len=past_len,
                             seq_len=seq_len, tq=tq)
    seq = pl.BlockSpec((seq_len, D), lambda b, j: (b, 0))
    cache = pl.BlockSpec((None, past_len, D), lambda b, j: (b, 0, 0))
    small = lambda shape: pl.BlockSpec(shape, lambda b, j: (0, 0))
    qblk = pl.BlockSpec((tq, D), lambda b, j: (b * n_qb + j, 0))
    return pl.pallas_call(
        kern,
        grid=(n_batch, n_qb),
        in_specs=[qblk, seq, seq, cache, cache, small((seq_len, HEAD_W)), small((seq_len, HEAD_W)),
                  small((2, MAP_W)), small((2, MAP_W)), small((1, HEAD_W))],
        out_specs=qblk,
        out_shape=jax.ShapeDtypeStruct((n, D), BF16),
        scratch_shapes=[pltpu.VMEM((past_len + seq_len, D), BF16),
                        pltpu.VMEM((past_len + seq_len, D), BF16)],
        compiler_params=_params(("parallel", "arbitrary"), 48),
        name="attn_latent",
    )(q, k, v, cache_k, cache_v, cos, sin_signed, lq, lk, g)


def _rope_tables(seq_len):
    t = np.arange(seq_len)
    pos = np.stack([t // GRID_W, t % GRID_W], axis=-1).astype(np.float32)
    nf = MAP_W // 4
    inv = (np.float32(ROPE_THETA) ** (-np.arange(nf, dtype=np.float32) / nf)).astype(np.float32)
    lane = np.arange(HEAD_W)
    axis = (lane % MAP_W) // 32
    freq = lane % nf
    sign = np.where(lane % 32 < nf, -1.0, 1.0).astype(np.float32)
    pos_l = jnp.asarray(pos)[:, axis]
    ang = pos_l * jnp.asarray(inv)[freq][None, :]
    return jnp.cos(ang), jnp.sin(ang) * jnp.asarray(sign)[None, :]


def _pool_kernel(x_ref, mod_ref, g_ref, w_ref, sc_ref, o_ref, h_ref, lvl_ref, *, seq_len):
    T = seq_len
    R = T + 2 * POOL_PAD
    h = _prenorm(x_ref[...], g_ref[...], mod_ref[0:1, :], mod_ref[1:2, :])
    h_ref[...] = h
    lvl_ref[0:POOL_PAD, :] = jnp.zeros((POOL_PAD, D), F32)
    lvl_ref[POOL_PAD + T:R, :] = jnp.zeros((POOL_PAD, D), F32)
    lvl_ref[POOL_PAD:POOL_PAD + T, :] = h
    t = lax.broadcasted_iota(jnp.int32, (T, 1), 0)
    for gi, win in enumerate(POOL_WINDOWS):
        cols = slice(gi * POOL_GROUP, (gi + 1) * POOL_GROUP)
        half = win // 2
        s = 1
        while s < win:
            lvl_ref[POOL_PAD:R, cols] = lvl_ref[POOL_PAD - s:R - s, cols] + lvl_ref[POOL_PAD:R, cols]
            s *= 2
        wsum = lvl_ref[POOL_PAD + half - 1:POOL_PAD + half - 1 + T, cols]
        cnt = (jnp.minimum(t + half, T) - jnp.maximum(t - half, 0)).astype(F32)
        diff = wsum * (1.0 / cnt) - h_ref[:, cols]
        m = _dot(diff.astype(BF16), w_ref[gi]) * sc_ref[:, cols]
        o_ref[:, cols] = x_ref[:, cols] + mod_ref[2:3, cols] * m


def _pool_mixer(x, mod3, g, w_bf16, scale, *, seq_len):
    n = x.shape[0]
    kern = functools.partial(_pool_kernel, seq_len=seq_len)
    tok = pl.BlockSpec((seq_len, D), lambda i: (i, 0))
    return pl.pallas_call(
        kern,
        grid=(n // seq_len,),
        in_specs=[
            tok,
            pl.BlockSpec((None, 3, D), lambda i: (i * seq_len // (n // mod3.shape[0]), 0, 0)),
            pl.BlockSpec((1, D), lambda i: (0, 0)),
            pl.BlockSpec((len(POOL_WINDOWS), POOL_GROUP, POOL_GROUP), lambda i: (0, 0, 0)),
            pl.BlockSpec((1, D), lambda i: (0, 0)),
        ],
        out_specs=tok,
        out_shape=jax.ShapeDtypeStruct((n, D), F32),
        scratch_shapes=[pltpu.VMEM((seq_len, D), F32),
                        pltpu.VMEM((seq_len + 2 * POOL_PAD, D), F32)],
        compiler_params=_params(("parallel",), 40),
        name="pool_mixer",
    )(x, mod3, g, w_bf16, scale)


def kernel(x_prompt, x_sample, state_rglru, cache_k_diff, cache_v_diff, c, c_ctx, norm_g, w_mod, b_mod, w_ffn_in, w_ffn_out, a_w_in, a_conv_w, a_conv_b, a_gate_w_a, a_gate_b_a, a_gate_w_x, a_gate_b_x, a_lambda, a_w_out, b_w_qkv, b_lam_q, b_lam_k, b_subln_g, b_w_o, c_w_pool, c_scale, final_norm_g):
    n_ctx_seq, ctx_len, _ = x_prompt.shape
    n_lat_seq, lat_len, _ = x_sample.shape
    n_ctx, n_lat = n_ctx_seq * ctx_len, n_lat_seq * lat_len
    xp = x_prompt.reshape(n_ctx, D)
    xs = x_sample.reshape(n_lat, D)

    cond8 = jnp.concatenate([c_ctx[None], c, jnp.zeros((8 - 1 - n_lat_seq, D), F32)], axis=0)
    mod = _adaln_all(cond8, w_mod, b_mod).reshape(DEPTH, 8, N_MOD, D)
    final_g = final_norm_g.reshape(1, D)

    assert n_ctx_seq % TM_GROUP == 0
    new_states, new_k, new_v = [], [], []
    for l in range(DEPTH):
        mod_p = mod[l, 0:1]
        mod_s = mod[l, 1:1 + n_lat_seq]
        g = norm_g[l]
        kind, j = l % 3, l // 3

        def ffn_both(xp, xs, k, half, final):
            sl = slice(3 * k, 3 * k + 3)
            xp, wa, wb, wo = _ffn_cast(xp, mod_p[:, sl], g[k:k + 1], w_ffn_in, w_ffn_out, final_g,
                                       layer=l, half=half, rows_per_group=n_ctx, final_norm=final)
            xs = _ffn_bf16(xs, mod_s[:, sl], g[k:k + 1], wa, wb, wo, final_g,
                           rows_per_group=lat_len, final_norm=final)
            return xp, xs

        xp, xs = ffn_both(xp, xs, 0, 0, False)

        mp3, ms3, g1 = mod_p[:, 3:6], mod_s[:, 3:6], g[1:2]
        if kind == 0:
            w_in = a_w_in[j].astype(BF16)
            w_out = a_w_out[j].astype(BF16)
            wg, bg = _gate_weights(a_gate_w_a[j], a_gate_b_a[j], a_gate_w_x[j], a_gate_b_x[j])
            conv_b = a_conv_b[j].reshape(1, D)
            y_p, fin = _rglru_core_tm(xp, mp3, g1, w_in, a_conv_w[j], conv_b, wg, bg, a_lambda[j],
                                      seq_len=ctx_len)
            xp = _outproj_residual(y_p, w_out, xp, mp3, rows_per_group=n_ctx)
            new_states.append(jnp.transpose(fin.reshape(2, n_ctx_seq, D), (1, 0, 2)))
            gate_s, xr_s = _prenorm_proj(xs, ms3, g1, w_in, rows_per_group=lat_len)
            h0 = jnp.transpose(state_rglru[:, j].astype(F32), (1, 0, 2)).reshape(2, n_lat_seq, 1, D)
            y_s, _ = _rglru_core(gate_s, xr_s, a_conv_w[j], conv_b, wg, bg, a_lambda[j], h0,
                                 seq_len=lat_len, n_seq=1)
            xs = _outproj_residual(y_s, w_out, xs, ms3, rows_per_group=lat_len)
        elif kind == 1:
            lam_init = 0.8 - 0.6 * math.exp(-0.3 * l)
            w_qkv = b_w_qkv[j].astype(BF16)
            w_o = b_w_o[j].astype(BF16)
            sub_g = b_subln_g[j].reshape(1, HEAD_W)
            q_p, k_p, v_p = _prenorm_proj(xp, mp3, g1, w_qkv, rows_per_group=n_ctx)
            o_p = _attn_ctx(q_p, k_p, v_p, b_lam_q[j], b_lam_k[j], sub_g, seq_len=ctx_len,
                            lam_init=lam_init)
            xp = _outproj_residual(o_p, w_o, xp, mp3, rows_per_group=n_ctx)
            new_k.append(k_p.reshape(n_ctx_seq, ctx_len, N_HEADS, HEAD_W))
            new_v.append(v_p.reshape(n_ctx_seq, ctx_len, N_HEADS, HEAD_W))
            q_s, k_s, v_s = _prenorm_proj(xs, ms3, g1, w_qkv, rows_per_group=lat_len)
            cos, sin_signed = _rope_tables(lat_len)
            past_len = cache_k_diff.shape[2]
            o_s = _attn_lat(q_s, k_s, v_s,
                            cache_k_diff[:, j].reshape(n_lat_seq, past_len, D),
                            cache_v_diff[:, j].reshape(n_lat_seq, past_len, D),
                            cos, sin_signed, b_lam_q[j], b_lam_k[j], sub_g,
                            seq_len=lat_len, lam_init=lam_init)
            xs = _outproj_residual(o_s, w_o, xs, ms3, rows_per_group=lat_len)
        else:
            w_pool = c_w_pool[j].astype(BF16)
            sc = c_scale[j].reshape(1, D)
            xp = _pool_mixer(xp, mp3, g1, w_pool, sc, seq_len=ctx_len)
            xs = _pool_mixer(xs, ms3, g1, w_pool, sc, seq_len=lat_len)

        xp, xs = ffn_both(xp, xs, 2, 1, l == DEPTH - 1)

    y_prompt = xp.reshape(x_prompt.shape)
    y_sample = xs.reshape(x_sample.shape)
    new_state = jnp.stack(new_states, axis=1).astype(x_prompt.dtype)
    new_cache_k = jnp.stack(new_k, axis=1).astype(x_prompt.dtype)
    new_cache_v = jnp.stack(new_v, axis=1).astype(x_prompt.dtype)
    return (y_prompt, y_sample, new_state, new_cache_k, new_cache_v)
```

```python
import functools
import math

import jax
import jax.numpy as jnp
import numpy as np
from jax import lax
from jax.experimental import pallas as pl
from jax.experimental.pallas import tpu as pltpu

F32 = jnp.float32
BF16 = jnp.bfloat16

D = 1024
DEPTH = 4
N_MOD = 9
EPS = 1e-6
D_FF = 2816
LRU_BLOCK = 64
N_LRU_BLOCKS = 16
LRU_C = 8.0
N_HEADS = 8
LANES = 128
HEAD_W = 128
MAP_W = 64
assert MAP_W ** -0.5 == 0.125
GRID_W = 64
ROPE_THETA = 10000.0
POOL_WINDOWS = (2, 4, 8, 16)
POOL_GROUP = 256
POOL_PAD = 16

GATE_GROUP = 256
N_GATE_GROUPS = D // GATE_GROUP

MIB = 1024 * 1024


def _params(sem, vmem_mib):
    return pltpu.CompilerParams(dimension_semantics=sem, vmem_limit_bytes=vmem_mib * MIB)


def _dot(a, b):
    return jnp.dot(a, b, preferred_element_type=F32)


def _dot_nt(a, b):
    return lax.dot_general(a, b, (((1,), (1,)), ((), ())), preferred_element_type=F32)


def _sigmoid(x):
    return 0.5 + 0.5 * jnp.tanh(0.5 * x)


def _silu(x):
    return x * _sigmoid(x)


def _gelu_tanh(x):
    c = math.sqrt(2.0 / math.pi)
    hx = 0.5 * x
    return hx + hx * jnp.tanh(x * (c + (0.044715 * c) * (x * x)))


def _rms(x, g):
    ms = jnp.mean(x * x, axis=-1, keepdims=True)
    return (x * lax.rsqrt(ms + EPS)) * g


def _prenorm(x, g, shift, scale):
    return _rms(x, g) * (1.0 + scale) + shift


def _mod_kernel(cond_ref, w_ref, b_ref, o_ref):
    s = _silu(cond_ref[...]).astype(BF16)
    o_ref[...] = _dot(s, w_ref[...].astype(BF16)) + b_ref[...]


def _adaln_all(cond8, w_mod, b_mod):
    tn = 1536
    n = N_MOD * D
    return pl.pallas_call(
        _mod_kernel,
        grid=(DEPTH, n // tn),
        in_specs=[
            pl.BlockSpec((8, D), lambda l, j: (0, 0)),
            pl.BlockSpec((None, D, tn), lambda l, j: (l, 0, j)),
            pl.BlockSpec((None, 1, tn), lambda l, j: (l, 0, j)),
        ],
        out_specs=pl.BlockSpec((None, 8, tn), lambda l, j: (l, 0, j)),
        out_shape=jax.ShapeDtypeStruct((DEPTH, 8, n), F32),
        compiler_params=_params(("parallel", "parallel"), 32),
        name="adaln_mod",
    )(cond8, w_mod, b_mod.reshape(DEPTH, 1, n))


FFN_TM = 1024
FFN_FC = 256
FFN_CHUNKS = D_FF // FFN_FC


TM_GROUP = 8
PROJ_TM = 512


def _ffn_hidden(x_ref, mod_ref, g_ref, h_ref):
    h = _prenorm(x_ref[...], g_ref[...], mod_ref[0:1, :], mod_ref[1:2, :])
    h_ref[...] = h.astype(BF16)


def _ffn_finish(x_ref, mod_ref, fg_ref, o_ref, acc, final_norm):
    y = x_ref[...] + (0.5 * mod_ref[2:3, :]) * acc
    if final_norm:
        y = _rms(y, fg_ref[...])
    o_ref[...] = y


def _ffn_tile(x_ref, mod_ref, g_ref, fg_ref, o_ref, wa, wb, wo, h_ref, u_ref, final_norm):
    _ffn_hidden(x_ref, mod_ref, g_ref, h_ref)
    for j in range(FFN_CHUNKS):
        h = h_ref[...]
        u = _silu(_dot(h, wa[j])) * _dot(h, wb[j])
        u_ref[:, j * FFN_FC:(j + 1) * FFN_FC] = u.astype(BF16)
    _ffn_finish(x_ref, mod_ref, fg_ref, o_ref, _dot(u_ref[...], wo[...]), final_norm)


def _ffn_cast_kernel(x_ref, mod_ref, g_ref, wa_ref, wb_ref, wo_ref, fg_ref, o_ref, *rest,
                     final_norm, emit_weights):
    weight_outs, (wa_s, wb_s, wo_s, h_ref, u_ref) = rest[:-5], rest[-5:]
    s = pl.program_id(0)
    fc = FFN_FC

    @pl.when(s < FFN_CHUNKS)
    def _():
        wa = wa_ref[...].astype(BF16)
        wb = wb_ref[...].astype(BF16)
        wo = wo_ref[...].astype(BF16)
        wa_s[s] = wa
        wb_s[s] = wb
        wo_s[pl.ds(pl.multiple_of(s * fc, fc), fc), :] = wo
        if emit_weights:
            wa_o, wb_o, wo_o = weight_outs
            wa_o[...] = wa
            wb_o[...] = wb
            wo_o[...] = wo

        @pl.when(s == 0)
        def _():
            _ffn_hidden(x_ref, mod_ref, g_ref, h_ref)
            o_ref[...] = jnp.zeros_like(o_ref)

        h = h_ref[...]
        u = (_silu(_dot(h, wa)) * _dot(h, wb)).astype(BF16)
        o_ref[...] += _dot(u, wo)

        @pl.when(s == FFN_CHUNKS - 1)
        def _():
            _ffn_finish(x_ref, mod_ref, fg_ref, o_ref, o_ref[...], final_norm)

    @pl.when(s >= FFN_CHUNKS)
    def _():
        _ffn_tile(x_ref, mod_ref, g_ref, fg_ref, o_ref, wa_s, wb_s, wo_s, h_ref, u_ref, final_norm)


def _ffn_cast(x, mod3, g, w_in, w_out, final_g, *, layer, half, group_of_tile, final_norm,
              tile0=0, n_tiles=None, emit_weights=True):
    tm, fc, nc = FFN_TM, FFN_FC, FFN_CHUNKS
    n = x.shape[0] if n_tiles is None else n_tiles * tm
    tile = lambda s: jnp.maximum(s - (nc - 1), 0)
    chunk = lambda s: jnp.minimum(s, nc - 1)
    kern = functools.partial(_ffn_cast_kernel, final_norm=final_norm, emit_weights=emit_weights)
    n_out = 4 if emit_weights else 1
    out = pl.pallas_call(
        kern,
        grid=(nc - 1 + n // tm,),
        in_specs=[
            pl.BlockSpec((tm, D), lambda s: (tile0 + tile(s), 0)),
            pl.BlockSpec((None, 3, D), lambda s: (group_of_tile(tile(s)), 0, 0)),
            pl.BlockSpec((1, D), lambda s: (0, 0)),
            pl.BlockSpec((None, None, D, fc), lambda s: (layer, half, 0, chunk(s))),
            pl.BlockSpec((None, None, D, fc), lambda s: (layer, half, 0, nc + chunk(s))),
            pl.BlockSpec((None, None, fc, D), lambda s: (layer, half, chunk(s), 0)),
            pl.BlockSpec((1, D), lambda s: (0, 0)),
        ],
        out_specs=[
            pl.BlockSpec((tm, D), lambda s: (tile(s), 0)),
            pl.BlockSpec((None, D, fc), lambda s: (chunk(s), 0, 0)),
            pl.BlockSpec((None, D, fc), lambda s: (chunk(s), 0, 0)),
            pl.BlockSpec((fc, D), lambda s: (chunk(s), 0)),
        ][:n_out],
        out_shape=[
            jax.ShapeDtypeStruct((n, D), F32),
            jax.ShapeDtypeStruct((nc, D, fc), BF16),
            jax.ShapeDtypeStruct((nc, D, fc), BF16),
            jax.ShapeDtypeStruct((D_FF, D), BF16),
        ][:n_out],
        scratch_shapes=[
            pltpu.VMEM((nc, D, fc), BF16),
            pltpu.VMEM((nc, D, fc), BF16),
            pltpu.VMEM((D_FF, D), BF16),
            pltpu.VMEM((tm, D), BF16),
            pltpu.VMEM((tm, D_FF), BF16),
        ],
        compiler_params=_params(("arbitrary",), 60),
        name="ffn_cast",
    )(x, mod3, g, w_in, w_in, w_out, final_g)
    return out if emit_weights else out[0]


def _ffn_bf16_kernel(x_ref, mod_ref, g_ref, wa_ref, wb_ref, wo_ref, fg_ref, o_ref, h_ref, u_ref, *,
                     final_norm):
    _ffn_tile(x_ref, mod_ref, g_ref, fg_ref, o_ref, wa_ref, wb_ref, wo_ref, h_ref, u_ref, final_norm)


def _ffn_bf16(x, mod3, g, wa, wb, wo, final_g, *, group_of_tile, final_norm, tile0=0, n_tiles=None):
    tm, fc, nc = FFN_TM, FFN_FC, FFN_CHUNKS
    n = x.shape[0] if n_tiles is None else n_tiles * tm
    kern = functools.partial(_ffn_bf16_kernel, final_norm=final_norm)
    return pl.pallas_call(
        kern,
        grid=(n // tm,),
        in_specs=[
            pl.BlockSpec((tm, D), lambda i: (tile0 + i, 0)),
            pl.BlockSpec((None, 3, D), lambda i: (group_of_tile(i), 0, 0)),
            pl.BlockSpec((1, D), lambda i: (0, 0)),
            pl.BlockSpec((nc, D, fc), lambda i: (0, 0, 0)),
            pl.BlockSpec((nc, D, fc), lambda i: (0, 0, 0)),
            pl.BlockSpec((D_FF, D), lambda i: (0, 0)),
            pl.BlockSpec((1, D), lambda i: (0, 0)),
        ],
        out_specs=pl.BlockSpec((tm, D), lambda i: (i, 0)),
        out_shape=jax.ShapeDtypeStruct((n, D), F32),
        scratch_shapes=[
            pltpu.VMEM((tm, D), BF16),
            pltpu.VMEM((tm, D_FF), BF16),
        ],
        compiler_params=_params(("arbitrary",), 60),
        name="ffn_bf16",
    )(x, mod3, g, wa, wb, wo, final_g)


def _proj_kernel(x_ref, mod_ref, g_ref, w_ref, *o_refs):
    h = _prenorm(x_ref[...], g_ref[...], mod_ref[0:1, :], mod_ref[1:2, :]).astype(BF16)
    for k, o_ref in enumerate(o_refs):
        o_ref[...] = _dot(h, w_ref[:, k * D:(k + 1) * D])


def _prenorm_proj(x, mod3, g, w_bf16, *, rows_per_group, row0=0, n_rows=None):
    n = x.shape[0] if n_rows is None else n_rows
    n_out = w_bf16.shape[1] // D
    tm = PROJ_TM
    tiles_per_group = rows_per_group // tm
    tile0 = row0 // tm
    return pl.pallas_call(
        _proj_kernel,
        grid=(n // tm,),
        in_specs=[
            pl.BlockSpec((tm, D), lambda i: (tile0 + i, 0)),
            pl.BlockSpec((None, 3, D), lambda i: (i // tiles_per_group, 0, 0)),
            pl.BlockSpec((1, D), lambda i: (0, 0)),
            pl.BlockSpec((D, n_out * D), lambda i: (0, 0)),
        ],
        out_specs=[pl.BlockSpec((tm, D), lambda i: (i, 0)) for _ in range(n_out)],
        out_shape=[jax.ShapeDtypeStruct((n, D), F32) for _ in range(n_out)],
        compiler_params=_params(("parallel",), 40),
        name="prenorm_proj",
    )(x, mod3, g, w_bf16)


def _outproj_kernel(yc_ref, yl_ref, w_ref, *rest, n_ctx_tiles):
    *x_refs, mod_ref, o_ref = rest
    is_ctx = pl.program_id(0) < n_ctx_tiles
    y = jnp.where(is_ctx, yc_ref[...], yl_ref[...])
    if len(x_refs) == 2:
        x = jnp.where(is_ctx, x_refs[0][...], x_refs[1][...])
    else:
        x = x_refs[0][...]
    o_ref[...] = x + mod_ref[2:3, :] * _dot(y, w_ref[...])


def _outproj_residual(y_ctx, y_lat, w_bf16, x, mod3, *, lat_len):
    tm = PROJ_TM
    n_ctx_tiles = y_ctx.shape[0] // tm
    n_lat_tiles = y_lat.shape[0] // tm
    lat_tiles_per_seq = lat_len // tm
    ctx_tile = lambda i: jnp.minimum(i, n_ctx_tiles - 1)
    lat_tile = lambda i: jnp.maximum(i - n_ctx_tiles, 0)
    group = lambda i: jnp.where(i < n_ctx_tiles, 0, 1 + lat_tile(i) // lat_tiles_per_seq)
    if isinstance(x, tuple):
        x_args = x
        x_specs = [pl.BlockSpec((tm, D), lambda i: (ctx_tile(i), 0)),
                   pl.BlockSpec((tm, D), lambda i: (lat_tile(i), 0))]
    else:
        x_args = (x,)
        x_specs = [pl.BlockSpec((tm, D), lambda i: (i, 0))]
    n_tiles = n_ctx_tiles + n_lat_tiles
    return pl.pallas_call(
        functools.partial(_outproj_kernel, n_ctx_tiles=n_ctx_tiles),
        grid=(n_tiles,),
        in_specs=[
            pl.BlockSpec((tm, D), lambda i: (ctx_tile(i), 0)),
            pl.BlockSpec((tm, D), lambda i: (lat_tile(i), 0)),
            pl.BlockSpec((D, D), lambda i: (0, 0)),
            *x_specs,
            pl.BlockSpec((None, 3, D), lambda i: (group(i), 0, 0)),
        ],
        out_specs=pl.BlockSpec((tm, D), lambda i: (i, 0)),
        out_shape=jax.ShapeDtypeStruct((n_tiles * tm, D), F32),
        compiler_params=_params(("parallel",), 32),
        name="outproj_residual",
    )(y_ctx, y_lat, w_bf16, *x_args, mod3)


def _rglru_kernel(gate_ref, xr_ref, cw_ref, cb_ref, wg_ref, bg_ref, lam_ref, h0_ref,
                  y_ref, fin_ref, pad_ref, xc_ref, a_ref, b_ref, yf_ref, *, seq_len, n_seq):
    tm = seq_len * n_seq
    pad_ref[0:8, :] = jnp.zeros((8, D), F32)
    pad_ref[8 + tm:16 + tm, :] = jnp.zeros((8, D), F32)
    pad_ref[8:8 + tm, :] = xr_ref[...]
    t_loc = lax.broadcasted_iota(jnp.int32, (tm, 1), 0) % seq_len
    xm2 = jnp.where(t_loc >= 2, pad_ref[6:6 + tm, :], 0.0)
    xm1 = jnp.where(t_loc >= 1, pad_ref[7:7 + tm, :], 0.0)
    xp1 = jnp.where(t_loc <= seq_len - 2, pad_ref[9:9 + tm, :], 0.0)
    xc = xm2 * cw_ref[0:1, :] + xm1 * cw_ref[1:2, :]
    xc = xc + xr_ref[...] * cw_ref[2:3, :]
    xc = xc + xp1 * cw_ref[3:4, :]
    xc_ref[...] = xc + cb_ref[...]

    if n_seq == 1:
        rows = lambda t: pl.ds(t, 1)
    else:
        rows = lambda t: pl.ds(t, n_seq, stride=seq_len)

    for d in range(2):
        half_decay = _lru_half_decay(lam_ref[d:d + 1, :])
        for q in range(N_GATE_GROUPS):
            cols = slice(q * GATE_GROUP, (q + 1) * GATE_GROUP)
            xq = xc_ref[:, cols]
            half_pre = _dot(xq.astype(BF16), wg_ref[d, q]) + bg_ref[d, q]
            a, b = _lru_coeffs(half_pre, 0.5 * xq, half_decay[:, cols])
            for c in range(GATE_GROUP // LANES):
                lg = q * (GATE_GROUP // LANES) + c
                a_ref[lg] = a[:, c * LANES:(c + 1) * LANES]
                b_ref[lg] = b[:, c * LANES:(c + 1) * LANES]

        h_dst = yf_ref if d == 0 else b_ref

        def step(k, hs, d=d, h_dst=h_dst):
            t = k if d == 0 else seq_len - 1 - k
            idx = rows(t)
            out = []
            for lg in range(D // LANES):
                h = a_ref[lg, idx, :] * hs[lg] + b_ref[lg, idx, :]
                h_dst[lg, idx, :] = h
                out.append(h)
            return tuple(out)

        h0 = h0_ref[d]
        h_init = tuple(h0[:, lg * LANES:(lg + 1) * LANES] for lg in range(D // LANES))
        h_fin = lax.fori_loop(0, seq_len, step, h_init, unroll=8)
        for lg in range(D // LANES):
            fin_ref[d, :, lg * LANES:(lg + 1) * LANES] = h_fin[lg]

    for lg in range(D // LANES):
        cols = slice(lg * LANES, (lg + 1) * LANES)
        y_ref[:, cols] = ((yf_ref[lg] + b_ref[lg]) * _gelu_tanh(gate_ref[:, cols])).astype(BF16)


def _rglru_core(gate, xr, conv_w, conv_b, wg, bg, lam, h0, *, seq_len, n_seq):
    n = gate.shape[0]
    tm = seq_len * n_seq
    n_tiles = n // tm
    kern = functools.partial(_rglru_kernel, seq_len=seq_len, n_seq=n_seq)
    return pl.pallas_call(
        kern,
        grid=(n_tiles,),
        in_specs=[
            pl.BlockSpec((tm, D), lambda i: (i, 0)),
            pl.BlockSpec((tm, D), lambda i: (i, 0)),
            pl.BlockSpec((4, D), lambda i: (0, 0)),
            pl.BlockSpec((1, D), lambda i: (0, 0)),
            pl.BlockSpec((2, N_GATE_GROUPS, GATE_GROUP, 2 * GATE_GROUP), lambda i: (0, 0, 0, 0)),
            pl.BlockSpec((2, N_GATE_GROUPS, 1, 2 * GATE_GROUP), lambda i: (0, 0, 0, 0)),
            pl.BlockSpec((2, D), lambda i: (0, 0)),
            pl.BlockSpec((2, None, n_seq, D), lambda i: (0, i, 0, 0)),
        ],
        out_specs=[
            pl.BlockSpec((tm, D), lambda i: (i, 0)),
            pl.BlockSpec((2, None, n_seq, D), lambda i: (0, i, 0, 0)),
        ],
        out_shape=[
            jax.ShapeDtypeStruct((n, D), BF16),
            jax.ShapeDtypeStruct((2, n_tiles, n_seq, D), F32),
        ],
        scratch_shapes=[
            pltpu.VMEM((tm + 16, D), F32),
            pltpu.VMEM((tm, D), F32),
            pltpu.VMEM((D // LANES, tm, LANES), F32),
            pltpu.VMEM((D // LANES, tm, LANES), F32),
            pltpu.VMEM((D // LANES, tm, LANES), F32),
        ],
        compiler_params=_params(("parallel",), 48),
        name="rglru_core",
    )(gate, xr, conv_w, conv_b, wg, bg, lam, h0)


def _lru_coeffs(half_pre, half_x, half_decay):
    tr = jnp.tanh(half_pre[:, :GATE_GROUP])
    ti = jnp.tanh(half_pre[:, GATE_GROUP:])
    log_a = half_decay * tr + half_decay
    a = jnp.exp(log_a)
    v = -jnp.tanh(log_a) * (1.0 + a * a)
    m = jnp.where(v > 0.0, v * lax.rsqrt(v), 0.0)
    return a, (m * half_x) * (1.0 + ti)


def _lru_half_decay(lam_row):
    z = -lam_row
    return (-0.5 * LRU_C) * (jnp.maximum(z, 0.0) + jnp.log1p(jnp.exp(-jnp.abs(z))))


def _rglru_tm_kernel(gate_ref, xr_ref, cw_ref, cb_ref, wg_ref, bg_ref, lam_ref, y_ref, fin_ref,
                     pad_ref, xc_ref, a0_ref, b0_ref, a1_ref, b1_ref, *, seq_len):
    S = TM_GROUP
    R = seq_len * S
    lead = 2 * S
    n_lg = GATE_GROUP // LANES
    for lg in range(n_lg):
        cols = slice(lg * LANES, (lg + 1) * LANES)
        pad_ref[lg, 0:lead, :] = jnp.zeros((lead, LANES), F32)
        pad_ref[lg, lead + R:lead + R + S, :] = jnp.zeros((S, LANES), F32)
        for s in range(S):
            pad_ref[lg, pl.ds(lead + s, seq_len, stride=S), :] = xr_ref[s * seq_len:(s + 1) * seq_len, cols]
        xc = pad_ref[lg, 0:R, :] * cw_ref[0:1, cols] + pad_ref[lg, S:S + R, :] * cw_ref[1:2, cols]
        xc = xc + pad_ref[lg, lead:lead + R, :] * cw_ref[2:3, cols]
        xc = xc + pad_ref[lg, lead + S:lead + S + R, :] * cw_ref[3:4, cols]
        xc_ref[:, cols] = xc + cb_ref[:, cols]
    xcb = xc_ref[...].astype(BF16)
    for d, (a_ref, b_ref) in enumerate(((a0_ref, b0_ref), (a1_ref, b1_ref))):
        half_pre = _dot(xcb, wg_ref[d]) + bg_ref[d]
        a, b = _lru_coeffs(half_pre, 0.5 * xc_ref[...], _lru_half_decay(lam_ref[d:d + 1, :]))
        a_ref[...] = a
        b_ref[...] = b

    def step(k, carry):
        hf, hb = carry
        rf = pl.ds(pl.multiple_of(k * S, S), S)
        rb = pl.ds(pl.multiple_of((seq_len - 1 - k) * S, S), S)
        hf = a0_ref[rf, :] * hf + b0_ref[rf, :]
        hb = a1_ref[rb, :] * hb + b1_ref[rb, :]
        b0_ref[rf, :] = hf
        b1_ref[rb, :] = hb
        return hf, hb

    zero = jnp.zeros((S, GATE_GROUP), F32)
    hf, hb = lax.fori_loop(0, seq_len, step, (zero, zero), unroll=8)
    fin_ref[0] = hf
    fin_ref[1] = hb
    for lg in range(n_lg):
        cols = slice(lg * LANES, (lg + 1) * LANES)
        pad_ref[lg, 0:R, :] = b0_ref[:, cols] + b1_ref[:, cols]
        for s in range(S):
            rows = slice(s * seq_len, (s + 1) * seq_len)
            ysum = pad_ref[lg, pl.ds(s, seq_len, stride=S), :]
            y_ref[rows, cols] = (ysum * _gelu_tanh(gate_ref[rows, cols])).astype(BF16)


def _rglru_core_tm(gate, xr, conv_w, conv_b, wg, bg, lam, *, seq_len, row_block0=0, n_rows=None):
    n = gate.shape[0] if n_rows is None else n_rows
    rows = seq_len * TM_GROUP
    n_groups = n // rows
    kern = functools.partial(_rglru_tm_kernel, seq_len=seq_len)
    slab_in = pl.BlockSpec((rows, GATE_GROUP), lambda gi, q: (row_block0 + gi, q))
    return pl.pallas_call(
        kern,
        grid=(n_groups, N_GATE_GROUPS),
        in_specs=[
            slab_in,
            slab_in,
            pl.BlockSpec((4, GATE_GROUP), lambda gi, q: (0, q)),
            pl.BlockSpec((1, GATE_GROUP), lambda gi, q: (0, q)),
            pl.BlockSpec((2, None, GATE_GROUP, 2 * GATE_GROUP), lambda gi, q: (0, q, 0, 0)),
            pl.BlockSpec((2, None, 1, 2 * GATE_GROUP), lambda gi, q: (0, q, 0, 0)),
            pl.BlockSpec((2, GATE_GROUP), lambda gi, q: (0, q)),
        ],
        out_specs=[
            pl.BlockSpec((rows, GATE_GROUP), lambda gi, q: (gi, q)),
            pl.BlockSpec((2, None, TM_GROUP, GATE_GROUP), lambda gi, q: (0, gi, 0, q)),
        ],
        out_shape=[
            jax.ShapeDtypeStruct((n, D), BF16),
            jax.ShapeDtypeStruct((2, n_groups, TM_GROUP, D), F32),
        ],
        scratch_shapes=[pltpu.VMEM((GATE_GROUP // LANES, rows + 3 * TM_GROUP, LANES), F32)]
        + [pltpu.VMEM((rows, GATE_GROUP), F32) for _ in range(5)],
        compiler_params=_params(("parallel", "parallel"), 40),
        name="rglru_core_tm",
    )(gate, xr, conv_w, conv_b, wg, bg, lam)


def _gate_weights(gw_a, gb_a, gw_x, gb_x):
    per = GATE_GROUP // LRU_BLOCK

    def dense(w):
        w = w.reshape(2, N_GATE_GROUPS, per, LRU_BLOCK, LRU_BLOCK)
        eye = jnp.eye(per, dtype=w.dtype)
        full = jnp.einsum('dqpkj,pr->dqpkrj', w, eye)
        return full.reshape(2, N_GATE_GROUPS, GATE_GROUP, GATE_GROUP)

    wg = (0.5 * jnp.concatenate([dense(gw_a), dense(gw_x)], axis=-1)).astype(BF16)
    ba = gb_a.reshape(2, N_GATE_GROUPS, 1, GATE_GROUP)
    bx = gb_x.reshape(2, N_GATE_GROUPS, 1, GATE_GROUP)
    return wg, 0.5 * jnp.concatenate([ba, bx], axis=-1)


def _lam_value(lq_ref, lk_ref, lam_init):
    s = jnp.sum(lq_ref[...] * lk_ref[...], axis=-1, keepdims=True)
    e = jnp.exp(s)
    return e[0:1, :] - e[1:2, :] + lam_init


def _softmax_av(s, v_ones):
    e = jnp.exp(s - jnp.max(s, axis=-1, keepdims=True)).astype(BF16)
    ov = _dot(e, v_ones)
    return ov[:, :HEAD_W] * (1.0 / ov[:, HEAD_W:])


def _diff_head(qh, kh_bf16, vh_bf16, lam, g_row, out_scale):
    lane = lax.broadcasted_iota(jnp.int32, qh.shape, 1)
    qs = qh * (MAP_W ** -0.5)
    q0 = jnp.where(lane < MAP_W, qs, 0.0).astype(BF16)
    q1 = jnp.where(lane >= MAP_W, qs, 0.0).astype(BF16)
    v_ones = jnp.concatenate([vh_bf16, jnp.ones_like(vh_bf16)], axis=1)
    o = _softmax_av(_dot_nt(q0, kh_bf16), v_ones) - lam * _softmax_av(_dot_nt(q1, kh_bf16), v_ones)
    return _rms(o, g_row) * out_scale


def _attn_ctx_kernel(q_ref, k_ref, v_ref, lq_ref, lk_ref, g_ref, o_ref, *, lam_init):
    lam = _lam_value(lq_ref, lk_ref, lam_init)
    for hd in range(N_HEADS):
        cols = slice(hd * HEAD_W, (hd + 1) * HEAD_W)
        o = _diff_head(q_ref[:, cols], k_ref[:, cols].astype(BF16), v_ref[:, cols].astype(BF16),
                       lam, g_ref[...], 1.0 - lam_init)
        o_ref[:, cols] = o.astype(BF16)


def _attn_ctx(q, k, v, lq, lk, g, *, seq_len, lam_init):
    n = q.shape[0]
    kern = functools.partial(_attn_ctx_kernel, lam_init=lam_init)
    tok = pl.BlockSpec((seq_len, D), lambda i: (i, 0))
    small = lambda shape: pl.BlockSpec(shape, lambda i: (0, 0))
    return pl.pallas_call(
        kern,
        grid=(n // seq_len,),
        in_specs=[tok, tok, tok, small((2, MAP_W)), small((2, MAP_W)), small((1, HEAD_W))],
        out_specs=tok,
        out_shape=jax.ShapeDtypeStruct((n, D), BF16),
        compiler_params=_params(("parallel",), 32),
        name="attn_context",
    )(q, k, v, lq, lk, g)


def _rope(x, cos, sin_signed):
    lane = lax.broadcasted_iota(jnp.int32, x.shape, 1)
    partner = jnp.where(lane % 32 < 16, pltpu.roll(x, HEAD_W - 16, 1), pltpu.roll(x, 16, 1))
    return x * cos + partner * sin_signed


def _attn_lat_kernel(q_ref, k_ref, v_ref, ck_ref, cv_ref, cos_ref, sin_ref, lq_ref, lk_ref, g_ref,
                     o_ref, kall_ref, vall_ref, *, lam_init, past_len, seq_len, tq):
    qb = pl.program_id(1)

    @pl.when(qb == 0)
    def _():
        kall_ref[0:past_len, :] = ck_ref[...].astype(BF16)
        vall_ref[0:past_len, :] = cv_ref[...].astype(BF16)
        vall_ref[past_len:past_len + seq_len, :] = v_ref[...].astype(BF16)
        for hd in range(N_HEADS):
            cols = slice(hd * HEAD_W, (hd + 1) * HEAD_W)
            kr = _rope(k_ref[:, cols], cos_ref[...], sin_ref[...])
            kall_ref[past_len:past_len + seq_len, cols] = kr.astype(BF16)

    lam = _lam_value(lq_ref, lk_ref, lam_init)
    row0 = pl.multiple_of(qb * tq, tq)
    cos_q = cos_ref[pl.ds(row0, tq), :]
    sin_q = sin_ref[pl.ds(row0, tq), :]
    for hd in range(N_HEADS):
        cols = slice(hd * HEAD_W, (hd + 1) * HEAD_W)
        qh = _rope(q_ref[:, cols], cos_q, sin_q)
        o = _diff_head(qh, kall_ref[:, cols], vall_ref[:, cols], lam, g_ref[...], 1.0 - lam_init)
        o_ref[:, cols] = o.astype(BF16)


def _attn_lat(q, k, v, cache_k, cache_v, cos, sin_signed, lq, lk, g, *, seq_len, lam_init):
    n = q.shape[0]
    n_batch = n // seq_len
    past_len = cache_k.shape[1]
    tq = 256
    n_qb = seq_len // tq
    kern = functools.partial(_attn_lat_kernel, lam_init=lam_init, past_len=past_len,
                             seq_len=seq_len, tq=tq)
    seq = pl.BlockSpec((seq_len, D), lambda b, j: (b, 0))
    cache = pl.BlockSpec((None, past_len, D), lambda b, j: (b, 0, 0))
    small = lambda shape: pl.BlockSpec(shape, lambda b, j: (0, 0))
    qblk = pl.BlockSpec((tq, D), lambda b, j: (b * n_qb + j, 0))
    return pl.pallas_call(
        kern,
        grid=(n_batch, n_qb),
        in_specs=[qblk, seq, seq, cache, cache, small((seq_len, HEAD_W)), small((seq_len, HEAD_W)),
                  small((2, MAP_W)), small((2, MAP_W)), small((1, HEAD_W))],
        out_specs=qblk,
        out_shape=jax.ShapeDtypeStruct((n, D), BF16),
        scratch_shapes=[pltpu.VMEM((past_len + seq_len, D), BF16),
                        pltpu.VMEM((past_len + seq_len, D), BF16)],
        compiler_params=_params(("parallel", "arbitrary"), 48),
        name="attn_latent",
    )(q, k, v, cache_k, cache_v, cos, sin_signed, lq, lk, g)


def _rope_tables(seq_len):
    t = np.arange(seq_len)
    pos = np.stack([t // GRID_W, t % GRID_W], axis=-1).astype(np.float32)
    nf = MAP_W // 4
    inv = (np.float32(ROPE_THETA) ** (-np.arange(nf, dtype=np.float32) / nf)).astype(np.float32)
    lane = np.arange(HEAD_W)
    axis = (lane % MAP_W) // 32
    freq = lane % nf
    sign = np.where(lane % 32 < nf, -1.0, 1.0).astype(np.float32)
    pos_l = jnp.asarray(pos)[:, axis]
    ang = pos_l * jnp.asarray(inv)[freq][None, :]
    return jnp.cos(ang), jnp.sin(ang) * jnp.asarray(sign)[None, :]


def _pool_sequence(x_ref, mod_ref, g_ref, w_ref, sc_ref, o_ref, h_ref, lvl_ref, T):
    R = T + 2 * POOL_PAD
    h = _prenorm(x_ref[...], g_ref[...], mod_ref[0:1, :], mod_ref[1:2, :])
    h_ref[...] = h
    lvl_ref[0:POOL_PAD, :] = jnp.zeros((POOL_PAD, D), F32)
    lvl_ref[POOL_PAD + T:R, :] = jnp.zeros((POOL_PAD, D), F32)
    lvl_ref[POOL_PAD:POOL_PAD + T, :] = h
    t = lax.broadcasted_iota(jnp.int32, (T, 1), 0)
    for gi, win in enumerate(POOL_WINDOWS):
        cols = slice(gi * POOL_GROUP, (gi + 1) * POOL_GROUP)
        half = win // 2
        s = 1
        while s < win:
            lvl_ref[POOL_PAD:R, cols] = lvl_ref[POOL_PAD - s:R - s, cols] + lvl_ref[POOL_PAD:R, cols]
            s *= 2
        wsum = lvl_ref[POOL_PAD + half - 1:POOL_PAD + half - 1 + T, cols]
        cnt = (jnp.minimum(t + half, T) - jnp.maximum(t - half, 0)).astype(F32)
        diff = wsum * (1.0 / cnt) - h_ref[:, cols]
        m = _dot(diff.astype(BF16), w_ref[gi]) * sc_ref[:, cols]
        o_ref[:, cols] = x_ref[:, cols] + mod_ref[2:3, cols] * m


def _pool_kernel(x_ref, mod_ref, g_ref, w_ref, sc_ref, o_ref, h_ref, lvl_ref, *, n_ctx_tiles, ctx_len):
    tile_rows = x_ref.shape[0]
    is_ctx = pl.program_id(0) < n_ctx_tiles

    @pl.when(is_ctx)
    def _():
        for k in range(tile_rows // ctx_len):
            rows = slice(k * ctx_len, (k + 1) * ctx_len)
            _pool_sequence(x_ref.at[rows], mod_ref, g_ref, w_ref, sc_ref, o_ref.at[rows], h_ref.at[rows],
                           lvl_ref.at[0:ctx_len + 2 * POOL_PAD], ctx_len)

    @pl.when(jnp.logical_not(is_ctx))
    def _():
        _pool_sequence(x_ref, mod_ref, g_ref, w_ref, sc_ref, o_ref, h_ref, lvl_ref, tile_rows)


def _pool_mixer(x, mod3, g, w_bf16, scale, *, n_ctx, ctx_len, lat_len):
    n = x.shape[0]
    tm = lat_len
    n_ctx_tiles = n_ctx // tm
    kern = functools.partial(_pool_kernel, n_ctx_tiles=n_ctx_tiles, ctx_len=ctx_len)
    tok = pl.BlockSpec((tm, D), lambda i: (i, 0))
    return pl.pallas_call(
        kern,
        grid=(n // tm,),
        in_specs=[
            tok,
            pl.BlockSpec((None, 3, D), lambda i: (jnp.maximum(i - (n_ctx_tiles - 1), 0), 0, 0)),
            pl.BlockSpec((1, D), lambda i: (0, 0)),
            pl.BlockSpec((len(POOL_WINDOWS), POOL_GROUP, POOL_GROUP), lambda i: (0, 0, 0)),
            pl.BlockSpec((1, D), lambda i: (0, 0)),
        ],
        out_specs=tok,
        out_shape=jax.ShapeDtypeStruct((n, D), F32),
        scratch_shapes=[pltpu.VMEM((tm, D), F32),
                        pltpu.VMEM((tm + 2 * POOL_PAD, D), F32)],
        compiler_params=_params(("parallel",), 40),
        name="pool_mixer",
    )(x, mod3, g, w_bf16, scale)


def kernel(x_prompt, x_sample, state_rglru, cache_k_diff, cache_v_diff, c, c_ctx, norm_g, w_mod, b_mod, w_ffn_in, w_ffn_out, a_w_in, a_conv_w, a_conv_b, a_gate_w_a, a_gate_b_a, a_gate_w_x, a_gate_b_x, a_lambda, a_w_out, b_w_qkv, b_lam_q, b_lam_k, b_subln_g, b_w_o, c_w_pool, c_scale, final_norm_g):
    n_ctx_seq, ctx_len, _ = x_prompt.shape
    n_lat_seq, lat_len, _ = x_sample.shape
    n_ctx, n_lat = n_ctx_seq * ctx_len, n_lat_seq * lat_len
    xp = x_prompt.reshape(n_ctx, D)
    xs = x_sample.reshape(n_lat, D)

    cond8 = jnp.concatenate([c_ctx[None], c, jnp.zeros((8 - 1 - n_lat_seq, D), F32)], axis=0)
    mod = _adaln_all(cond8, w_mod, b_mod).reshape(DEPTH, 8, N_MOD, D)
    final_g = final_norm_g.reshape(1, D)

    assert n_ctx_seq % TM_GROUP == 0 and lat_len == FFN_TM and n_ctx % FFN_TM == 0
    n_ctx_tiles = n_ctx // FFN_TM
    n_lat_tiles = n_lat // FFN_TM
    ctx_group = lambda t: 0
    lat_group = lambda t: t
    all_group = lambda t: jnp.maximum(t - (n_ctx_tiles - 1), 0)
    new_states, new_k, new_v = [], [], []
    x = (xp, xs)
    for l in range(DEPTH):
        mod_all = mod[l, 0:1 + n_lat_seq]
        mod_p, mod_s = mod_all[0:1], mod_all[1:]
        g = norm_g[l]
        kind, j = l % 3, l // 3

        def ffn_split(x, k, half, final):
            sl = slice(3 * k, 3 * k + 3)
            x_ctx, x_lat, lat_tile0 = (x[0], x[1], 0) if isinstance(x, tuple) else (x, x, n_ctx_tiles)
            yp, wa, wb, wo = _ffn_cast(x_ctx, mod_p[:, sl], g[k:k + 1], w_ffn_in, w_ffn_out, final_g,
                                       layer=l, half=half, group_of_tile=ctx_group, final_norm=final,
                                       n_tiles=n_ctx_tiles)
            ys = _ffn_bf16(x_lat, mod_s[:, sl], g[k:k + 1], wa, wb, wo, final_g,
                           group_of_tile=lat_group, final_norm=final, tile0=lat_tile0, n_tiles=n_lat_tiles)
            return yp, ys

        def ffn_all(x, k, half):
            sl = slice(3 * k, 3 * k + 3)
            return _ffn_cast(x, mod_all[:, sl], g[k:k + 1], w_ffn_in, w_ffn_out, final_g,
                             layer=l, half=half, group_of_tile=all_group, final_norm=False,
                             emit_weights=False)

        x = ffn_split(x, 0, 0, False) if isinstance(x, tuple) else ffn_all(x, 0, 0)

        g1 = g[1:2]
        mp3, ms3, mall3 = mod_p[:, 3:6], mod_s[:, 3:6], mod_all[:, 3:6]
        if isinstance(x, tuple):
            ctx_rows = dict(x=x[0])
            lat_rows = dict(x=x[1])
        else:
            ctx_rows = dict(x=x, row0=0, n_rows=n_ctx)
            lat_rows = dict(x=x, row0=n_ctx, n_rows=n_lat)
        if kind == 0:
            w_in = a_w_in[j].astype(BF16)
            w_out = a_w_out[j].astype(BF16)
            wg, bg = _gate_weights(a_gate_w_a[j], a_gate_b_a[j], a_gate_w_x[j], a_gate_b_x[j])
            conv_b = a_conv_b[j].reshape(1, D)
            gate_p, xr_p = _prenorm_proj(mod3=mp3, g=g1, w_bf16=w_in, rows_per_group=n_ctx, **ctx_rows)
            y_p, fin = _rglru_core_tm(gate_p, xr_p, a_conv_w[j], conv_b, wg, bg, a_lambda[j],
                                      seq_len=ctx_len)
            new_states.append(jnp.transpose(fin.reshape(2, n_ctx_seq, D), (1, 0, 2)))
            gate_s, xr_s = _prenorm_proj(mod3=ms3, g=g1, w_bf16=w_in, rows_per_group=lat_len, **lat_rows)
            h0 = jnp.transpose(state_rglru[:, j].astype(F32), (1, 0, 2)).reshape(2, n_lat_seq, 1, D)
            y_s, _ = _rglru_core(gate_s, xr_s, a_conv_w[j], conv_b, wg, bg, a_lambda[j], h0,
                                 seq_len=lat_len, n_seq=1)
            x = _outproj_residual(y_p, y_s, w_out, x, mall3, lat_len=lat_len)
        elif kind == 1:
            lam_init = 0.8 - 0.6 * math.exp(-0.3 * l)
            w_qkv = b_w_qkv[j].astype(BF16)
            w_o = b_w_o[j].astype(BF16)
            sub_g = b_subln_g[j].reshape(1, HEAD_W)
            q_p, k_p, v_p = _prenorm_proj(mod3=mp3, g=g1, w_bf16=w_qkv, rows_per_group=n_ctx, **ctx_rows)
            o_p = _attn_ctx(q_p, k_p, v_p, b_lam_q[j], b_lam_k[j], sub_g, seq_len=ctx_len,
                            lam_init=lam_init)
            new_k.append(k_p.reshape(n_ctx_seq, ctx_len, N_HEADS, HEAD_W))
            new_v.append(v_p.reshape(n_ctx_seq, ctx_len, N_HEADS, HEAD_W))
            q_s, k_s, v_s = _prenorm_proj(mod3=ms3, g=g1, w_bf16=w_qkv, rows_per_group=lat_len, **lat_rows)
            cos, sin_signed = _rope_tables(lat_len)
            past_len = cache_k_diff.shape[2]
            o_s = _attn_lat(q_s, k_s, v_s,
                            cache_k_diff[:, j].reshape(n_lat_seq, past_len, D),
                            cache_v_diff[:, j].reshape(n_lat_seq, past_len, D),
                            cos, sin_signed, b_lam_q[j], b_lam_k[j], sub_g,
                            seq_len=lat_len, lam_init=lam_init)
            x = _outproj_residual(o_p, o_s, w_o, x, mall3, lat_len=lat_len)
        else:
            assert not isinstance(x, tuple)
            x = _pool_mixer(x, mall3, g1, c_w_pool[j].astype(BF16), c_scale[j].reshape(1, D),
                            n_ctx=n_ctx, ctx_len=ctx_len, lat_len=lat_len)

        x = ffn_split(x, 2, 1, True) if l == DEPTH - 1 else ffn_all(x, 2, 1)

    xp, xs = x
    y_prompt = xp.reshape(x_prompt.shape)
    y_sample = xs.reshape(x_sample.shape)
    new_state = jnp.stack(new_states, axis=1).astype(x_prompt.dtype)
    new_cache_k = jnp.stack(new_k, axis=1).astype(x_prompt.dtype)
    new_cache_v = jnp.stack(new_v, axis=1).astype(x_prompt.dtype)
    return (y_prompt, y_sample, new_state, new_cache_k, new_cache_v)
```

```python
import functools
import math

import jax
import jax.numpy as jnp
import numpy as np
from jax import lax
from jax.experimental import pallas as pl
from jax.experimental.pallas import tpu as pltpu

F32 = jnp.float32
BF16 = jnp.bfloat16

D = 1024
DEPTH = 4
N_MOD = 9
EPS = 1e-6
D_FF = 2816
LRU_BLOCK = 64
N_LRU_BLOCKS = 16
LRU_C = 8.0
N_HEADS = 8
LANES = 128
HEAD_W = 128
MAP_W = 64
assert MAP_W ** -0.5 == 0.125
GRID_W = 64
ROPE_THETA = 10000.0
POOL_WINDOWS = (2, 4, 8, 16)
POOL_GROUP = 256
POOL_PAD = 16

GATE_GROUP = 256
N_GATE_GROUPS = D // GATE_GROUP

MIB = 1024 * 1024


def _params(sem, vmem_mib):
    return pltpu.CompilerParams(dimension_semantics=sem, vmem_limit_bytes=vmem_mib * MIB)


def _dot(a, b):
    return jnp.dot(a, b, preferred_element_type=F32)


def _dot_nt(a, b):
    return lax.dot_general(a, b, (((1,), (1,)), ((), ())), preferred_element_type=F32)


def _sigmoid(x):
    return 0.5 + 0.5 * jnp.tanh(0.5 * x)


def _silu(x):
    return x * _sigmoid(x)


def _gelu_tanh(x):
    c = math.sqrt(2.0 / math.pi)
    hx = 0.5 * x
    return hx + hx * jnp.tanh(x * (c + (0.044715 * c) * (x * x)))


def _rms(x, g):
    ms = jnp.mean(x * x, axis=-1, keepdims=True)
    return (x * lax.rsqrt(ms + EPS)) * g


def _prenorm(x, g, shift, scale):
    return _rms(x, g) * (1.0 + scale) + shift


def _mod_kernel(cond_ref, w_ref, b_ref, o_ref):
    s = _silu(cond_ref[...]).astype(BF16)
    o_ref[...] = _dot(s, w_ref[...].astype(BF16)) + b_ref[...]


def _adaln_all(cond8, w_mod, b_mod):
    tn = 1536
    n = N_MOD * D
    return pl.pallas_call(
        _mod_kernel,
        grid=(DEPTH, n // tn),
        in_specs=[
            pl.BlockSpec((8, D), lambda l, j: (0, 0)),
            pl.BlockSpec((None, D, tn), lambda l, j: (l, 0, j)),
            pl.BlockSpec((None, 1, tn), lambda l, j: (l, 0, j)),
        ],
        out_specs=pl.BlockSpec((None, 8, tn), lambda l, j: (l, 0, j)),
        out_shape=jax.ShapeDtypeStruct((DEPTH, 8, n), F32),
        compiler_params=_params(("parallel", "parallel"), 32),
        name="adaln_mod",
    )(cond8, w_mod, b_mod.reshape(DEPTH, 1, n))


FFN_TM = 1024
FFN_FC = 256
FFN_CHUNKS = D_FF // FFN_FC


TM_GROUP = 8
PROJ_TM = 512


def _ffn_kernel(*refs, final_norm, n_x, n_o, n_ctx_tiles):
    x_refs = refs[:n_x]
    mod_ref, g_ref, wa_ref, wb_ref, wo_ref, fg_ref = refs[n_x:n_x + 6]
    o_refs = refs[n_x + 6:n_x + 6 + n_o]
    wa_s, wb_s, wo_s, h_ref, u_ref = refs[n_x + 6 + n_o:]
    s = pl.program_id(0)
    fc = FFN_FC
    is_ctx = s - (FFN_CHUNKS - 1) < n_ctx_tiles

    def x_tile():
        if n_x == 1:
            return x_refs[0][...]
        return jnp.where(is_ctx, x_refs[0][...], x_refs[1][...])

    def hidden():
        h = _prenorm(x_tile(), g_ref[...], mod_ref[0:1, :], mod_ref[1:2, :])
        h_ref[...] = h.astype(BF16)

    def result(acc):
        y = x_tile() + (0.5 * mod_ref[2:3, :]) * acc
        return _rms(y, fg_ref[...]) if final_norm else y

    @pl.when(s < FFN_CHUNKS)
    def _():
        o_ref = o_refs[0]
        wa = wa_ref[...].astype(BF16)
        wb = wb_ref[...].astype(BF16)
        wo = wo_ref[...].astype(BF16)
        wa_s[s] = wa
        wb_s[s] = wb
        wo_s[pl.ds(pl.multiple_of(s * fc, fc), fc), :] = wo

        @pl.when(s == 0)
        def _():
            hidden()
            o_ref[...] = jnp.zeros_like(o_ref)

        h = h_ref[...]
        u = (_silu(_dot(h, wa)) * _dot(h, wb)).astype(BF16)
        o_ref[...] += _dot(u, wo)

        @pl.when(s == FFN_CHUNKS - 1)
        def _():
            o_ref[...] = result(o_ref[...])

    @pl.when(s >= FFN_CHUNKS)
    def _():
        hidden()
        for j in range(FFN_CHUNKS):
            h = h_ref[...]
            u = _silu(_dot(h, wa_s[j])) * _dot(h, wb_s[j])
            u_ref[:, j * fc:(j + 1) * fc] = u.astype(BF16)
        y = result(_dot(u_ref[...], wo_s[...]))
        if n_o == 1:
            o_refs[0][...] = y
        else:
            @pl.when(is_ctx)
            def _():
                o_refs[0][...] = y

            @pl.when(jnp.logical_not(is_ctx))
            def _():
                o_refs[1][...] = y


def _ffn(x, mod3, g, w_in, w_out, final_g, *, layer, half, n_ctx_tiles, final_norm, split_out):
    tm, fc, nc = FFN_TM, FFN_FC, FFN_CHUNKS
    xs = x if isinstance(x, tuple) else (x,)
    n = sum(a.shape[0] for a in xs)
    n_ctx = n_ctx_tiles * tm
    tile = lambda s: jnp.maximum(s - (nc - 1), 0)
    chunk = lambda s: jnp.minimum(s, nc - 1)
    one_spec = [pl.BlockSpec((tm, D), lambda s: (tile(s), 0))]
    two_specs = [pl.BlockSpec((tm, D), lambda s: (jnp.minimum(tile(s), n_ctx_tiles - 1), 0)),
                 pl.BlockSpec((tm, D), lambda s: (jnp.maximum(tile(s) - n_ctx_tiles, 0), 0))]
    two_shapes = [jax.ShapeDtypeStruct((n_ctx, D), F32), jax.ShapeDtypeStruct((n - n_ctx, D), F32)]
    n_o = 2 if split_out else 1
    kern = functools.partial(_ffn_kernel, final_norm=final_norm, n_x=len(xs), n_o=n_o,
                             n_ctx_tiles=n_ctx_tiles)
    out = pl.pallas_call(
        kern,
        grid=(nc - 1 + n // tm,),
        in_specs=[
            *(two_specs if len(xs) == 2 else one_spec),
            pl.BlockSpec((None, 3, D), lambda s: (jnp.maximum(tile(s) - (n_ctx_tiles - 1), 0), 0, 0)),
            pl.BlockSpec((1, D), lambda s: (0, 0)),
            pl.BlockSpec((None, None, D, fc), lambda s: (layer, half, 0, chunk(s))),
            pl.BlockSpec((None, None, D, fc), lambda s: (layer, half, 0, nc + chunk(s))),
            pl.BlockSpec((None, None, fc, D), lambda s: (layer, half, chunk(s), 0)),
            pl.BlockSpec((1, D), lambda s: (0, 0)),
        ],
        out_specs=two_specs if split_out else one_spec,
        out_shape=two_shapes if split_out else [jax.ShapeDtypeStruct((n, D), F32)],
        scratch_shapes=[
            pltpu.VMEM((nc, D, fc), BF16),
            pltpu.VMEM((nc, D, fc), BF16),
            pltpu.VMEM((D_FF, D), BF16),
            pltpu.VMEM((tm, D), BF16),
            pltpu.VMEM((tm, D_FF), BF16),
        ],
        compiler_params=_params(("arbitrary",), 60),
        name="ffn",
    )(*xs, mod3, g, w_in, w_in, w_out, final_g)
    return tuple(out) if split_out else out[0]


def _proj_kernel(x_ref, mod_ref, g_ref, w_ref, *o_refs):
    h = _prenorm(x_ref[...], g_ref[...], mod_ref[0:1, :], mod_ref[1:2, :]).astype(BF16)
    for k, o_ref in enumerate(o_refs):
        o_ref[...] = _dot(h, w_ref[:, k * D:(k + 1) * D])


def _prenorm_proj(x, mod3, g, w_bf16, *, rows_per_group, row0=0, n_rows=None):
    n = x.shape[0] if n_rows is None else n_rows
    n_out = w_bf16.shape[1] // D
    tm = PROJ_TM
    tiles_per_group = rows_per_group // tm
    tile0 = row0 // tm
    return pl.pallas_call(
        _proj_kernel,
        grid=(n // tm,),
        in_specs=[
            pl.BlockSpec((tm, D), lambda i: (tile0 + i, 0)),
            pl.BlockSpec((None, 3, D), lambda i: (i // tiles_per_group, 0, 0)),
            pl.BlockSpec((1, D), lambda i: (0, 0)),
            pl.BlockSpec((D, n_out * D), lambda i: (0, 0)),
        ],
        out_specs=[pl.BlockSpec((tm, D), lambda i: (i, 0)) for _ in range(n_out)],
        out_shape=[jax.ShapeDtypeStruct((n, D), F32) for _ in range(n_out)],
        compiler_params=_params(("parallel",), 40),
        name="prenorm_proj",
    )(x, mod3, g, w_bf16)


def _outproj_kernel(yc_ref, yl_ref, w_ref, *rest, n_ctx_tiles):
    *x_refs, mod_ref, o_ref = rest
    is_ctx = pl.program_id(0) < n_ctx_tiles
    y = jnp.where(is_ctx, yc_ref[...], yl_ref[...])
    if len(x_refs) == 2:
        x = jnp.where(is_ctx, x_refs[0][...], x_refs[1][...])
    else:
        x = x_refs[0][...]
    o_ref[...] = x + mod_ref[2:3, :] * _dot(y, w_ref[...])


def _outproj_residual(y_ctx, y_lat, w_bf16, x, mod3, *, lat_len):
    tm = PROJ_TM
    n_ctx_tiles = y_ctx.shape[0] // tm
    n_lat_tiles = y_lat.shape[0] // tm
    lat_tiles_per_seq = lat_len // tm
    ctx_tile = lambda i: jnp.minimum(i, n_ctx_tiles - 1)
    lat_tile = lambda i: jnp.maximum(i - n_ctx_tiles, 0)
    group = lambda i: jnp.where(i < n_ctx_tiles, 0, 1 + lat_tile(i) // lat_tiles_per_seq)
    if isinstance(x, tuple):
        x_args = x
        x_specs = [pl.BlockSpec((tm, D), lambda i: (ctx_tile(i), 0)),
                   pl.BlockSpec((tm, D), lambda i: (lat_tile(i), 0))]
    else:
        x_args = (x,)
        x_specs = [pl.BlockSpec((tm, D), lambda i: (i, 0))]
    n_tiles = n_ctx_tiles + n_lat_tiles
    return pl.pallas_call(
        functools.partial(_outproj_kernel, n_ctx_tiles=n_ctx_tiles),
        grid=(n_tiles,),
        in_specs=[
            pl.BlockSpec((tm, D), lambda i: (ctx_tile(i), 0)),
            pl.BlockSpec((tm, D), lambda i: (lat_tile(i), 0)),
            pl.BlockSpec((D, D), lambda i: (0, 0)),
            *x_specs,
            pl.BlockSpec((None, 3, D), lambda i: (group(i), 0, 0)),
        ],
        out_specs=pl.BlockSpec((tm, D), lambda i: (i, 0)),
        out_shape=jax.ShapeDtypeStruct((n_tiles * tm, D), F32),
        compiler_params=_params(("parallel",), 32),
        name="outproj_residual",
    )(y_ctx, y_lat, w_bf16, *x_args, mod3)


def _rglru_kernel(gate_ref, xr_ref, cw_ref, cb_ref, wg_ref, bg_ref, lam_ref, h0_ref,
                  y_ref, fin_ref, pad_ref, xc_ref, a_ref, b_ref, yf_ref, *, seq_len, n_seq):
    tm = seq_len * n_seq
    pad_ref[0:8, :] = jnp.zeros((8, D), F32)
    pad_ref[8 + tm:16 + tm, :] = jnp.zeros((8, D), F32)
    pad_ref[8:8 + tm, :] = xr_ref[...]
    t_loc = lax.broadcasted_iota(jnp.int32, (tm, 1), 0) % seq_len
    xm2 = jnp.where(t_loc >= 2, pad_ref[6:6 + tm, :], 0.0)
    xm1 = jnp.where(t_loc >= 1, pad_ref[7:7 + tm, :], 0.0)
    xp1 = jnp.where(t_loc <= seq_len - 2, pad_ref[9:9 + tm, :], 0.0)
    xc = xm2 * cw_ref[0:1, :] + xm1 * cw_ref[1:2, :]
    xc = xc + xr_ref[...] * cw_ref[2:3, :]
    xc = xc + xp1 * cw_ref[3:4, :]
    xc_ref[...] = xc + cb_ref[...]

    if n_seq == 1:
        rows = lambda t: pl.ds(t, 1)
    else:
        rows = lambda t: pl.ds(t, n_seq, stride=seq_len)

    for d in range(2):
        half_decay = _lru_half_decay(lam_ref[d:d + 1, :])
        for q in range(N_GATE_GROUPS):
            cols = slice(q * GATE_GROUP, (q + 1) * GATE_GROUP)
            xq = xc_ref[:, cols]
            half_pre = _dot(xq.astype(BF16), wg_ref[d, q]) + bg_ref[d, q]
            a, b = _lru_coeffs(half_pre, 0.5 * xq, half_decay[:, cols])
            for c in range(GATE_GROUP // LANES):
                lg = q * (GATE_GROUP // LANES) + c
                a_ref[lg] = a[:, c * LANES:(c + 1) * LANES]
                b_ref[lg] = b[:, c * LANES:(c + 1) * LANES]

        h_dst = yf_ref if d == 0 else b_ref

        def step(k, hs, d=d, h_dst=h_dst):
            t = k if d == 0 else seq_len - 1 - k
            idx = rows(t)
            out = []
            for lg in range(D // LANES):
                h = a_ref[lg, idx, :] * hs[lg] + b_ref[lg, idx, :]
                h_dst[lg, idx, :] = h
                out.append(h)
            return tuple(out)

        h0 = h0_ref[d]
        h_init = tuple(h0[:, lg * LANES:(lg + 1) * LANES] for lg in range(D // LANES))
        h_fin = lax.fori_loop(0, seq_len, step, h_init, unroll=8)
        for lg in range(D // LANES):
            fin_ref[d, :, lg * LANES:(lg + 1) * LANES] = h_fin[lg]

    for lg in range(D // LANES):
        cols = slice(lg * LANES, (lg + 1) * LANES)
        y_ref[:, cols] = ((yf_ref[lg] + b_ref[lg]) * _gelu_tanh(gate_ref[:, cols])).astype(BF16)


def _rglru_core(gate, xr, conv_w, conv_b, wg, bg, lam, h0, *, seq_len, n_seq):
    n = gate.shape[0]
    tm = seq_len * n_seq
    n_tiles = n // tm
    kern = functools.partial(_rglru_kernel, seq_len=seq_len, n_seq=n_seq)
    return pl.pallas_call(
        kern,
        grid=(n_tiles,),
        in_specs=[
            pl.BlockSpec((tm, D), lambda i: (i, 0)),
            pl.BlockSpec((tm, D), lambda i: (i, 0)),
            pl.BlockSpec((4, D), lambda i: (0, 0)),
            pl.BlockSpec((1, D), lambda i: (0, 0)),
            pl.BlockSpec((2, N_GATE_GROUPS, GATE_GROUP, 2 * GATE_GROUP), lambda i: (0, 0, 0, 0)),
            pl.BlockSpec((2, N_GATE_GROUPS, 1, 2 * GATE_GROUP), lambda i: (0, 0, 0, 0)),
            pl.BlockSpec((2, D), lambda i: (0, 0)),
            pl.BlockSpec((2, None, n_seq, D), lambda i: (0, i, 0, 0)),
        ],
        out_specs=[
            pl.BlockSpec((tm, D), lambda i: (i, 0)),
            pl.BlockSpec((2, None, n_seq, D), lambda i: (0, i, 0, 0)),
        ],
        out_shape=[
            jax.ShapeDtypeStruct((n, D), BF16),
            jax.ShapeDtypeStruct((2, n_tiles, n_seq, D), F32),
        ],
        scratch_shapes=[
            pltpu.VMEM((tm + 16, D), F32),
            pltpu.VMEM((tm, D), F32),
            pltpu.VMEM((D // LANES, tm, LANES), F32),
            pltpu.VMEM((D // LANES, tm, LANES), F32),
            pltpu.VMEM((D // LANES, tm, LANES), F32),
        ],
        compiler_params=_params(("parallel",), 48),
        name="rglru_core",
    )(gate, xr, conv_w, conv_b, wg, bg, lam, h0)


def _lru_coeffs(half_pre, half_x, half_decay):
    tr = jnp.tanh(half_pre[:, :GATE_GROUP])
    ti = jnp.tanh(half_pre[:, GATE_GROUP:])
    log_a = half_decay * tr + half_decay
    a = jnp.exp(log_a)
    v = -jnp.tanh(log_a) * (1.0 + a * a)
    m = jnp.where(v > 0.0, v * lax.rsqrt(v), 0.0)
    return a, (m * half_x) * (1.0 + ti)


def _lru_half_decay(lam_row):
    z = -lam_row
    return (-0.5 * LRU_C) * (jnp.maximum(z, 0.0) + jnp.log1p(jnp.exp(-jnp.abs(z))))


def _rglru_tm_kernel(gate_ref, xr_ref, cw_ref, cb_ref, wg_ref, bg_ref, lam_ref, y_ref, fin_ref,
                     pad_ref, xc_ref, a0_ref, b0_ref, a1_ref, b1_ref, *, seq_len):
    S = TM_GROUP
    R = seq_len * S
    lead = 2 * S
    n_lg = GATE_GROUP // LANES
    for lg in range(n_lg):
        cols = slice(lg * LANES, (lg + 1) * LANES)
        pad_ref[lg, 0:lead, :] = jnp.zeros((lead, LANES), F32)
        pad_ref[lg, lead + R:lead + R + S, :] = jnp.zeros((S, LANES), F32)
        for s in range(S):
            pad_ref[lg, pl.ds(lead + s, seq_len, stride=S), :] = xr_ref[s * seq_len:(s + 1) * seq_len, cols]
        xc = pad_ref[lg, 0:R, :] * cw_ref[0:1, cols] + pad_ref[lg, S:S + R, :] * cw_ref[1:2, cols]
        xc = xc + pad_ref[lg, lead:lead + R, :] * cw_ref[2:3, cols]
        xc = xc + pad_ref[lg, lead + S:lead + S + R, :] * cw_ref[3:4, cols]
        xc_ref[:, cols] = xc + cb_ref[:, cols]
    xcb = xc_ref[...].astype(BF16)
    for d, (a_ref, b_ref) in enumerate(((a0_ref, b0_ref), (a1_ref, b1_ref))):
        half_pre = _dot(xcb, wg_ref[d]) + bg_ref[d]
        a, b = _lru_coeffs(half_pre, 0.5 * xc_ref[...], _lru_half_decay(lam_ref[d:d + 1, :]))
        a_ref[...] = a
        b_ref[...] = b

    def step(k, carry):
        hf, hb = carry
        rf = pl.ds(pl.multiple_of(k * S, S), S)
        rb = pl.ds(pl.multiple_of((seq_len - 1 - k) * S, S), S)
        hf = a0_ref[rf, :] * hf + b0_ref[rf, :]
        hb = a1_ref[rb, :] * hb + b1_ref[rb, :]
        b0_ref[rf, :] = hf
        b1_ref[rb, :] = hb
        return hf, hb

    zero = jnp.zeros((S, GATE_GROUP), F32)
    hf, hb = lax.fori_loop(0, seq_len, step, (zero, zero), unroll=8)
    fin_ref[0] = hf
    fin_ref[1] = hb
    for lg in range(n_lg):
        cols = slice(lg * LANES, (lg + 1) * LANES)
        pad_ref[lg, 0:R, :] = b0_ref[:, cols] + b1_ref[:, cols]
        for s in range(S):
            rows = slice(s * seq_len, (s + 1) * seq_len)
            ysum = pad_ref[lg, pl.ds(s, seq_len, stride=S), :]
            y_ref[rows, cols] = (ysum * _gelu_tanh(gate_ref[rows, cols])).astype(BF16)


def _rglru_core_tm(gate, xr, conv_w, conv_b, wg, bg, lam, *, seq_len, row_block0=0, n_rows=None):
    n = gate.shape[0] if n_rows is None else n_rows
    rows = seq_len * TM_GROUP
    n_groups = n // rows
    kern = functools.partial(_rglru_tm_kernel, seq_len=seq_len)
    slab_in = pl.BlockSpec((rows, GATE_GROUP), lambda gi, q: (row_block0 + gi, q))
    return pl.pallas_call(
        kern,
        grid=(n_groups, N_GATE_GROUPS),
        in_specs=[
            slab_in,
            slab_in,
            pl.BlockSpec((4, GATE_GROUP), lambda gi, q: (0, q)),
            pl.BlockSpec((1, GATE_GROUP), lambda gi, q: (0, q)),
            pl.BlockSpec((2, None, GATE_GROUP, 2 * GATE_GROUP), lambda gi, q: (0, q, 0, 0)),
            pl.BlockSpec((2, None, 1, 2 * GATE_GROUP), lambda gi, q: (0, q, 0, 0)),
            pl.BlockSpec((2, GATE_GROUP), lambda gi, q: (0, q)),
        ],
        out_specs=[
            pl.BlockSpec((rows, GATE_GROUP), lambda gi, q: (gi, q)),
            pl.BlockSpec((2, None, TM_GROUP, GATE_GROUP), lambda gi, q: (0, gi, 0, q)),
        ],
        out_shape=[
            jax.ShapeDtypeStruct((n, D), BF16),
            jax.ShapeDtypeStruct((2, n_groups, TM_GROUP, D), F32),
        ],
        scratch_shapes=[pltpu.VMEM((GATE_GROUP // LANES, rows + 3 * TM_GROUP, LANES), F32)]
        + [pltpu.VMEM((rows, GATE_GROUP), F32) for _ in range(5)],
        compiler_params=_params(("parallel", "parallel"), 40),
        name="rglru_core_tm",
    )(gate, xr, conv_w, conv_b, wg, bg, lam)


def _gate_weights(gw_a, gb_a, gw_x, gb_x):
    per = GATE_GROUP // LRU_BLOCK

    def dense(w):
        w = w.reshape(2, N_GATE_GROUPS, per, LRU_BLOCK, LRU_BLOCK)
        eye = jnp.eye(per, dtype=w.dtype)
        full = jnp.einsum('dqpkj,pr->dqpkrj', w, eye)
        return full.reshape(2, N_GATE_GROUPS, GATE_GROUP, GATE_GROUP)

    wg = (0.5 * jnp.concatenate([dense(gw_a), dense(gw_x)], axis=-1)).astype(BF16)
    ba = gb_a.reshape(2, N_GATE_GROUPS, 1, GATE_GROUP)
    bx = gb_x.reshape(2, N_GATE_GROUPS, 1, GATE_GROUP)
    return wg, 0.5 * jnp.concatenate([ba, bx], axis=-1)


def _lam_value(lq_ref, lk_ref, lam_init):
    s = jnp.sum(lq_ref[...] * lk_ref[...], axis=-1, keepdims=True)
    e = jnp.exp(s)
    return e[0:1, :] - e[1:2, :] + lam_init


def _softmax_av(s, v_ones):
    e = jnp.exp(s - jnp.max(s, axis=-1, keepdims=True)).astype(BF16)
    ov = _dot(e, v_ones)
    return ov[:, :HEAD_W] * (1.0 / ov[:, HEAD_W:])


def _diff_head(qh, kh_bf16, vh_bf16, lam, g_row, out_scale):
    lane = lax.broadcasted_iota(jnp.int32, qh.shape, 1)
    qs = qh * (MAP_W ** -0.5)
    q0 = jnp.where(lane < MAP_W, qs, 0.0).astype(BF16)
    q1 = jnp.where(lane >= MAP_W, qs, 0.0).astype(BF16)
    v_ones = jnp.concatenate([vh_bf16, jnp.ones_like(vh_bf16)], axis=1)
    o = _softmax_av(_dot_nt(q0, kh_bf16), v_ones) - lam * _softmax_av(_dot_nt(q1, kh_bf16), v_ones)
    return _rms(o, g_row) * out_scale


def _attn_ctx_kernel(q_ref, k_ref, v_ref, lq_ref, lk_ref, g_ref, o_ref, *, lam_init):
    lam = _lam_value(lq_ref, lk_ref, lam_init)
    for hd in range(N_HEADS):
        cols = slice(hd * HEAD_W, (hd + 1) * HEAD_W)
        o = _diff_head(q_ref[:, cols], k_ref[:, cols].astype(BF16), v_ref[:, cols].astype(BF16),
                       lam, g_ref[...], 1.0 - lam_init)
        o_ref[:, cols] = o.astype(BF16)


def _attn_ctx(q, k, v, lq, lk, g, *, seq_len, lam_init):
    n = q.shape[0]
    kern = functools.partial(_attn_ctx_kernel, lam_init=lam_init)
    tok = pl.BlockSpec((seq_len, D), lambda i: (i, 0))
    small = lambda shape: pl.BlockSpec(shape, lambda i: (0, 0))
    return pl.pallas_call(
        kern,
        grid=(n // seq_len,),
        in_specs=[tok, tok, tok, small((2, MAP_W)), small((2, MAP_W)), small((1, HEAD_W))],
        out_specs=tok,
        out_shape=jax.ShapeDtypeStruct((n, D), BF16),
        compiler_params=_params(("parallel",), 32),
        name="attn_context",
    )(q, k, v, lq, lk, g)


def _rope(x, cos, sin_signed):
    lane = lax.broadcasted_iota(jnp.int32, x.shape, 1)
    partner = jnp.where(lane % 32 < 16, pltpu.roll(x, HEAD_W - 16, 1), pltpu.roll(x, 16, 1))
    return x * cos + partner * sin_signed


def _attn_lat_kernel(q_ref, k_ref, v_ref, ck_ref, cv_ref, cos_ref, sin_ref, lq_ref, lk_ref, g_ref,
                     o_ref, kall_ref, vall_ref, *, lam_init, past_len, seq_len, tq):
    qb = pl.program_id(1)

    @pl.when(qb == 0)
    def _():
        kall_ref[0:past_len, :] = ck_ref[...].astype(BF16)
        vall_ref[0:past_len, :] = cv_ref[...].astype(BF16)
        vall_ref[past_len:past_len + seq_len, :] = v_ref[...].astype(BF16)
        for hd in range(N_HEADS):
            cols = slice(hd * HEAD_W, (hd + 1) * HEAD_W)
            kr = _rope(k_ref[:, cols], cos_ref[...], sin_ref[...])
            kall_ref[past_len:past_len + seq_len, cols] = kr.astype(BF16)

    lam = _lam_value(lq_ref, lk_ref, lam_init)
    row0 = pl.multiple_of(qb * tq, tq)
    cos_q = cos_ref[pl.ds(row0, tq), :]
    sin_q = sin_ref[pl.ds(row0, tq), :]
    for hd in range(N_HEADS):
        cols = slice(hd * HEAD_W, (hd + 1) * HEAD_W)
        qh = _rope(q_ref[:, cols], cos_q, sin_q)
        o = _diff_head(qh, kall_ref[:, cols], vall_ref[:, cols], lam, g_ref[...], 1.0 - lam_init)
        o_ref[:, cols] = o.astype(BF16)


def _attn_lat(q, k, v, cache_k, cache_v, cos, sin_signed, lq, lk, g, *, seq_len, lam_init):
    n = q.shape[0]
    n_batch = n // seq_len
    past_len = cache_k.shape[1]
    tq = 256
    n_qb = seq_len // tq
    kern = functools.partial(_attn_lat_kernel, lam_init=lam_init, past_len=past_len,
                             seq_len=seq_len, tq=tq)
    seq = pl.BlockSpec((seq_len, D), lambda b, j: (b, 0))
    cache = pl.BlockSpec((None, past_len, D), lambda b, j: (b, 0, 0))
    small = lambda shape: pl.BlockSpec(shape, lambda b, j: (0, 0))
    qblk = pl.BlockSpec((tq, D), lambda b, j: (b * n_qb + j, 0))
    return pl.pallas_call(
        kern,
        grid=(n_batch, n_qb),
        in_specs=[qblk, seq, seq, cache, cache, small((seq_len, HEAD_W)), small((seq_len, HEAD_W)),
                  small((2, MAP_W)), small((2, MAP_W)), small((1, HEAD_W))],
        out_specs=qblk,
        out_shape=jax.ShapeDtypeStruct((n, D), BF16),
        scratch_shapes=[pltpu.VMEM((past_len + seq_len, D), BF16),
                        pltpu.VMEM((past_len + seq_len, D), BF16)],
        compiler_params=_params(("parallel", "arbitrary"), 48),
        name="attn_latent",
    )(q, k, v, cache_k, cache_v, cos, sin_signed, lq, lk, g)


def _rope_tables(seq_len):
    t = np.arange(seq_len)
    pos = np.stack([t // GRID_W, t % GRID_W], axis=-1).astype(np.float32)
    nf = MAP_W // 4
    inv = (np.float32(ROPE_THETA) ** (-np.arange(nf, dtype=np.float32) / nf)).astype(np.float32)
    lane = np.arange(HEAD_W)
    axis = (lane % MAP_W) // 32
    freq = lane % nf
    sign = np.where(lane % 32 < nf, -1.0, 1.0).astype(np.float32)
    pos_l = jnp.asarray(pos)[:, axis]
    ang = pos_l * jnp.asarray(inv)[freq][None, :]
    return jnp.cos(ang), jnp.sin(ang) * jnp.asarray(sign)[None, :]


def _pool_sequence(x_ref, mod_ref, g_ref, w_ref, sc_ref, o_ref, h_ref, lvl_ref, T):
    R = T + 2 * POOL_PAD
    h = _prenorm(x_ref[...], g_ref[...], mod_ref[0:1, :], mod_ref[1:2, :])
    h_ref[...] = h
    lvl_ref[0:POOL_PAD, :] = jnp.zeros((POOL_PAD, D), F32)
    lvl_ref[POOL_PAD + T:R, :] = jnp.zeros((POOL_PAD, D), F32)
    lvl_ref[POOL_PAD:POOL_PAD + T, :] = h
    t = lax.broadcasted_iota(jnp.int32, (T, 1), 0)
    for gi, win in enumerate(POOL_WINDOWS):
        cols = slice(gi * POOL_GROUP, (gi + 1) * POOL_GROUP)
        half = win // 2
        s = 1
        while s < win:
            lvl_ref[POOL_PAD:R, cols] = lvl_ref[POOL_PAD - s:R - s, cols] + lvl_ref[POOL_PAD:R, cols]
            s *= 2
        wsum = lvl_ref[POOL_PAD + half - 1:POOL_PAD + half - 1 + T, cols]
        cnt = (jnp.minimum(t + half, T) - jnp.maximum(t - half, 0)).astype(F32)
        diff = wsum * (1.0 / cnt) - h_ref[:, cols]
        m = _dot(diff.astype(BF16), w_ref[gi]) * sc_ref[:, cols]
        o_ref[:, cols] = x_ref[:, cols] + mod_ref[2:3, cols] * m


def _pool_kernel(x_ref, mod_ref, g_ref, w_ref, sc_ref, o_ref, h_ref, lvl_ref, *, n_ctx_tiles, ctx_len):
    tile_rows = x_ref.shape[0]
    is_ctx = pl.program_id(0) < n_ctx_tiles

    @pl.when(is_ctx)
    def _():
        for k in range(tile_rows // ctx_len):
            rows = slice(k * ctx_len, (k + 1) * ctx_len)
            _pool_sequence(x_ref.at[rows], mod_ref, g_ref, w_ref, sc_ref, o_ref.at[rows], h_ref.at[rows],
                           lvl_ref.at[0:ctx_len + 2 * POOL_PAD], ctx_len)

    @pl.when(jnp.logical_not(is_ctx))
    def _():
        _pool_sequence(x_ref, mod_ref, g_ref, w_ref, sc_ref, o_ref, h_ref, lvl_ref, tile_rows)


def _pool_mixer(x, mod3, g, w_bf16, scale, *, n_ctx, ctx_len, lat_len):
    n = x.shape[0]
    tm = lat_len
    n_ctx_tiles = n_ctx // tm
    kern = functools.partial(_pool_kernel, n_ctx_tiles=n_ctx_tiles, ctx_len=ctx_len)
    tok = pl.BlockSpec((tm, D), lambda i: (i, 0))
    return pl.pallas_call(
        kern,
        grid=(n // tm,),
        in_specs=[
            tok,
            pl.BlockSpec((None, 3, D), lambda i: (jnp.maximum(i - (n_ctx_tiles - 1), 0), 0, 0)),
            pl.BlockSpec((1, D), lambda i: (0, 0)),
            pl.BlockSpec((len(POOL_WINDOWS), POOL_GROUP, POOL_GROUP), lambda i: (0, 0, 0)),
            pl.BlockSpec((1, D), lambda i: (0, 0)),
        ],
        out_specs=tok,
        out_shape=jax.ShapeDtypeStruct((n, D), F32),
        scratch_shapes=[pltpu.VMEM((tm, D), F32),
                        pltpu.VMEM((tm + 2 * POOL_PAD, D), F32)],
        compiler_params=_params(("parallel",), 40),
        name="pool_mixer",
    )(x, mod3, g, w_bf16, scale)


def kernel(x_prompt, x_sample, state_rglru, cache_k_diff, cache_v_diff, c, c_ctx, norm_g, w_mod, b_mod, w_ffn_in, w_ffn_out, a_w_in, a_conv_w, a_conv_b, a_gate_w_a, a_gate_b_a, a_gate_w_x, a_gate_b_x, a_lambda, a_w_out, b_w_qkv, b_lam_q, b_lam_k, b_subln_g, b_w_o, c_w_pool, c_scale, final_norm_g):
    n_ctx_seq, ctx_len, _ = x_prompt.shape
    n_lat_seq, lat_len, _ = x_sample.shape
    n_ctx, n_lat = n_ctx_seq * ctx_len, n_lat_seq * lat_len
    xp = x_prompt.reshape(n_ctx, D)
    xs = x_sample.reshape(n_lat, D)

    cond8 = jnp.concatenate([c_ctx[None], c, jnp.zeros((8 - 1 - n_lat_seq, D), F32)], axis=0)
    mod = _adaln_all(cond8, w_mod, b_mod).reshape(DEPTH, 8, N_MOD, D)
    final_g = final_norm_g.reshape(1, D)

    assert n_ctx_seq % TM_GROUP == 0 and lat_len == FFN_TM and n_ctx % FFN_TM == 0
    n_ctx_tiles = n_ctx // FFN_TM
    new_states, new_k, new_v = [], [], []
    x = (xp, xs)
    for l in range(DEPTH):
        mod_all = mod[l, 0:1 + n_lat_seq]
        mod_p, mod_s = mod_all[0:1], mod_all[1:]
        g = norm_g[l]
        kind, j = l % 3, l // 3

        def ffn(x, k, half, last):
            return _ffn(x, mod_all[:, 3 * k:3 * k + 3], g[k:k + 1], w_ffn_in, w_ffn_out, final_g,
                        layer=l, half=half, n_ctx_tiles=n_ctx_tiles, final_norm=last, split_out=last)

        x = ffn(x, 0, 0, False)

        g1 = g[1:2]
        mp3, ms3, mall3 = mod_p[:, 3:6], mod_s[:, 3:6], mod_all[:, 3:6]
        if isinstance(x, tuple):
            ctx_rows = dict(x=x[0])
            lat_rows = dict(x=x[1])
        else:
            ctx_rows = dict(x=x, row0=0, n_rows=n_ctx)
            lat_rows = dict(x=x, row0=n_ctx, n_rows=n_lat)
        if kind == 0:
            w_in = a_w_in[j].astype(BF16)
            w_out = a_w_out[j].astype(BF16)
            wg, bg = _gate_weights(a_gate_w_a[j], a_gate_b_a[j], a_gate_w_x[j], a_gate_b_x[j])
            conv_b = a_conv_b[j].reshape(1, D)
            gate_p, xr_p = _prenorm_proj(mod3=mp3, g=g1, w_bf16=w_in, rows_per_group=n_ctx, **ctx_rows)
            y_p, fin = _rglru_core_tm(gate_p, xr_p, a_conv_w[j], conv_b, wg, bg, a_lambda[j],
                                      seq_len=ctx_len)
            new_states.append(jnp.transpose(fin.reshape(2, n_ctx_seq, D), (1, 0, 2)))
            gate_s, xr_s = _prenorm_proj(mod3=ms3, g=g1, w_bf16=w_in, rows_per_group=lat_len, **lat_rows)
            h0 = jnp.transpose(state_rglru[:, j].astype(F32), (1, 0, 2)).reshape(2, n_lat_seq, 1, D)
            y_s, _ = _rglru_core(gate_s, xr_s, a_conv_w[j], conv_b, wg, bg, a_lambda[j], h0,
                                 seq_len=lat_len, n_seq=1)
            x = _outproj_residual(y_p, y_s, w_out, x, mall3, lat_len=lat_len)
        elif kind == 1:
            lam_init = 0.8 - 0.6 * math.exp(-0.3 * l)
            w_qkv = b_w_qkv[j].astype(BF16)
            w_o = b_w_o[j].astype(BF16)
            sub_g = b_subln_g[j].reshape(1, HEAD_W)
            q_p, k_p, v_p = _prenorm_proj(mod3=mp3, g=g1, w_bf16=w_qkv, rows_per_group=n_ctx, **ctx_rows)
            o_p = _attn_ctx(q_p, k_p, v_p, b_lam_q[j], b_lam_k[j], sub_g, seq_len=ctx_len,
                            lam_init=lam_init)
            new_k.append(k_p.reshape(n_ctx_seq, ctx_len, N_HEADS, HEAD_W))
            new_v.append(v_p.reshape(n_ctx_seq, ctx_len, N_HEADS, HEAD_W))
            q_s, k_s, v_s = _prenorm_proj(mod3=ms3, g=g1, w_bf16=w_qkv, rows_per_group=lat_len, **lat_rows)
            cos, sin_signed = _rope_tables(lat_len)
            past_len = cache_k_diff.shape[2]
            o_s = _attn_lat(q_s, k_s, v_s,
                            cache_k_diff[:, j].reshape(n_lat_seq, past_len, D),
                            cache_v_diff[:, j].reshape(n_lat_seq, past_len, D),
                            cos, sin_signed, b_lam_q[j], b_lam_k[j], sub_g,
                            seq_len=lat_len, lam_init=lam_init)
            x = _outproj_residual(o_p, o_s, w_o, x, mall3, lat_len=lat_len)
        else:
            assert not isinstance(x, tuple)
            x = _pool_mixer(x, mall3, g1, c_w_pool[j].astype(BF16), c_scale[j].reshape(1, D),
                            n_ctx=n_ctx, ctx_len=ctx_len, lat_len=lat_len)

        x = ffn(x, 2, 1, l == DEPTH - 1)

    xp, xs = x
    y_prompt = xp.reshape(x_prompt.shape)
    y_sample = xs.reshape(x_sample.shape)
    new_state = jnp.stack(new_states, axis=1).astype(x_prompt.dtype)
    new_cache_k = jnp.stack(new_k, axis=1).astype(x_prompt.dtype)
    new_cache_v = jnp.stack(new_v, axis=1).astype(x_prompt.dtype)
    return (y_prompt, y_sample, new_state, new_cache_k, new_cache_v)
```

```python
import functools
import math

import jax
import jax.numpy as jnp
import numpy as np
from jax import lax
from jax.experimental import pallas as pl
from jax.experimental.pallas import tpu as pltpu

F32 = jnp.float32
BF16 = jnp.bfloat16

D = 1024
DEPTH = 4
N_MOD = 9
EPS = 1e-6
D_FF = 2816
LRU_BLOCK = 64
N_LRU_BLOCKS = 16
LRU_C = 8.0
N_HEADS = 8
LANES = 128
HEAD_W = 128
MAP_W = 64
assert MAP_W ** -0.5 == 0.125
GRID_W = 64
ROPE_THETA = 10000.0
POOL_WINDOWS = (2, 4, 8, 16)
POOL_GROUP = 256
POOL_PAD = 16

GATE_GROUP = 256
N_GATE_GROUPS = D // GATE_GROUP

MIB = 1024 * 1024


def _params(sem, vmem_mib):
    return pltpu.CompilerParams(dimension_semantics=sem, vmem_limit_bytes=vmem_mib * MIB)


def _dot(a, b):
    return jnp.dot(a, b, preferred_element_type=F32)


def _dot_nt(a, b):
    return lax.dot_general(a, b, (((1,), (1,)), ((), ())), preferred_element_type=F32)


def _sigmoid(x):
    return 0.5 + 0.5 * jnp.tanh(0.5 * x)


def _silu(x):
    return x * _sigmoid(x)


def _gelu_tanh(x):
    c = math.sqrt(2.0 / math.pi)
    hx = 0.5 * x
    return hx + hx * jnp.tanh(x * (c + (0.044715 * c) * (x * x)))


def _rms(x, g):
    ms = jnp.mean(x * x, axis=-1, keepdims=True)
    return (x * lax.rsqrt(ms + EPS)) * g


def _prenorm(x, g, shift, scale):
    return _rms(x, g) * (1.0 + scale) + shift


def _mod_kernel(cond_ref, w_ref, b_ref, o_ref):
    s = _silu(cond_ref[...]).astype(BF16)
    o_ref[...] = _dot(s, w_ref[...].astype(BF16)) + b_ref[...]


def _adaln_all(cond8, w_mod, b_mod):
    tn = 1536
    n = N_MOD * D
    return pl.pallas_call(
        _mod_kernel,
        grid=(DEPTH, n // tn),
        in_specs=[
            pl.BlockSpec((8, D), lambda l, j: (0, 0)),
            pl.BlockSpec((None, D, tn), lambda l, j: (l, 0, j)),
            pl.BlockSpec((None, 1, tn), lambda l, j: (l, 0, j)),
        ],
        out_specs=pl.BlockSpec((None, 8, tn), lambda l, j: (l, 0, j)),
        out_shape=jax.ShapeDtypeStruct((DEPTH, 8, n), F32),
        compiler_params=_params(("parallel", "parallel"), 32),
        name="adaln_mod",
    )(cond8, w_mod, b_mod.reshape(DEPTH, 1, n))


FFN_TM = 1024
FFN_FC = 256
FFN_CHUNKS = D_FF // FFN_FC


TM_GROUP = 8
PROJ_TM = 1024


def _ffn_kernel(*refs, final_norm, n_x, n_o, n_ctx_tiles):
    x_refs = refs[:n_x]
    mod_ref, g_ref, wa_ref, wb_ref, wo_ref, fg_ref = refs[n_x:n_x + 6]
    o_refs = refs[n_x + 6:n_x + 6 + n_o]
    wa_s, wb_s, wo_s, h_ref, u_ref = refs[n_x + 6 + n_o:]
    s = pl.program_id(0)
    fc = FFN_FC
    is_ctx = s - (FFN_CHUNKS - 1) < n_ctx_tiles

    def x_tile():
        if n_x == 1:
            return x_refs[0][...]
        return jnp.where(is_ctx, x_refs[0][...], x_refs[1][...])

    def hidden():
        h = _prenorm(x_tile(), g_ref[...], mod_ref[0:1, :], mod_ref[1:2, :])
        h_ref[...] = h.astype(BF16)

    def result(acc):
        y = x_tile() + (0.5 * mod_ref[2:3, :]) * acc
        return _rms(y, fg_ref[...]) if final_norm else y

    @pl.when(s < FFN_CHUNKS)
    def _():
        o_ref = o_refs[0]
        wa = wa_ref[...].astype(BF16)
        wb = wb_ref[...].astype(BF16)
        wo = wo_ref[...].astype(BF16)
        wa_s[s] = wa
        wb_s[s] = wb
        wo_s[pl.ds(pl.multiple_of(s * fc, fc), fc), :] = wo

        @pl.when(s == 0)
        def _():
            hidden()
            o_ref[...] = jnp.zeros_like(o_ref)

        h = h_ref[...]
        u = (_silu(_dot(h, wa)) * _dot(h, wb)).astype(BF16)
        o_ref[...] += _dot(u, wo)

        @pl.when(s == FFN_CHUNKS - 1)
        def _():
            o_ref[...] = result(o_ref[...])

    @pl.when(s >= FFN_CHUNKS)
    def _():
        hidden()
        for j in range(FFN_CHUNKS):
            h = h_ref[...]
            u = _silu(_dot(h, wa_s[j])) * _dot(h, wb_s[j])
            u_ref[:, j * fc:(j + 1) * fc] = u.astype(BF16)
        y = result(_dot(u_ref[...], wo_s[...]))
        if n_o == 1:
            o_refs[0][...] = y
        else:
            @pl.when(is_ctx)
            def _():
                o_refs[0][...] = y

            @pl.when(jnp.logical_not(is_ctx))
            def _():
                o_refs[1][...] = y


def _ffn(x, mod3, g, w_in, w_out, final_g, *, layer, half, n_ctx_tiles, final_norm, split_out):
    tm, fc, nc = FFN_TM, FFN_FC, FFN_CHUNKS
    xs = x if isinstance(x, tuple) else (x,)
    n = sum(a.shape[0] for a in xs)
    n_ctx = n_ctx_tiles * tm
    tile = lambda s: jnp.maximum(s - (nc - 1), 0)
    chunk = lambda s: jnp.minimum(s, nc - 1)
    one_spec = [pl.BlockSpec((tm, D), lambda s: (tile(s), 0))]
    two_specs = [pl.BlockSpec((tm, D), lambda s: (jnp.minimum(tile(s), n_ctx_tiles - 1), 0)),
                 pl.BlockSpec((tm, D), lambda s: (jnp.maximum(tile(s) - n_ctx_tiles, 0), 0))]
    two_shapes = [jax.ShapeDtypeStruct((n_ctx, D), F32), jax.ShapeDtypeStruct((n - n_ctx, D), F32)]
    n_o = 2 if split_out else 1
    kern = functools.partial(_ffn_kernel, final_norm=final_norm, n_x=len(xs), n_o=n_o,
                             n_ctx_tiles=n_ctx_tiles)
    out = pl.pallas_call(
        kern,
        grid=(nc - 1 + n // tm,),
        in_specs=[
            *(two_specs if len(xs) == 2 else one_spec),
            pl.BlockSpec((None, 3, D), lambda s: (jnp.maximum(tile(s) - (n_ctx_tiles - 1), 0), 0, 0)),
            pl.BlockSpec((1, D), lambda s: (0, 0)),
            pl.BlockSpec((None, None, D, fc), lambda s: (layer, half, 0, chunk(s))),
            pl.BlockSpec((None, None, D, fc), lambda s: (layer, half, 0, nc + chunk(s))),
            pl.BlockSpec((None, None, fc, D), lambda s: (layer, half, chunk(s), 0)),
            pl.BlockSpec((1, D), lambda s: (0, 0)),
        ],
        out_specs=two_specs if split_out else one_spec,
        out_shape=two_shapes if split_out else [jax.ShapeDtypeStruct((n, D), F32)],
        scratch_shapes=[
            pltpu.VMEM((nc, D, fc), BF16),
            pltpu.VMEM((nc, D, fc), BF16),
            pltpu.VMEM((D_FF, D), BF16),
            pltpu.VMEM((tm, D), BF16),
            pltpu.VMEM((tm, D_FF), BF16),
        ],
        compiler_params=_params(("arbitrary",), 60),
        name="ffn",
    )(*xs, mod3, g, w_in, w_in, w_out, final_g)
    return tuple(out) if split_out else out[0]


def _proj_kernel(x_ref, mod_ref, g_ref, w_ref, *o_refs):
    h = _prenorm(x_ref[...], g_ref[...], mod_ref[0:1, :], mod_ref[1:2, :]).astype(BF16)
    for k, o_ref in enumerate(o_refs):
        o_ref[...] = _dot(h, w_ref[:, k * D:(k + 1) * D])


def _prenorm_proj(x, mod3, g, w_bf16, *, rows_per_group, row0=0, n_rows=None):
    n = x.shape[0] if n_rows is None else n_rows
    n_out = w_bf16.shape[1] // D
    tm = PROJ_TM
    tiles_per_group = rows_per_group // tm
    tile0 = row0 // tm
    return pl.pallas_call(
        _proj_kernel,
        grid=(n // tm,),
        in_specs=[
            pl.BlockSpec((tm, D), lambda i: (tile0 + i, 0)),
            pl.BlockSpec((None, 3, D), lambda i: (i // tiles_per_group, 0, 0)),
            pl.BlockSpec((1, D), lambda i: (0, 0)),
            pl.BlockSpec((D, n_out * D), lambda i: (0, 0)),
        ],
        out_specs=[pl.BlockSpec((tm, D), lambda i: (i, 0)) for _ in range(n_out)],
        out_shape=[jax.ShapeDtypeStruct((n, D), F32) for _ in range(n_out)],
        compiler_params=_params(("parallel",), 56),
        name="prenorm_proj",
    )(x, mod3, g, w_bf16)


def _outproj_kernel(yc_ref, yl_ref, w_ref, *rest, n_ctx_tiles):
    *x_refs, mod_ref, o_ref = rest
    is_ctx = pl.program_id(0) < n_ctx_tiles
    y = jnp.where(is_ctx, yc_ref[...], yl_ref[...])
    if len(x_refs) == 2:
        x = jnp.where(is_ctx, x_refs[0][...], x_refs[1][...])
    else:
        x = x_refs[0][...]
    o_ref[...] = x + mod_ref[2:3, :] * _dot(y, w_ref[...])


def _outproj_residual(y_ctx, y_lat, w_bf16, x, mod3, *, lat_len):
    tm = PROJ_TM
    n_ctx_tiles = y_ctx.shape[0] // tm
    n_lat_tiles = y_lat.shape[0] // tm
    lat_tiles_per_seq = lat_len // tm
    ctx_tile = lambda i: jnp.minimum(i, n_ctx_tiles - 1)
    lat_tile = lambda i: jnp.maximum(i - n_ctx_tiles, 0)
    group = lambda i: jnp.where(i < n_ctx_tiles, 0, 1 + lat_tile(i) // lat_tiles_per_seq)
    if isinstance(x, tuple):
        x_args = x
        x_specs = [pl.BlockSpec((tm, D), lambda i: (ctx_tile(i), 0)),
                   pl.BlockSpec((tm, D), lambda i: (lat_tile(i), 0))]
    else:
        x_args = (x,)
        x_specs = [pl.BlockSpec((tm, D), lambda i: (i, 0))]
    n_tiles = n_ctx_tiles + n_lat_tiles
    return pl.pallas_call(
        functools.partial(_outproj_kernel, n_ctx_tiles=n_ctx_tiles),
        grid=(n_tiles,),
        in_specs=[
            pl.BlockSpec((tm, D), lambda i: (ctx_tile(i), 0)),
            pl.BlockSpec((tm, D), lambda i: (lat_tile(i), 0)),
            pl.BlockSpec((D, D), lambda i: (0, 0)),
            *x_specs,
            pl.BlockSpec((None, 3, D), lambda i: (group(i), 0, 0)),
        ],
        out_specs=pl.BlockSpec((tm, D), lambda i: (i, 0)),
        out_shape=jax.ShapeDtypeStruct((n_tiles * tm, D), F32),
        compiler_params=_params(("parallel",), 48),
        name="outproj_residual",
    )(y_ctx, y_lat, w_bf16, *x_args, mod3)


def _rglru_kernel(gate_ref, xr_ref, cw_ref, cb_ref, wg_ref, bg_ref, lam_ref, h0_ref,
                  y_ref, fin_ref, pad_ref, xc_ref, a0_ref, b0_ref, a1_ref, b1_ref, *, seq_len, n_seq):
    tm = seq_len * n_seq
    pad_ref[0:8, :] = jnp.zeros((8, D), F32)
    pad_ref[8 + tm:16 + tm, :] = jnp.zeros((8, D), F32)
    pad_ref[8:8 + tm, :] = xr_ref[...]
    t_loc = lax.broadcasted_iota(jnp.int32, (tm, 1), 0) % seq_len
    xm2 = jnp.where(t_loc >= 2, pad_ref[6:6 + tm, :], 0.0)
    xm1 = jnp.where(t_loc >= 1, pad_ref[7:7 + tm, :], 0.0)
    xp1 = jnp.where(t_loc <= seq_len - 2, pad_ref[9:9 + tm, :], 0.0)
    xc = xm2 * cw_ref[0:1, :] + xm1 * cw_ref[1:2, :]
    xc = xc + xr_ref[...] * cw_ref[2:3, :]
    xc = xc + xp1 * cw_ref[3:4, :]
    xc_ref[...] = xc + cb_ref[...]

    if n_seq == 1:
        rows = lambda t: pl.ds(t, 1)
    else:
        rows = lambda t: pl.ds(t, n_seq, stride=seq_len)

    n_lg = D // LANES
    ab_refs = ((a0_ref, b0_ref), (a1_ref, b1_ref))
    for d, (a_ref, b_ref) in enumerate(ab_refs):
        half_decay = _lru_half_decay(lam_ref[d:d + 1, :])
        for q in range(N_GATE_GROUPS):
            cols = slice(q * GATE_GROUP, (q + 1) * GATE_GROUP)
            xq = xc_ref[:, cols]
            half_pre = _dot(xq.astype(BF16), wg_ref[d, q]) + bg_ref[d, q]
            a, b = _lru_coeffs(half_pre, 0.5 * xq, half_decay[:, cols])
            for c in range(GATE_GROUP // LANES):
                lg = q * (GATE_GROUP // LANES) + c
                a_ref[lg] = a[:, c * LANES:(c + 1) * LANES]
                b_ref[lg] = b[:, c * LANES:(c + 1) * LANES]

    def step(k, carry):
        nxt = []
        for (a_ref, b_ref), hs, t in zip(ab_refs, carry, (k, seq_len - 1 - k)):
            idx = rows(t)
            out = []
            for lg in range(n_lg):
                h = a_ref[lg, idx, :] * hs[lg] + b_ref[lg, idx, :]
                b_ref[lg, idx, :] = h
                out.append(h)
            nxt.append(tuple(out))
        return tuple(nxt)

    h_init = tuple(tuple(h0_ref[d][:, lg * LANES:(lg + 1) * LANES] for lg in range(n_lg)) for d in range(2))
    h_fin = lax.fori_loop(0, seq_len, step, h_init, unroll=8)
    for d in range(2):
        for lg in range(n_lg):
            fin_ref[d, :, lg * LANES:(lg + 1) * LANES] = h_fin[d][lg]

    for lg in range(n_lg):
        cols = slice(lg * LANES, (lg + 1) * LANES)
        y_ref[:, cols] = ((b0_ref[lg] + b1_ref[lg]) * _gelu_tanh(gate_ref[:, cols])).astype(BF16)


def _rglru_core(gate, xr, conv_w, conv_b, wg, bg, lam, h0, *, seq_len, n_seq):
    n = gate.shape[0]
    tm = seq_len * n_seq
    n_tiles = n // tm
    kern = functools.partial(_rglru_kernel, seq_len=seq_len, n_seq=n_seq)
    return pl.pallas_call(
        kern,
        grid=(n_tiles,),
        in_specs=[
            pl.BlockSpec((tm, D), lambda i: (i, 0)),
            pl.BlockSpec((tm, D), lambda i: (i, 0)),
            pl.BlockSpec((4, D), lambda i: (0, 0)),
            pl.BlockSpec((1, D), lambda i: (0, 0)),
            pl.BlockSpec((2, N_GATE_GROUPS, GATE_GROUP, 2 * GATE_GROUP), lambda i: (0, 0, 0, 0)),
            pl.BlockSpec((2, N_GATE_GROUPS, 1, 2 * GATE_GROUP), lambda i: (0, 0, 0, 0)),
            pl.BlockSpec((2, D), lambda i: (0, 0)),
            pl.BlockSpec((2, None, n_seq, D), lambda i: (0, i, 0, 0)),
        ],
        out_specs=[
            pl.BlockSpec((tm, D), lambda i: (i, 0)),
            pl.BlockSpec((2, None, n_seq, D), lambda i: (0, i, 0, 0)),
        ],
        out_shape=[
            jax.ShapeDtypeStruct((n, D), BF16),
            jax.ShapeDtypeStruct((2, n_tiles, n_seq, D), F32),
        ],
        scratch_shapes=[
            pltpu.VMEM((tm + 16, D), F32),
            pltpu.VMEM((tm, D), F32),
        ] + [pltpu.VMEM((D // LANES, tm, LANES), F32) for _ in range(4)],
        compiler_params=_params(("parallel",), 56),
        name="rglru_core",
    )(gate, xr, conv_w, conv_b, wg, bg, lam, h0)


def _lru_coeffs(half_pre, half_x, half_decay):
    tr = jnp.tanh(half_pre[:, :GATE_GROUP])
    ti = jnp.tanh(half_pre[:, GATE_GROUP:])
    log_a = half_decay * tr + half_decay
    a = jnp.exp(log_a)
    v = -jnp.tanh(log_a) * (1.0 + a * a)
    m = jnp.where(v > 0.0, v * lax.rsqrt(v), 0.0)
    return a, (m * half_x) * (1.0 + ti)


def _lru_half_decay(lam_row):
    z = -lam_row
    return (-0.5 * LRU_C) * (jnp.maximum(z, 0.0) + jnp.log1p(jnp.exp(-jnp.abs(z))))


def _rglru_tm_kernel(gate_ref, xr_ref, cw_ref, cb_ref, wg_ref, bg_ref, lam_ref, y_ref, fin_ref,
                     pad_ref, xc_ref, a0_ref, b0_ref, a1_ref, b1_ref, *, seq_len):
    S = TM_GROUP
    R = seq_len * S
    lead = 2 * S
    n_lg = GATE_GROUP // LANES
    for lg in range(n_lg):
        cols = slice(lg * LANES, (lg + 1) * LANES)
        pad_ref[lg, 0:lead, :] = jnp.zeros((lead, LANES), F32)
        pad_ref[lg, lead + R:lead + R + S, :] = jnp.zeros((S, LANES), F32)
        for s in range(S):
            pad_ref[lg, pl.ds(lead + s, seq_len, stride=S), :] = xr_ref[s * seq_len:(s + 1) * seq_len, cols]
        xc = pad_ref[lg, 0:R, :] * cw_ref[0:1, cols] + pad_ref[lg, S:S + R, :] * cw_ref[1:2, cols]
        xc = xc + pad_ref[lg, lead:lead + R, :] * cw_ref[2:3, cols]
        xc = xc + pad_ref[lg, lead + S:lead + S + R, :] * cw_ref[3:4, cols]
        xc_ref[:, cols] = xc + cb_ref[:, cols]
    xcb = xc_ref[...].astype(BF16)
    for d, (a_ref, b_ref) in enumerate(((a0_ref, b0_ref), (a1_ref, b1_ref))):
        half_pre = _dot(xcb, wg_ref[d]) + bg_ref[d]
        a, b = _lru_coeffs(half_pre, 0.5 * xc_ref[...], _lru_half_decay(lam_ref[d:d + 1, :]))
        a_ref[...] = a
        b_ref[...] = b

    def step(k, carry):
        hf, hb = carry
        rf = pl.ds(pl.multiple_of(k * S, S), S)
        rb = pl.ds(pl.multiple_of((seq_len - 1 - k) * S, S), S)
        hf = a0_ref[rf, :] * hf + b0_ref[rf, :]
        hb = a1_ref[rb, :] * hb + b1_ref[rb, :]
        b0_ref[rf, :] = hf
        b1_ref[rb, :] = hb
        return hf, hb

    zero = jnp.zeros((S, GATE_GROUP), F32)
    hf, hb = lax.fori_loop(0, seq_len, step, (zero, zero), unroll=8)
    fin_ref[0] = hf
    fin_ref[1] = hb
    for lg in range(n_lg):
        cols = slice(lg * LANES, (lg + 1) * LANES)
        pad_ref[lg, 0:R, :] = b0_ref[:, cols] + b1_ref[:, cols]
        for s in range(S):
            rows = slice(s * seq_len, (s + 1) * seq_len)
            ysum = pad_ref[lg, pl.ds(s, seq_len, stride=S), :]
            y_ref[rows, cols] = (ysum * _gelu_tanh(gate_ref[rows, cols])).astype(BF16)


def _rglru_core_tm(gate, xr, conv_w, conv_b, wg, bg, lam, *, seq_len, row_block0=0, n_rows=None):
    n = gate.shape[0] if n_rows is None else n_rows
    rows = seq_len * TM_GROUP
    n_groups = n // rows
    kern = functools.partial(_rglru_tm_kernel, seq_len=seq_len)
    slab_in = pl.BlockSpec((rows, GATE_GROUP), lambda gi, q: (row_block0 + gi, q))
    return pl.pallas_call(
        kern,
        grid=(n_groups, N_GATE_GROUPS),
        in_specs=[
            slab_in,
            slab_in,
            pl.BlockSpec((4, GATE_GROUP), lambda gi, q: (0, q)),
            pl.BlockSpec((1, GATE_GROUP), lambda gi, q: (0, q)),
            pl.BlockSpec((2, None, GATE_GROUP, 2 * GATE_GROUP), lambda gi, q: (0, q, 0, 0)),
            pl.BlockSpec((2, None, 1, 2 * GATE_GROUP), lambda gi, q: (0, q, 0, 0)),
            pl.BlockSpec((2, GATE_GROUP), lambda gi, q: (0, q)),
        ],
        out_specs=[
            pl.BlockSpec((rows, GATE_GROUP), lambda gi, q: (gi, q)),
            pl.BlockSpec((2, None, TM_GROUP, GATE_GROUP), lambda gi, q: (0, gi, 0, q)),
        ],
        out_shape=[
            jax.ShapeDtypeStruct((n, D), BF16),
            jax.ShapeDtypeStruct((2, n_groups, TM_GROUP, D), F32),
        ],
        scratch_shapes=[pltpu.VMEM((GATE_GROUP // LANES, rows + 3 * TM_GROUP, LANES), F32)]
        + [pltpu.VMEM((rows, GATE_GROUP), F32) for _ in range(5)],
        compiler_params=_params(("parallel", "parallel"), 40),
        name="rglru_core_tm",
    )(gate, xr, conv_w, conv_b, wg, bg, lam)


def _gate_weights(gw_a, gb_a, gw_x, gb_x):
    per = GATE_GROUP // LRU_BLOCK
    blk = np.arange(GATE_GROUP) // LRU_BLOCK
    on_diagonal = jnp.asarray(blk[:, None] == blk[None, :])

    def dense(w):
        rows = w.reshape(2, N_GATE_GROUPS, GATE_GROUP, LRU_BLOCK)
        return jnp.where(on_diagonal, jnp.concatenate([rows] * per, axis=-1), 0.0)

    wg = (0.5 * jnp.concatenate([dense(gw_a), dense(gw_x)], axis=-1)).astype(BF16)
    ba = gb_a.reshape(2, N_GATE_GROUPS, 1, GATE_GROUP)
    bx = gb_x.reshape(2, N_GATE_GROUPS, 1, GATE_GROUP)
    return wg, 0.5 * jnp.concatenate([ba, bx], axis=-1)


def _lam_value(lq_ref, lk_ref, lam_init):
    s = jnp.sum(lq_ref[...] * lk_ref[...], axis=-1, keepdims=True)
    e = jnp.exp(s)
    return e[0:1, :] - e[1:2, :] + lam_init


def _softmax_av(s, v_ones):
    e = jnp.exp(s - jnp.max(s, axis=-1, keepdims=True)).astype(BF16)
    ov = _dot(e, v_ones)
    return ov[:, :HEAD_W] * (1.0 / ov[:, HEAD_W:])


def _diff_head(qh, kh_bf16, vh_bf16, lam, g_row, out_scale):
    lane = lax.broadcasted_iota(jnp.int32, qh.shape, 1)
    qs = qh * (MAP_W ** -0.5)
    q0 = jnp.where(lane < MAP_W, qs, 0.0).astype(BF16)
    q1 = jnp.where(lane >= MAP_W, qs, 0.0).astype(BF16)
    v_ones = jnp.concatenate([vh_bf16, jnp.ones_like(vh_bf16)], axis=1)
    o = _softmax_av(_dot_nt(q0, kh_bf16), v_ones) - lam * _softmax_av(_dot_nt(q1, kh_bf16), v_ones)
    return _rms(o, g_row) * out_scale


def _attn_ctx_kernel(q_ref, k_ref, v_ref, lq_ref, lk_ref, g_ref, o_ref, *, lam_init):
    lam = _lam_value(lq_ref, lk_ref, lam_init)
    for hd in range(N_HEADS):
        cols = slice(hd * HEAD_W, (hd + 1) * HEAD_W)
        o = _diff_head(q_ref[:, cols], k_ref[:, cols].astype(BF16), v_ref[:, cols].astype(BF16),
                       lam, g_ref[...], 1.0 - lam_init)
        o_ref[:, cols] = o.astype(BF16)


def _attn_ctx(q, k, v, lq, lk, g, *, seq_len, lam_init):
    n = q.shape[0]
    kern = functools.partial(_attn_ctx_kernel, lam_init=lam_init)
    tok = pl.BlockSpec((seq_len, D), lambda i: (i, 0))
    small = lambda shape: pl.BlockSpec(shape, lambda i: (0, 0))
    return pl.pallas_call(
        kern,
        grid=(n // seq_len,),
        in_specs=[tok, tok, tok, small((2, MAP_W)), small((2, MAP_W)), small((1, HEAD_W))],
        out_specs=tok,
        out_shape=jax.ShapeDtypeStruct((n, D), BF16),
        compiler_params=_params(("parallel",), 32),
        name="attn_context",
    )(q, k, v, lq, lk, g)


def _rope(x, cos, sin_signed):
    lane = lax.broadcasted_iota(jnp.int32, x.shape, 1)
    partner = jnp.where(lane % 32 < 16, pltpu.roll(x, HEAD_W - 16, 1), pltpu.roll(x, 16, 1))
    return x * cos + partner * sin_signed


def _attn_lat_kernel(q_ref, k_ref, v_ref, ck_ref, cv_ref, cos_ref, sin_ref, lq_ref, lk_ref, g_ref,
                     o_ref, kall_ref, vall_ref, *, lam_init, past_len, seq_len, tq):
    qb = pl.program_id(1)

    @pl.when(qb == 0)
    def _():
        kall_ref[0:past_len, :] = ck_ref[...].astype(BF16)
        vall_ref[0:past_len, :] = cv_ref[...].astype(BF16)
        vall_ref[past_len:past_len + seq_len, :] = v_ref[...].astype(BF16)
        for hd in range(N_HEADS):
            cols = slice(hd * HEAD_W, (hd + 1) * HEAD_W)
            kr = _rope(k_ref[:, cols], cos_ref[...], sin_ref[...])
            kall_ref[past_len:past_len + seq_len, cols] = kr.astype(BF16)

    lam = _lam_value(lq_ref, lk_ref, lam_init)
    row0 = pl.multiple_of(qb * tq, tq)
    cos_q = cos_ref[pl.ds(row0, tq), :]
    sin_q = sin_ref[pl.ds(row0, tq), :]
    for hd in range(N_HEADS):
        cols = slice(hd * HEAD_W, (hd + 1) * HEAD_W)
        qh = _rope(q_ref[:, cols], cos_q, sin_q)
        o = _diff_head(qh, kall_ref[:, cols], vall_ref[:, cols], lam, g_ref[...], 1.0 - lam_init)
        o_ref[:, cols] = o.astype(BF16)


def _attn_lat(q, k, v, cache_k, cache_v, cos, sin_signed, lq, lk, g, *, seq_len, lam_init):
    n = q.shape[0]
    n_batch = n // seq_len
    past_len = cache_k.shape[1]
    tq = 256
    n_qb = seq_len // tq
    kern = functools.partial(_attn_lat_kernel, lam_init=lam_init, past_len=past_len,
                             seq_len=seq_len, tq=tq)
    seq = pl.BlockSpec((seq_len, D), lambda b, j: (b, 0))
    cache = pl.BlockSpec((None, past_len, D), lambda b, j: (b, 0, 0))
    small = lambda shape: pl.BlockSpec(shape, lambda b, j: (0, 0))
    qblk = pl.BlockSpec((tq, D), lambda b, j: (b * n_qb + j, 0))
    return pl.pallas_call(
        kern,
        grid=(n_batch, n_qb),
        in_specs=[qblk, seq, seq, cache, cache, small((seq_len, HEAD_W)), small((seq_len, HEAD_W)),
                  small((2, MAP_W)), small((2, MAP_W)), small((1, HEAD_W))],
        out_specs=qblk,
        out_shape=jax.ShapeDtypeStruct((n, D), BF16),
        scratch_shapes=[pltpu.VMEM((past_len + seq_len, D), BF16),
                        pltpu.VMEM((past_len + seq_len, D), BF16)],
        compiler_params=_params(("parallel", "arbitrary"), 48),
        name="attn_latent",
    )(q, k, v, cache_k, cache_v, cos, sin_signed, lq, lk, g)


def _rope_tables(seq_len):
    t = np.arange(seq_len)
    pos = np.stack([t // GRID_W, t % GRID_W], axis=-1).astype(np.float32)
    nf = MAP_W // 4
    inv = (np.float32(ROPE_THETA) ** (-np.arange(nf, dtype=np.float32) / nf)).astype(np.float32)
    lane = np.arange(HEAD_W)
    axis = (lane % MAP_W) // 32
    freq = lane % nf
    sign = np.where(lane % 32 < nf, -1.0, 1.0).astype(np.float32)
    pos_l = jnp.asarray(pos)[:, axis]
    ang = pos_l * jnp.asarray(inv)[freq][None, :]
    return jnp.cos(ang), jnp.sin(ang) * jnp.asarray(sign)[None, :]


def _pool_sequence(x_ref, mod_ref, g_ref, w_ref, sc_ref, o_ref, h_ref, lvl_ref, T):
    R = T + 2 * POOL_PAD
    h = _prenorm(x_ref[...], g_ref[...], mod_ref[0:1, :], mod_ref[1:2, :])
    h_ref[...] = h
    lvl_ref[0:POOL_PAD, :] = jnp.zeros((POOL_PAD, D), F32)
    lvl_ref[POOL_PAD + T:R, :] = jnp.zeros((POOL_PAD, D), F32)
    lvl_ref[POOL_PAD:POOL_PAD + T, :] = h
    t = lax.broadcasted_iota(jnp.int32, (T, 1), 0)
    for gi, win in enumerate(POOL_WINDOWS):
        cols = slice(gi * POOL_GROUP, (gi + 1) * POOL_GROUP)
        half = win // 2
        s = 1
        while s < win:
            lvl_ref[POOL_PAD:R, cols] = lvl_ref[POOL_PAD - s:R - s, cols] + lvl_ref[POOL_PAD:R, cols]
            s *= 2
        wsum = lvl_ref[POOL_PAD + half - 1:POOL_PAD + half - 1 + T, cols]
        cnt = (jnp.minimum(t + half, T) - jnp.maximum(t - half, 0)).astype(F32)
        diff = wsum * (1.0 / cnt) - h_ref[:, cols]
        m = _dot(diff.astype(BF16), w_ref[gi]) * sc_ref[:, cols]
        o_ref[:, cols] = x_ref[:, cols] + mod_ref[2:3, cols] * m


def _pool_kernel(x_ref, mod_ref, g_ref, w_ref, sc_ref, o_ref, h_ref, lvl_ref, *, n_ctx_tiles, ctx_len):
    tile_rows = x_ref.shape[0]
    is_ctx = pl.program_id(0) < n_ctx_tiles

    @pl.when(is_ctx)
    def _():
        for k in range(tile_rows // ctx_len):
            rows = slice(k * ctx_len, (k + 1) * ctx_len)
            _pool_sequence(x_ref.at[rows], mod_ref, g_ref, w_ref, sc_ref, o_ref.at[rows], h_ref.at[rows],
                           lvl_ref.at[0:ctx_len + 2 * POOL_PAD], ctx_len)

    @pl.when(jnp.logical_not(is_ctx))
    def _():
        _pool_sequence(x_ref, mod_ref, g_ref, w_ref, sc_ref, o_ref, h_ref, lvl_ref, tile_rows)


def _pool_mixer(x, mod3, g, w_bf16, scale, *, n_ctx, ctx_len, lat_len):
    n = x.shape[0]
    tm = lat_len
    n_ctx_tiles = n_ctx // tm
    kern = functools.partial(_pool_kernel, n_ctx_tiles=n_ctx_tiles, ctx_len=ctx_len)
    tok = pl.BlockSpec((tm, D), lambda i: (i, 0))
    return pl.pallas_call(
        kern,
        grid=(n // tm,),
        in_specs=[
            tok,
            pl.BlockSpec((None, 3, D), lambda i: (jnp.maximum(i - (n_ctx_tiles - 1), 0), 0, 0)),
            pl.BlockSpec((1, D), lambda i: (0, 0)),
            pl.BlockSpec((len(POOL_WINDOWS), POOL_GROUP, POOL_GROUP), lambda i: (0, 0, 0)),
            pl.BlockSpec((1, D), lambda i: (0, 0)),
        ],
        out_specs=tok,
        out_shape=jax.ShapeDtypeStruct((n, D), F32),
        scratch_shapes=[pltpu.VMEM((tm, D), F32),
                        pltpu.VMEM((tm + 2 * POOL_PAD, D), F32)],
        compiler_params=_params(("parallel",), 40),
        name="pool_mixer",
    )(x, mod3, g, w_bf16, scale)


def kernel(x_prompt, x_sample, state_rglru, cache_k_diff, cache_v_diff, c, c_ctx, norm_g, w_mod, b_mod, w_ffn_in, w_ffn_out, a_w_in, a_conv_w, a_conv_b, a_gate_w_a, a_gate_b_a, a_gate_w_x, a_gate_b_x, a_lambda, a_w_out, b_w_qkv, b_lam_q, b_lam_k, b_subln_g, b_w_o, c_w_pool, c_scale, final_norm_g):
    n_ctx_seq, ctx_len, _ = x_prompt.shape
    n_lat_seq, lat_len, _ = x_sample.shape
    n_ctx, n_lat = n_ctx_seq * ctx_len, n_lat_seq * lat_len
    xp = x_prompt.reshape(n_ctx, D)
    xs = x_sample.reshape(n_lat, D)

    cond8 = jnp.concatenate([c_ctx[None], c, jnp.zeros((8 - 1 - n_lat_seq, D), F32)], axis=0)
    mod = _adaln_all(cond8, w_mod, b_mod).reshape(DEPTH, 8, N_MOD, D)
    final_g = final_norm_g.reshape(1, D)

    assert n_ctx_seq % TM_GROUP == 0 and lat_len == FFN_TM and n_ctx % FFN_TM == 0
    n_ctx_tiles = n_ctx // FFN_TM
    new_states, new_k, new_v = [], [], []
    x = (xp, xs)
    for l in range(DEPTH):
        mod_all = mod[l, 0:1 + n_lat_seq]
        mod_p, mod_s = mod_all[0:1], mod_all[1:]
        g = norm_g[l]
        kind, j = l % 3, l // 3

        def ffn(x, k, half, last):
            return _ffn(x, mod_all[:, 3 * k:3 * k + 3], g[k:k + 1], w_ffn_in, w_ffn_out, final_g,
                        layer=l, half=half, n_ctx_tiles=n_ctx_tiles, final_norm=last, split_out=last)

        x = ffn(x, 0, 0, False)

        g1 = g[1:2]
        mp3, ms3, mall3 = mod_p[:, 3:6], mod_s[:, 3:6], mod_all[:, 3:6]
        if isinstance(x, tuple):
            ctx_rows = dict(x=x[0])
            lat_rows = dict(x=x[1])
        else:
            ctx_rows = dict(x=x, row0=0, n_rows=n_ctx)
            lat_rows = dict(x=x, row0=n_ctx, n_rows=n_lat)
        if kind == 0:
            w_in = a_w_in[j].astype(BF16)
            w_out = a_w_out[j].astype(BF16)
            wg, bg = _gate_weights(a_gate_w_a[j], a_gate_b_a[j], a_gate_w_x[j], a_gate_b_x[j])
            conv_b = a_conv_b[j].reshape(1, D)
            gate_p, xr_p = _prenorm_proj(mod3=mp3, g=g1, w_bf16=w_in, rows_per_group=n_ctx, **ctx_rows)
            y_p, fin = _rglru_core_tm(gate_p, xr_p, a_conv_w[j], conv_b, wg, bg, a_lambda[j],
                                      seq_len=ctx_len)
            new_states.append(jnp.transpose(fin.reshape(2, n_ctx_seq, D), (1, 0, 2)))
            gate_s, xr_s = _prenorm_proj(mod3=ms3, g=g1, w_bf16=w_in, rows_per_group=lat_len, **lat_rows)
            h0 = jnp.transpose(state_rglru[:, j].astype(F32), (1, 0, 2)).reshape(2, n_lat_seq, 1, D)
            y_s, _ = _rglru_core(gate_s, xr_s, a_conv_w[j], conv_b, wg, bg, a_lambda[j], h0,
                                 seq_len=lat_len, n_seq=1)
            x = _outproj_residual(y_p, y_s, w_out, x, mall3, lat_len=lat_len)
        elif kind == 1:
            lam_init = 0.8 - 0.6 * math.exp(-0.3 * l)
            w_qkv = b_w_qkv[j].astype(BF16)
            w_o = b_w_o[j].astype(BF16)
            sub_g = b_subln_g[j].reshape(1, HEAD_W)
            q_p, k_p, v_p = _prenorm_proj(mod3=mp3, g=g1, w_bf16=w_qkv, rows_per_group=n_ctx, **ctx_rows)
            o_p = _attn_ctx(q_p, k_p, v_p, b_lam_q[j], b_lam_k[j], sub_g, seq_len=ctx_len,
                            lam_init=lam_init)
            new_k.append(k_p.reshape(n_ctx_seq, ctx_len, N_HEADS, HEAD_W))
            new_v.append(v_p.reshape(n_ctx_seq, ctx_len, N_HEADS, HEAD_W))
            q_s, k_s, v_s = _prenorm_proj(mod3=ms3, g=g1, w_bf16=w_qkv, rows_per_group=lat_len, **lat_rows)
            cos, sin_signed = _rope_tables(lat_len)
            past_len = cache_k_diff.shape[2]
            o_s = _attn_lat(q_s, k_s, v_s,
                            cache_k_diff[:, j].reshape(n_lat_seq, past_len, D),
                            cache_v_diff[:, j].reshape(n_lat_seq, past_len, D),
                            cos, sin_signed, b_lam_q[j], b_lam_k[j], sub_g,
                            seq_len=lat_len, lam_init=lam_init)
            x = _outproj_residual(o_p, o_s, w_o, x, mall3, lat_len=lat_len)
        else:
            assert not isinstance(x, tuple)
            x = _pool_mixer(x, mall3, g1, c_w_pool[j].astype(BF16), c_scale[j].reshape(1, D),
                            n_ctx=n_ctx, ctx_len=ctx_len, lat_len=lat_len)

        x = ffn(x, 2, 1, l == DEPTH - 1)

    xp, xs = x
    y_prompt = xp.reshape(x_prompt.shape)
    y_sample = xs.reshape(x_sample.shape)
    new_state = jnp.stack(new_states, axis=1).astype(x_prompt.dtype)
    new_cache_k = jnp.stack(new_k, axis=1).astype(x_prompt.dtype)
    new_cache_v = jnp.stack(new_v, axis=1).astype(x_prompt.dtype)
    return (y_prompt, y_sample, new_state, new_cache_k, new_cache_v)
```

```python
import functools
import math

import jax
import jax.numpy as jnp
import numpy as np
from jax import lax
from jax.experimental import pallas as pl
from jax.experimental.pallas import tpu as pltpu

F32 = jnp.float32
BF16 = jnp.bfloat16

D = 1024
DEPTH = 4
N_MOD = 9
EPS = 1e-6
D_FF = 2816
LRU_BLOCK = 64
N_LRU_BLOCKS = 16
LRU_C = 8.0
N_HEADS = 8
LANES = 128
HEAD_W = 128
MAP_W = 64
assert MAP_W ** -0.5 == 0.125
GRID_W = 64
ROPE_THETA = 10000.0
POOL_WINDOWS = (2, 4, 8, 16)
POOL_GROUP = 256
POOL_PAD = 16

GATE_GROUP = 256
N_GATE_GROUPS = D // GATE_GROUP

MIB = 1024 * 1024


def _params(sem, vmem_mib):
    return pltpu.CompilerParams(dimension_semantics=sem, vmem_limit_bytes=vmem_mib * MIB)


def _dot(a, b):
    return jnp.dot(a, b, preferred_element_type=F32)


def _dot_nt(a, b):
    return lax.dot_general(a, b, (((1,), (1,)), ((), ())), preferred_element_type=F32)


def _sigmoid(x):
    return 0.5 + 0.5 * jnp.tanh(0.5 * x)


def _silu(x):
    return x * _sigmoid(x)


def _gelu_tanh(x):
    c = math.sqrt(2.0 / math.pi)
    hx = 0.5 * x
    return hx + hx * jnp.tanh(x * (c + (0.044715 * c) * (x * x)))


def _rms(x, g):
    ms = jnp.mean(x * x, axis=-1, keepdims=True)
    return (x * lax.rsqrt(ms + EPS)) * g


def _prenorm(x, g, shift, scale):
    return _rms(x, g) * (1.0 + scale) + shift


def _mod_kernel(cond_ref, w_ref, b_ref, o_ref):
    s = _silu(cond_ref[...]).astype(BF16)
    o_ref[...] = _dot(s, w_ref[...].astype(BF16)) + b_ref[...]


def _adaln_all(cond8, w_mod, b_mod):
    tn = 1536
    n = N_MOD * D
    return pl.pallas_call(
        _mod_kernel,
        grid=(DEPTH, n // tn),
        in_specs=[
            pl.BlockSpec((8, D), lambda l, j: (0, 0)),
            pl.BlockSpec((None, D, tn), lambda l, j: (l, 0, j)),
            pl.BlockSpec((None, 1, tn), lambda l, j: (l, 0, j)),
        ],
        out_specs=pl.BlockSpec((None, 8, tn), lambda l, j: (l, 0, j)),
        out_shape=jax.ShapeDtypeStruct((DEPTH, 8, n), F32),
        compiler_params=_params(("parallel", "parallel"), 32),
        name="adaln_mod",
    )(cond8, w_mod, b_mod.reshape(DEPTH, 1, n))


FFN_TM = 1024
FFN_FC = 256
FFN_CHUNKS = D_FF // FFN_FC


TM_GROUP = 8
PROJ_TM = 1024


def _ffn_kernel(*refs, final_norm, n_x, n_o, n_ctx_tiles):
    x_refs = refs[:n_x]
    mod_ref, g_ref, wa_ref, wb_ref, wo_ref, fg_ref = refs[n_x:n_x + 6]
    o_refs = refs[n_x + 6:n_x + 6 + n_o]
    wa_s, wb_s, wo_s, h_ref, u_ref = refs[n_x + 6 + n_o:]
    s = pl.program_id(0)
    fc = FFN_FC
    is_ctx = s - (FFN_CHUNKS - 1) < n_ctx_tiles

    def x_tile():
        if n_x == 1:
            return x_refs[0][...]
        return jnp.where(is_ctx, x_refs[0][...], x_refs[1][...])

    def hidden():
        h = _prenorm(x_tile(), g_ref[...], mod_ref[0:1, :], mod_ref[1:2, :])
        h_ref[...] = h.astype(BF16)

    def result(acc):
        y = x_tile() + (0.5 * mod_ref[2:3, :]) * acc
        return _rms(y, fg_ref[...]) if final_norm else y

    @pl.when(s < FFN_CHUNKS)
    def _():
        o_ref = o_refs[0]
        wa = wa_ref[...].astype(BF16)
        wb = wb_ref[...].astype(BF16)
        wo = wo_ref[...].astype(BF16)
        wa_s[s] = wa
        wb_s[s] = wb
        wo_s[pl.ds(pl.multiple_of(s * fc, fc), fc), :] = wo

        @pl.when(s == 0)
        def _():
            hidden()
            o_ref[...] = jnp.zeros_like(o_ref)

        h = h_ref[...]
        u = (_silu(_dot(h, wa)) * _dot(h, wb)).astype(BF16)
        o_ref[...] += _dot(u, wo)

        @pl.when(s == FFN_CHUNKS - 1)
        def _():
            o_ref[...] = result(o_ref[...])

    @pl.when(s >= FFN_CHUNKS)
    def _():
        hidden()
        for j in range(FFN_CHUNKS):
            h = h_ref[...]
            u = _silu(_dot(h, wa_s[j])) * _dot(h, wb_s[j])
            u_ref[:, j * fc:(j + 1) * fc] = u.astype(BF16)
        y = result(_dot(u_ref[...], wo_s[...]))
        if n_o == 1:
            o_refs[0][...] = y
        else:
            @pl.when(is_ctx)
            def _():
                o_refs[0][...] = y

            @pl.when(jnp.logical_not(is_ctx))
            def _():
                o_refs[1][...] = y


def _mod_spec(layer, k, group):
    return pl.BlockSpec((None, None, None, 3, D), lambda i: (layer, group(i), k, 0, 0))


def _gain_spec(layer, k):
    return pl.BlockSpec((None, None, 1, D), lambda i: (layer, k, 0, 0))


def _ffn(x, mod, gains, w_in, w_out, final_g, *, layer, k, half, n_ctx_tiles, final_norm, split_out):
    tm, fc, nc = FFN_TM, FFN_FC, FFN_CHUNKS
    xs = x if isinstance(x, tuple) else (x,)
    n = sum(a.shape[0] for a in xs)
    n_ctx = n_ctx_tiles * tm
    tile = lambda s: jnp.maximum(s - (nc - 1), 0)
    chunk = lambda s: jnp.minimum(s, nc - 1)
    one_spec = [pl.BlockSpec((tm, D), lambda s: (tile(s), 0))]
    two_specs = [pl.BlockSpec((tm, D), lambda s: (jnp.minimum(tile(s), n_ctx_tiles - 1), 0)),
                 pl.BlockSpec((tm, D), lambda s: (jnp.maximum(tile(s) - n_ctx_tiles, 0), 0))]
    two_shapes = [jax.ShapeDtypeStruct((n_ctx, D), F32), jax.ShapeDtypeStruct((n - n_ctx, D), F32)]
    n_o = 2 if split_out else 1
    kern = functools.partial(_ffn_kernel, final_norm=final_norm, n_x=len(xs), n_o=n_o,
                             n_ctx_tiles=n_ctx_tiles)
    out = pl.pallas_call(
        kern,
        grid=(nc - 1 + n // tm,),
        in_specs=[
            *(two_specs if len(xs) == 2 else one_spec),
            _mod_spec(layer, k, lambda s: jnp.maximum(tile(s) - (n_ctx_tiles - 1), 0)),
            _gain_spec(layer, k),
            pl.BlockSpec((None, None, D, fc), lambda s: (layer, half, 0, chunk(s))),
            pl.BlockSpec((None, None, D, fc), lambda s: (layer, half, 0, nc + chunk(s))),
            pl.BlockSpec((None, None, fc, D), lambda s: (layer, half, chunk(s), 0)),
            pl.BlockSpec((1, D), lambda s: (0, 0)),
        ],
        out_specs=two_specs if split_out else one_spec,
        out_shape=two_shapes if split_out else [jax.ShapeDtypeStruct((n, D), F32)],
        scratch_shapes=[
            pltpu.VMEM((nc, D, fc), BF16),
            pltpu.VMEM((nc, D, fc), BF16),
            pltpu.VMEM((D_FF, D), BF16),
            pltpu.VMEM((tm, D), BF16),
            pltpu.VMEM((tm, D_FF), BF16),
        ],
        compiler_params=_params(("arbitrary",), 60),
        name="ffn",
    )(*xs, mod, gains, w_in, w_in, w_out, final_g)
    return tuple(out) if split_out else out[0]


def _proj_kernel(x_ref, mod_ref, g_ref, w_ref, *o_refs):
    h = _prenorm(x_ref[...], g_ref[...], mod_ref[0:1, :], mod_ref[1:2, :]).astype(BF16)
    for k, o_ref in enumerate(o_refs):
        o_ref[...] = _dot(h, w_ref[:, k * D:(k + 1) * D])


def _prenorm_proj(x, mod, gains, w_bf16, *, layer, group0, rows_per_group, tm, row0=0, n_rows=None):
    n = x.shape[0] if n_rows is None else n_rows
    n_out = w_bf16.shape[1] // D
    tiles_per_group = rows_per_group // tm
    tile0 = row0 // tm
    return pl.pallas_call(
        _proj_kernel,
        grid=(n // tm,),
        in_specs=[
            pl.BlockSpec((tm, D), lambda i: (tile0 + i, 0)),
            _mod_spec(layer, 1, lambda i: group0 + i // tiles_per_group),
            _gain_spec(layer, 1),
            pl.BlockSpec((D, n_out * D), lambda i: (0, 0)),
        ],
        out_specs=[pl.BlockSpec((tm, D), lambda i: (i, 0)) for _ in range(n_out)],
        out_shape=[jax.ShapeDtypeStruct((n, D), F32) for _ in range(n_out)],
        compiler_params=_params(("parallel",), 56),
        name="prenorm_proj",
    )(x, mod, gains, w_bf16)


def _outproj_kernel(yc_ref, yl_ref, w_ref, *rest, n_ctx_tiles):
    *x_refs, mod_ref, o_ref = rest
    is_ctx = pl.program_id(0) < n_ctx_tiles
    y = jnp.where(is_ctx, yc_ref[...], yl_ref[...])
    if len(x_refs) == 2:
        x = jnp.where(is_ctx, x_refs[0][...], x_refs[1][...])
    else:
        x = x_refs[0][...]
    o_ref[...] = x + mod_ref[2:3, :] * _dot(y, w_ref[...])


def _outproj_residual(y_ctx, y_lat, w_bf16, x, mod, *, layer, lat_len):
    tm = PROJ_TM
    n_ctx_tiles = y_ctx.shape[0] // tm
    n_lat_tiles = y_lat.shape[0] // tm
    lat_tiles_per_seq = lat_len // tm
    ctx_tile = lambda i: jnp.minimum(i, n_ctx_tiles - 1)
    lat_tile = lambda i: jnp.maximum(i - n_ctx_tiles, 0)
    group = lambda i: jnp.where(i < n_ctx_tiles, 0, 1 + lat_tile(i) // lat_tiles_per_seq)
    if isinstance(x, tuple):
        x_args = x
        x_specs = [pl.BlockSpec((tm, D), lambda i: (ctx_tile(i), 0)),
                   pl.BlockSpec((tm, D), lambda i: (lat_tile(i), 0))]
    else:
        x_args = (x,)
        x_specs = [pl.BlockSpec((tm, D), lambda i: (i, 0))]
    n_tiles = n_ctx_tiles + n_lat_tiles
    return pl.pallas_call(
        functools.partial(_outproj_kernel, n_ctx_tiles=n_ctx_tiles),
        grid=(n_tiles,),
        in_specs=[
            pl.BlockSpec((tm, D), lambda i: (ctx_tile(i), 0)),
            pl.BlockSpec((tm, D), lambda i: (lat_tile(i), 0)),
            pl.BlockSpec((D, D), lambda i: (0, 0)),
            *x_specs,
            _mod_spec(layer, 1, group),
        ],
        out_specs=pl.BlockSpec((tm, D), lambda i: (i, 0)),
        out_shape=jax.ShapeDtypeStruct((n_tiles * tm, D), F32),
        compiler_params=_params(("parallel",), 48),
        name="outproj_residual",
    )(y_ctx, y_lat, w_bf16, *x_args, mod)


def _rglru_kernel(gate_ref, xr_ref, cw_ref, cb_ref, wg_ref, bg_ref, lam_ref, h0_ref,
                  y_ref, fin_ref, pad_ref, xc_ref, a0_ref, b0_ref, a1_ref, b1_ref, *, seq_len, n_seq):
    tm = seq_len * n_seq
    pad_ref[0:8, :] = jnp.zeros((8, D), F32)
    pad_ref[8 + tm:16 + tm, :] = jnp.zeros((8, D), F32)
    pad_ref[8:8 + tm, :] = xr_ref[...]
    t_loc = lax.broadcasted_iota(jnp.int32, (tm, 1), 0) % seq_len
    xm2 = jnp.where(t_loc >= 2, pad_ref[6:6 + tm, :], 0.0)
    xm1 = jnp.where(t_loc >= 1, pad_ref[7:7 + tm, :], 0.0)
    xp1 = jnp.where(t_loc <= seq_len - 2, pad_ref[9:9 + tm, :], 0.0)
    xc = xm2 * cw_ref[0:1, :] + xm1 * cw_ref[1:2, :]
    xc = xc + xr_ref[...] * cw_ref[2:3, :]
    xc = xc + xp1 * cw_ref[3:4, :]
    xc_ref[...] = xc + cb_ref[...]

    if n_seq == 1:
        rows = lambda t: pl.ds(t, 1)
    else:
        rows = lambda t: pl.ds(t, n_seq, stride=seq_len)

    n_lg = D // LANES
    ab_refs = ((a0_ref, b0_ref), (a1_ref, b1_ref))
    for d, (a_ref, b_ref) in enumerate(ab_refs):
        half_decay = _lru_half_decay(lam_ref[d:d + 1, :])
        for q in range(N_GATE_GROUPS):
            cols = slice(q * GATE_GROUP, (q + 1) * GATE_GROUP)
            xq = xc_ref[:, cols]
            half_pre = _dot(xq.astype(BF16), wg_ref[d, q]) + bg_ref[d, q]
            a, b = _lru_coeffs(half_pre, 0.5 * xq, half_decay[:, cols])
            for c in range(GATE_GROUP // LANES):
                lg = q * (GATE_GROUP // LANES) + c
                a_ref[lg] = a[:, c * LANES:(c + 1) * LANES]
                b_ref[lg] = b[:, c * LANES:(c + 1) * LANES]

    def step(k, carry):
        nxt = []
        for (a_ref, b_ref), hs, t in zip(ab_refs, carry, (k, seq_len - 1 - k)):
            idx = rows(t)
            out = []
            for lg in range(n_lg):
                h = a_ref[lg, idx, :] * hs[lg] + b_ref[lg, idx, :]
                b_ref[lg, idx, :] = h
                out.append(h)
            nxt.append(tuple(out))
        return tuple(nxt)

    h_init = tuple(tuple(h0_ref[d][:, lg * LANES:(lg + 1) * LANES] for lg in range(n_lg)) for d in range(2))
    h_fin = lax.fori_loop(0, seq_len, step, h_init, unroll=8)
    for d in range(2):
        for lg in range(n_lg):
            fin_ref[d, :, lg * LANES:(lg + 1) * LANES] = h_fin[d][lg]

    for lg in range(n_lg):
        cols = slice(lg * LANES, (lg + 1) * LANES)
        y_ref[:, cols] = ((b0_ref[lg] + b1_ref[lg]) * _gelu_tanh(gate_ref[:, cols])).astype(BF16)


def _rglru_core(gate, xr, conv_w, conv_b, wg, bg, lam, h0, *, seq_len, n_seq):
    n = gate.shape[0]
    tm = seq_len * n_seq
    n_tiles = n // tm
    kern = functools.partial(_rglru_kernel, seq_len=seq_len, n_seq=n_seq)
    return pl.pallas_call(
        kern,
        grid=(n_tiles,),
        in_specs=[
            pl.BlockSpec((tm, D), lambda i: (i, 0)),
            pl.BlockSpec((tm, D), lambda i: (i, 0)),
            pl.BlockSpec((4, D), lambda i: (0, 0)),
            pl.BlockSpec((1, D), lambda i: (0, 0)),
            pl.BlockSpec((2, N_GATE_GROUPS, GATE_GROUP, 2 * GATE_GROUP), lambda i: (0, 0, 0, 0)),
            pl.BlockSpec((2, N_GATE_GROUPS, 1, 2 * GATE_GROUP), lambda i: (0, 0, 0, 0)),
            pl.BlockSpec((2, D), lambda i: (0, 0)),
            pl.BlockSpec((2, None, n_seq, D), lambda i: (0, i, 0, 0)),
        ],
        out_specs=[
            pl.BlockSpec((tm, D), lambda i: (i, 0)),
            pl.BlockSpec((2, None, n_seq, D), lambda i: (0, i, 0, 0)),
        ],
        out_shape=[
            jax.ShapeDtypeStruct((n, D), BF16),
            jax.ShapeDtypeStruct((2, n_tiles, n_seq, D), F32),
        ],
        scratch_shapes=[
            pltpu.VMEM((tm + 16, D), F32),
            pltpu.VMEM((tm, D), F32),
        ] + [pltpu.VMEM((D // LANES, tm, LANES), F32) for _ in range(4)],
        compiler_params=_params(("parallel",), 56),
        name="rglru_core",
    )(gate, xr, conv_w, conv_b, wg, bg, lam, h0)


def _lru_coeffs(half_pre, half_x, half_decay):
    tr = jnp.tanh(half_pre[:, :GATE_GROUP])
    ti = jnp.tanh(half_pre[:, GATE_GROUP:])
    log_a = half_decay * tr + half_decay
    a = jnp.exp(log_a)
    v = -jnp.tanh(log_a) * (1.0 + a * a)
    m = jnp.where(v > 0.0, v * lax.rsqrt(v), 0.0)
    return a, (m * half_x) * (1.0 + ti)


def _lru_half_decay(lam_row):
    z = -lam_row
    return (-0.5 * LRU_C) * (jnp.maximum(z, 0.0) + jnp.log1p(jnp.exp(-jnp.abs(z))))


def _rglru_tm_kernel(gate_ref, xr_ref, cw_ref, cb_ref, wg_ref, bg_ref, lam_ref, y_ref, fin_ref,
                     pad_ref, xc_ref, a0_ref, b0_ref, a1_ref, b1_ref, *, seq_len):
    S = TM_GROUP
    R = seq_len * S
    lead = 2 * S
    n_lg = GATE_GROUP // LANES
    for lg in range(n_lg):
        cols = slice(lg * LANES, (lg + 1) * LANES)
        pad_ref[lg, 0:lead, :] = jnp.zeros((lead, LANES), F32)
        pad_ref[lg, lead + R:lead + R + S, :] = jnp.zeros((S, LANES), F32)
        for s in range(S):
            pad_ref[lg, pl.ds(lead + s, seq_len, stride=S), :] = xr_ref[s * seq_len:(s + 1) * seq_len, cols]
        xc = pad_ref[lg, 0:R, :] * cw_ref[0:1, cols] + pad_ref[lg, S:S + R, :] * cw_ref[1:2, cols]
        xc = xc + pad_ref[lg, lead:lead + R, :] * cw_ref[2:3, cols]
        xc = xc + pad_ref[lg, lead + S:lead + S + R, :] * cw_ref[3:4, cols]
        xc_ref[:, cols] = xc + cb_ref[:, cols]
    xcb = xc_ref[...].astype(BF16)
    for d, (a_ref, b_ref) in enumerate(((a0_ref, b0_ref), (a1_ref, b1_ref))):
        half_pre = _dot(xcb, wg_ref[d]) + bg_ref[d]
        a, b = _lru_coeffs(half_pre, 0.5 * xc_ref[...], _lru_half_decay(lam_ref[d:d + 1, :]))
        a_ref[...] = a
        b_ref[...] = b

    def step(k, carry):
        hf, hb = carry
        rf = pl.ds(pl.multiple_of(k * S, S), S)
        rb = pl.ds(pl.multiple_of((seq_len - 1 - k) * S, S), S)
        hf = a0_ref[rf, :] * hf + b0_ref[rf, :]
        hb = a1_ref[rb, :] * hb + b1_ref[rb, :]
        b0_ref[rf, :] = hf
        b1_ref[rb, :] = hb
        return hf, hb

    zero = jnp.zeros((S, GATE_GROUP), F32)
    hf, hb = lax.fori_loop(0, seq_len, step, (zero, zero), unroll=8)
    fin_ref[0] = hf
    fin_ref[1] = hb
    for lg in range(n_lg):
        cols = slice(lg * LANES, (lg + 1) * LANES)
        pad_ref[lg, 0:R, :] = b0_ref[:, cols] + b1_ref[:, cols]
        for s in range(S):
            rows = slice(s * seq_len, (s + 1) * seq_len)
            ysum = pad_ref[lg, pl.ds(s, seq_len, stride=S), :]
            y_ref[rows, cols] = (ysum * _gelu_tanh(gate_ref[rows, cols])).astype(BF16)


def _rglru_core_tm(gate, xr, conv_w, conv_b, wg, bg, lam, *, seq_len, row_block0=0, n_rows=None):
    n = gate.shape[0] if n_rows is None else n_rows
    rows = seq_len * TM_GROUP
    n_groups = n // rows
    kern = functools.partial(_rglru_tm_kernel, seq_len=seq_len)
    slab_in = pl.BlockSpec((rows, GATE_GROUP), lambda gi, q: (row_block0 + gi, q))
    return pl.pallas_call(
        kern,
        grid=(n_groups, N_GATE_GROUPS),
        in_specs=[
            slab_in,
            slab_in,
            pl.BlockSpec((4, GATE_GROUP), lambda gi, q: (0, q)),
            pl.BlockSpec((1, GATE_GROUP), lambda gi, q: (0, q)),
            pl.BlockSpec((2, None, GATE_GROUP, 2 * GATE_GROUP), lambda gi, q: (0, q, 0, 0)),
            pl.BlockSpec((2, None, 1, 2 * GATE_GROUP), lambda gi, q: (0, q, 0, 0)),
            pl.BlockSpec((2, GATE_GROUP), lambda gi, q: (0, q)),
        ],
        out_specs=[
            pl.BlockSpec((rows, GATE_GROUP), lambda gi, q: (gi, q)),
            pl.BlockSpec((2, None, TM_GROUP, GATE_GROUP), lambda gi, q: (0, gi, 0, q)),
        ],
        out_shape=[
            jax.ShapeDtypeStruct((n, D), BF16),
            jax.ShapeDtypeStruct((2, n_groups, TM_GROUP, D), F32),
        ],
        scratch_shapes=[pltpu.VMEM((GATE_GROUP // LANES, rows + 3 * TM_GROUP, LANES), F32)]
        + [pltpu.VMEM((rows, GATE_GROUP), F32) for _ in range(5)],
        compiler_params=_params(("parallel", "parallel"), 40),
        name="rglru_core_tm",
    )(gate, xr, conv_w, conv_b, wg, bg, lam)


def _gate_weights(gw_a, gb_a, gw_x, gb_x):
    per = GATE_GROUP // LRU_BLOCK
    blk = np.arange(GATE_GROUP) // LRU_BLOCK
    on_diagonal = jnp.asarray(blk[:, None] == blk[None, :])

    def dense(w):
        rows = w.reshape(2, N_GATE_GROUPS, GATE_GROUP, LRU_BLOCK)
        return jnp.where(on_diagonal, jnp.concatenate([rows] * per, axis=-1), 0.0)

    wg = (0.5 * jnp.concatenate([dense(gw_a), dense(gw_x)], axis=-1)).astype(BF16)
    ba = gb_a.reshape(2, N_GATE_GROUPS, 1, GATE_GROUP)
    bx = gb_x.reshape(2, N_GATE_GROUPS, 1, GATE_GROUP)
    return wg, 0.5 * jnp.concatenate([ba, bx], axis=-1)


def _lam_value(lq_ref, lk_ref, lam_init):
    s = jnp.sum(lq_ref[...] * lk_ref[...], axis=-1, keepdims=True)
    e = jnp.exp(s)
    return e[0:1, :] - e[1:2, :] + lam_init


def _softmax_av(s, v_ones):
    e = jnp.exp(s - jnp.max(s, axis=-1, keepdims=True)).astype(BF16)
    ov = _dot(e, v_ones)
    return ov[:, :HEAD_W] * (1.0 / ov[:, HEAD_W:])


def _diff_head(qh, kh_bf16, vh_bf16, lam, g_row, out_scale):
    lane = lax.broadcasted_iota(jnp.int32, qh.shape, 1)
    qs = qh * (MAP_W ** -0.5)
    q0 = jnp.where(lane < MAP_W, qs, 0.0).astype(BF16)
    q1 = jnp.where(lane >= MAP_W, qs, 0.0).astype(BF16)
    v_ones = jnp.concatenate([vh_bf16, jnp.ones_like(vh_bf16)], axis=1)
    o = _softmax_av(_dot_nt(q0, kh_bf16), v_ones) - lam * _softmax_av(_dot_nt(q1, kh_bf16), v_ones)
    return _rms(o, g_row) * out_scale


def _attn_ctx_kernel(q_ref, k_ref, v_ref, lq_ref, lk_ref, g_ref, o_ref, *, lam_init):
    lam = _lam_value(lq_ref, lk_ref, lam_init)
    for hd in range(N_HEADS):
        cols = slice(hd * HEAD_W, (hd + 1) * HEAD_W)
        o = _diff_head(q_ref[:, cols], k_ref[:, cols].astype(BF16), v_ref[:, cols].astype(BF16),
                       lam, g_ref[...], 1.0 - lam_init)
        o_ref[:, cols] = o.astype(BF16)


def _attn_ctx(q, k, v, lq, lk, g, *, seq_len, lam_init):
    n = q.shape[0]
    kern = functools.partial(_attn_ctx_kernel, lam_init=lam_init)
    tok = pl.BlockSpec((seq_len, D), lambda i: (i, 0))
    small = lambda shape: pl.BlockSpec(shape, lambda i: (0, 0))
    return pl.pallas_call(
        kern,
        grid=(n // seq_len,),
        in_specs=[tok, tok, tok, small((2, MAP_W)), small((2, MAP_W)), small((1, HEAD_W))],
        out_specs=tok,
        out_shape=jax.ShapeDtypeStruct((n, D), BF16),
        compiler_params=_params(("parallel",), 32),
        name="attn_context",
    )(q, k, v, lq, lk, g)


def _rope(x, cos, sin_signed):
    lane = lax.broadcasted_iota(jnp.int32, x.shape, 1)
    partner = jnp.where(lane % 32 < 16, pltpu.roll(x, HEAD_W - 16, 1), pltpu.roll(x, 16, 1))
    return x * cos + partner * sin_signed


def _attn_lat_kernel(q_ref, k_ref, v_ref, ck_ref, cv_ref, cos_ref, sin_ref, lq_ref, lk_ref, g_ref,
                     o_ref, kall_ref, vall_ref, *, lam_init, past_len, seq_len, tq):
    qb = pl.program_id(1)

    @pl.when(qb == 0)
    def _():
        kall_ref[0:past_len, :] = ck_ref[...].astype(BF16)
        vall_ref[0:past_len, :] = cv_ref[...].astype(BF16)
        vall_ref[past_len:past_len + seq_len, :] = v_ref[...].astype(BF16)
        for hd in range(N_HEADS):
            cols = slice(hd * HEAD_W, (hd + 1) * HEAD_W)
            kr = _rope(k_ref[:, cols], cos_ref[...], sin_ref[...])
            kall_ref[past_len:past_len + seq_len, cols] = kr.astype(BF16)

    lam = _lam_value(lq_ref, lk_ref, lam_init)
    row0 = pl.multiple_of(qb * tq, tq)
    cos_q = cos_ref[pl.ds(row0, tq), :]
    sin_q = sin_ref[pl.ds(row0, tq), :]
    for hd in range(N_HEADS):
        cols = slice(hd * HEAD_W, (hd + 1) * HEAD_W)
        qh = _rope(q_ref[:, cols], cos_q, sin_q)
        o = _diff_head(qh, kall_ref[:, cols], vall_ref[:, cols], lam, g_ref[...], 1.0 - lam_init)
        o_ref[:, cols] = o.astype(BF16)


def _attn_lat(q, k, v, cache_k, cache_v, cos, sin_signed, lq, lk, g, *, seq_len, lam_init):
    n = q.shape[0]
    n_batch = n // seq_len
    past_len = cache_k.shape[1]
    tq = 256
    n_qb = seq_len // tq
    kern = functools.partial(_attn_lat_kernel, lam_init=lam_init, past_len=past_len,
                             seq_len=seq_len, tq=tq)
    seq = pl.BlockSpec((seq_len, D), lambda b, j: (b, 0))
    cache = pl.BlockSpec((None, past_len, D), lambda b, j: (b, 0, 0))
    small = lambda shape: pl.BlockSpec(shape, lambda b, j: (0, 0))
    qblk = pl.BlockSpec((tq, D), lambda b, j: (b * n_qb + j, 0))
    return pl.pallas_call(
        kern,
        grid=(n_batch, n_qb),
        in_specs=[qblk, seq, seq, cache, cache, small((seq_len, HEAD_W)), small((seq_len, HEAD_W)),
                  small((2, MAP_W)), small((2, MAP_W)), small((1, HEAD_W))],
        out_specs=qblk,
        out_shape=jax.ShapeDtypeStruct((n, D), BF16),
        scratch_shapes=[pltpu.VMEM((past_len + seq_len, D), BF16),
                        pltpu.VMEM((past_len + seq_len, D), BF16)],
        compiler_params=_params(("parallel", "arbitrary"), 48),
        name="attn_latent",
    )(q, k, v, cache_k, cache_v, cos, sin_signed, lq, lk, g)


def _rope_tables(seq_len):
    t = np.arange(seq_len)
    pos = np.stack([t // GRID_W, t % GRID_W], axis=-1).astype(np.float32)
    nf = MAP_W // 4
    inv = (np.float32(ROPE_THETA) ** (-np.arange(nf, dtype=np.float32) / nf)).astype(np.float32)
    lane = np.arange(HEAD_W)
    axis = (lane % MAP_W) // 32
    freq = lane % nf
    sign = np.where(lane % 32 < nf, -1.0, 1.0).astype(np.float32)
    pos_l = jnp.asarray(pos)[:, axis]
    ang = pos_l * jnp.asarray(inv)[freq][None, :]
    return jnp.cos(ang), jnp.sin(ang) * jnp.asarray(sign)[None, :]


def _pool_sequence(x_ref, mod_ref, g_ref, w_ref, sc_ref, o_ref, h_ref, lvl_ref, T):
    R = T + 2 * POOL_PAD
    h = _prenorm(x_ref[...], g_ref[...], mod_ref[0:1, :], mod_ref[1:2, :])
    h_ref[...] = h
    lvl_ref[0:POOL_PAD, :] = jnp.zeros((POOL_PAD, D), F32)
    lvl_ref[POOL_PAD + T:R, :] = jnp.zeros((POOL_PAD, D), F32)
    lvl_ref[POOL_PAD:POOL_PAD + T, :] = h
    t = lax.broadcasted_iota(jnp.int32, (T, 1), 0)
    for gi, win in enumerate(POOL_WINDOWS):
        cols = slice(gi * POOL_GROUP, (gi + 1) * POOL_GROUP)
        half = win // 2
        s = 1
        while s < win:
            lvl_ref[POOL_PAD:R, cols] = lvl_ref[POOL_PAD - s:R - s, cols] + lvl_ref[POOL_PAD:R, cols]
            s *= 2
        wsum = lvl_ref[POOL_PAD + half - 1:POOL_PAD + half - 1 + T, cols]
        cnt = (jnp.minimum(t + half, T) - jnp.maximum(t - half, 0)).astype(F32)
        diff = wsum * (1.0 / cnt) - h_ref[:, cols]
        m = _dot(diff.astype(BF16), w_ref[gi]) * sc_ref[:, cols]
        o_ref[:, cols] = x_ref[:, cols] + mod_ref[2:3, cols] * m


def _pool_kernel(x_ref, mod_ref, g_ref, w_ref, sc_ref, o_ref, h_ref, lvl_ref, *, n_ctx_tiles, ctx_len):
    tile_rows = x_ref.shape[0]
    is_ctx = pl.program_id(0) < n_ctx_tiles

    @pl.when(is_ctx)
    def _():
        for k in range(tile_rows // ctx_len):
            rows = slice(k * ctx_len, (k + 1) * ctx_len)
            _pool_sequence(x_ref.at[rows], mod_ref, g_ref, w_ref, sc_ref, o_ref.at[rows], h_ref.at[rows],
                           lvl_ref.at[0:ctx_len + 2 * POOL_PAD], ctx_len)

    @pl.when(jnp.logical_not(is_ctx))
    def _():
        _pool_sequence(x_ref, mod_ref, g_ref, w_ref, sc_ref, o_ref, h_ref, lvl_ref, tile_rows)


def _pool_mixer(x, mod, gains, w_bf16, scale, *, layer, n_ctx, ctx_len, lat_len):
    n = x.shape[0]
    tm = lat_len
    n_ctx_tiles = n_ctx // tm
    kern = functools.partial(_pool_kernel, n_ctx_tiles=n_ctx_tiles, ctx_len=ctx_len)
    tok = pl.BlockSpec((tm, D), lambda i: (i, 0))
    return pl.pallas_call(
        kern,
        grid=(n // tm,),
        in_specs=[
            tok,
            _mod_spec(layer, 1, lambda i: jnp.maximum(i - (n_ctx_tiles - 1), 0)),
            _gain_spec(layer, 1),
            pl.BlockSpec((len(POOL_WINDOWS), POOL_GROUP, POOL_GROUP), lambda i: (0, 0, 0)),
            pl.BlockSpec((1, D), lambda i: (0, 0)),
        ],
        out_specs=tok,
        out_shape=jax.ShapeDtypeStruct((n, D), F32),
        scratch_shapes=[pltpu.VMEM((tm, D), F32),
                        pltpu.VMEM((tm + 2 * POOL_PAD, D), F32)],
        compiler_params=_params(("parallel",), 40),
        name="pool_mixer",
    )(x, mod, gains, w_bf16, scale)


def kernel(x_prompt, x_sample, state_rglru, cache_k_diff, cache_v_diff, c, c_ctx, norm_g, w_mod, b_mod, w_ffn_in, w_ffn_out, a_w_in, a_conv_w, a_conv_b, a_gate_w_a, a_gate_b_a, a_gate_w_x, a_gate_b_x, a_lambda, a_w_out, b_w_qkv, b_lam_q, b_lam_k, b_subln_g, b_w_o, c_w_pool, c_scale, final_norm_g):
    n_ctx_seq, ctx_len, _ = x_prompt.shape
    n_lat_seq, lat_len, _ = x_sample.shape
    n_ctx, n_lat = n_ctx_seq * ctx_len, n_lat_seq * lat_len
    xp = x_prompt.reshape(n_ctx, D)
    xs = x_sample.reshape(n_lat, D)

    cond8 = jnp.concatenate([c_ctx[None], c, jnp.zeros((8 - 1 - n_lat_seq, D), F32)], axis=0)
    mod = _adaln_all(cond8, w_mod, b_mod).reshape(DEPTH, 8, N_MOD // 3, 3, D)
    gains = norm_g.reshape(DEPTH, 3, 1, D)
    final_g = final_norm_g.reshape(1, D)

    assert n_ctx_seq % TM_GROUP == 0 and lat_len == FFN_TM and n_ctx % FFN_TM == 0
    n_ctx_tiles = n_ctx // FFN_TM
    new_states, new_k, new_v = [], [], []
    x = (xp, xs)
    for l in range(DEPTH):
        kind, j = l % 3, l // 3

        def ffn(x, k, half, last):
            return _ffn(x, mod, gains, w_ffn_in, w_ffn_out, final_g, layer=l, k=k, half=half,
                        n_ctx_tiles=n_ctx_tiles, final_norm=last, split_out=last)

        x = ffn(x, 0, 0, False)

        ctx_proj = dict(mod=mod, gains=gains, layer=l, group0=0, rows_per_group=n_ctx, tm=PROJ_TM)
        lat_proj = dict(mod=mod, gains=gains, layer=l, group0=1, rows_per_group=lat_len, tm=PROJ_TM // 2)
        if isinstance(x, tuple):
            ctx_proj.update(x=x[0])
            lat_proj.update(x=x[1])
        else:
            ctx_proj.update(x=x, row0=0, n_rows=n_ctx)
            lat_proj.update(x=x, row0=n_ctx, n_rows=n_lat)
        if kind == 0:
            w_in = a_w_in[j].astype(BF16)
            w_out = a_w_out[j].astype(BF16)
            wg, bg = _gate_weights(a_gate_w_a[j], a_gate_b_a[j], a_gate_w_x[j], a_gate_b_x[j])
            conv_b = a_conv_b[j].reshape(1, D)
            gate_p, xr_p = _prenorm_proj(w_bf16=w_in, **ctx_proj)
            y_p, fin = _rglru_core_tm(gate_p, xr_p, a_conv_w[j], conv_b, wg, bg, a_lambda[j],
                                      seq_len=ctx_len)
            new_states.append(jnp.transpose(fin.reshape(2, n_ctx_seq, D), (1, 0, 2)))
            gate_s, xr_s = _prenorm_proj(w_bf16=w_in, **lat_proj)
            h0 = jnp.transpose(state_rglru[:, j].astype(F32), (1, 0, 2)).reshape(2, n_lat_seq, 1, D)
            y_s, _ = _rglru_core(gate_s, xr_s, a_conv_w[j], conv_b, wg, bg, a_lambda[j], h0,
                                 seq_len=lat_len, n_seq=1)
            x = _outproj_residual(y_p, y_s, w_out, x, mod, layer=l, lat_len=lat_len)
        elif kind == 1:
            lam_init = 0.8 - 0.6 * math.exp(-0.3 * l)
            w_qkv = b_w_qkv[j].astype(BF16)
            w_o = b_w_o[j].astype(BF16)
            sub_g = b_subln_g[j].reshape(1, HEAD_W)
            q_p, k_p, v_p = _prenorm_proj(w_bf16=w_qkv, **ctx_proj)
            o_p = _attn_ctx(q_p, k_p, v_p, b_lam_q[j], b_lam_k[j], sub_g, seq_len=ctx_len,
                            lam_init=lam_init)
            new_k.append(k_p.reshape(n_ctx_seq, ctx_len, N_HEADS, HEAD_W))
            new_v.append(v_p.reshape(n_ctx_seq, ctx_len, N_HEADS, HEAD_W))
            q_s, k_s, v_s = _prenorm_proj(w_bf16=w_qkv, **lat_proj)
            cos, sin_signed = _rope_tables(lat_len)
            past_len = cache_k_diff.shape[2]
            o_s = _attn_lat(q_s, k_s, v_s,
                            cache_k_diff[:, j].reshape(n_lat_seq, past_len, D),
                            cache_v_diff[:, j].reshape(n_lat_seq, past_len, D),
                            cos, sin_signed, b_lam_q[j], b_lam_k[j], sub_g,
                            seq_len=lat_len, lam_init=lam_init)
            x = _outproj_residual(o_p, o_s, w_o, x, mod, layer=l, lat_len=lat_len)
        else:
            assert not isinstance(x, tuple)
            x = _pool_mixer(x, mod, gains, c_w_pool[j].astype(BF16), c_scale[j].reshape(1, D),
                            layer=l, n_ctx=n_ctx, ctx_len=ctx_len, lat_len=lat_len)

        x = ffn(x, 2, 1, l == DEPTH - 1)

    xp, xs = x
    y_prompt = xp.reshape(x_prompt.shape)
    y_sample = xs.reshape(x_sample.shape)
    new_state = jnp.stack(new_states, axis=1).astype(x_prompt.dtype)
    new_cache_k = jnp.stack(new_k, axis=1).astype(x_prompt.dtype)
    new_cache_v = jnp.stack(new_v, axis=1).astype(x_prompt.dtype)
    return (y_prompt, y_sample, new_state, new_cache_k, new_cache_v)
```

```python
import functools
import math

import jax
import jax.numpy as jnp
import numpy as np
from jax import lax
from jax.experimental import pallas as pl
from jax.experimental.pallas import tpu as pltpu

F32 = jnp.float32
BF16 = jnp.bfloat16

D = 1024
DEPTH = 4
N_MOD = 9
EPS = 1e-6
D_FF = 2816
LRU_BLOCK = 64
N_LRU_BLOCKS = 16
LRU_C = 8.0
N_HEADS = 8
LANES = 128
HEAD_W = 128
MAP_W = 64
assert MAP_W ** -0.5 == 0.125
GRID_W = 64
ROPE_THETA = 10000.0
POOL_WINDOWS = (2, 4, 8, 16)
POOL_GROUP = 256
POOL_PAD = 16

GATE_GROUP = 256
N_GATE_GROUPS = D // GATE_GROUP

MIB = 1024 * 1024


def _params(sem, vmem_mib):
    return pltpu.CompilerParams(dimension_semantics=sem, vmem_limit_bytes=vmem_mib * MIB)


def _dot(a, b):
    return jnp.dot(a, b, preferred_element_type=F32)


def _dot_nt(a, b):
    return lax.dot_general(a, b, (((1,), (1,)), ((), ())), preferred_element_type=F32)


def _sigmoid(x):
    return 0.5 + 0.5 * jnp.tanh(0.5 * x)


def _silu(x):
    return x * _sigmoid(x)


def _gelu_tanh(x):
    c = math.sqrt(2.0 / math.pi)
    hx = 0.5 * x
    return hx + hx * jnp.tanh(x * (c + (0.044715 * c) * (x * x)))


def _rms(x, g):
    ms = jnp.mean(x * x, axis=-1, keepdims=True)
    return (x * lax.rsqrt(ms + EPS)) * g


def _prenorm(x, g, shift, scale):
    return _rms(x, g) * (1.0 + scale) + shift


def _mod_kernel(cond_ref, w_ref, b_ref, o_ref):
    s = _silu(cond_ref[...]).astype(BF16)
    o_ref[...] = _dot(s, w_ref[...].astype(BF16)) + b_ref[...]


def _adaln_all(cond8, w_mod, b_mod):
    tn = 1536
    n = N_MOD * D
    return pl.pallas_call(
        _mod_kernel,
        grid=(DEPTH, n // tn),
        in_specs=[
            pl.BlockSpec((8, D), lambda l, j: (0, 0)),
            pl.BlockSpec((None, D, tn), lambda l, j: (l, 0, j)),
            pl.BlockSpec((None, 1, tn), lambda l, j: (l, 0, j)),
        ],
        out_specs=pl.BlockSpec((None, 8, tn), lambda l, j: (l, 0, j)),
        out_shape=jax.ShapeDtypeStruct((DEPTH, 8, n), F32),
        compiler_params=_params(("parallel", "parallel"), 32),
        name="adaln_mod",
    )(cond8, w_mod, b_mod.reshape(DEPTH, 1, n))


FFN_TM = 1024
FFN_FC = 256
FFN_CHUNKS = D_FF // FFN_FC


TM_GROUP = 8
PROJ_TM = 1024


def _ffn_stream_steps(paired):
    return (FFN_CHUNKS + 1) // 2 if paired else FFN_CHUNKS


def _ffn_kernel(*refs, final_norm, n_x, n_o, n_ctx_tiles, paired):
    assert FFN_CHUNKS % 2 == 1 or not paired
    n_w = 6 if paired else 3
    n_stream = _ffn_stream_steps(paired)
    x_refs = refs[:n_x]
    mod_ref, g_ref = refs[n_x:n_x + 2]
    w_refs = refs[n_x + 2:n_x + 2 + n_w]
    fg_ref = refs[n_x + 2 + n_w]
    o_refs = refs[n_x + 3 + n_w:n_x + 3 + n_w + n_o]
    wa_s, wb_s, wo_s, h_ref, u_ref = refs[n_x + 3 + n_w + n_o:]
    s = pl.program_id(0)
    fc = FFN_FC
    is_ctx = s - (n_stream - 1) < n_ctx_tiles

    def x_tile():
        if n_x == 1:
            return x_refs[0][...]
        return jnp.where(is_ctx, x_refs[0][...], x_refs[1][...])

    def hidden():
        h = _prenorm(x_tile(), g_ref[...], mod_ref[0:1, :], mod_ref[1:2, :])
        h_ref[...] = h.astype(BF16)

    def result(acc):
        y = x_tile() + (0.5 * mod_ref[2:3, :]) * acc
        return _rms(y, fg_ref[...]) if final_norm else y

    o_ref = o_refs[0]

    def stage(window, cid):
        wa_ref, wb_ref, wo_ref = w_refs[3 * window:3 * window + 3]
        wa = wa_ref[...].astype(BF16)
        wb = wb_ref[...].astype(BF16)
        wo = wo_ref[...].astype(BF16)
        wa_s[cid] = wa
        wb_s[cid] = wb
        wo_s[pl.ds(pl.multiple_of(cid * fc, fc), fc), :] = wo
        h = h_ref[...]
        u = (_silu(_dot(h, wa)) * _dot(h, wb)).astype(BF16)
        o_ref[...] += _dot(u, wo)

    @pl.when(s == 0)
    def _():
        hidden()
        o_ref[...] = jnp.zeros_like(o_ref)
        stage(0, 0)

    @pl.when(jnp.logical_and(s > 0, s < n_stream))
    def _():
        if paired:
            stage(0, 2 * s - 1)
            stage(1, 2 * s)
        else:
            stage(0, s)

        @pl.when(s == n_stream - 1)
        def _():
            o_ref[...] = result(o_ref[...])

    @pl.when(s >= n_stream)
    def _():
        hidden()
        for j in range(FFN_CHUNKS):
            h = h_ref[...]
            u = _silu(_dot(h, wa_s[j])) * _dot(h, wb_s[j])
            u_ref[:, j * fc:(j + 1) * fc] = u.astype(BF16)
        y = result(_dot(u_ref[...], wo_s[...]))
        if n_o == 1:
            o_refs[0][...] = y
        else:
            @pl.when(is_ctx)
            def _():
                o_refs[0][...] = y

            @pl.when(jnp.logical_not(is_ctx))
            def _():
                o_refs[1][...] = y


def _mod_spec(layer, k, group):
    return pl.BlockSpec((None, None, None, 3, D), lambda i: (layer, group(i), k, 0, 0))


def _gain_spec(layer, k):
    return pl.BlockSpec((None, None, 1, D), lambda i: (layer, k, 0, 0))


def _ffn(x, mod, gains, w_in, w_out, final_g, *, layer, k, half, n_ctx_tiles, final_norm, split_out):
    tm, fc, nc = FFN_TM, FFN_FC, FFN_CHUNKS
    xs = x if isinstance(x, tuple) else (x,)
    n = sum(a.shape[0] for a in xs)
    n_ctx = n_ctx_tiles * tm
    paired = len(xs) == 1 and not split_out
    n_stream = _ffn_stream_steps(paired)
    tile = lambda s: jnp.maximum(s - (n_stream - 1), 0)
    if paired:
        chunks = [lambda s: jnp.clip(2 * s - 1, 0, nc - 1), lambda s: jnp.clip(2 * s, 0, nc - 1)]
    else:
        chunks = [lambda s: jnp.minimum(s, nc - 1)]
    w_specs = []
    for chunk in chunks:
        w_specs += [
            pl.BlockSpec((None, None, D, fc), lambda s, chunk=chunk: (layer, half, 0, chunk(s))),
            pl.BlockSpec((None, None, D, fc), lambda s, chunk=chunk: (layer, half, 0, nc + chunk(s))),
            pl.BlockSpec((None, None, fc, D), lambda s, chunk=chunk: (layer, half, chunk(s), 0)),
        ]
    one_spec = [pl.BlockSpec((tm, D), lambda s: (tile(s), 0))]
    two_specs = [pl.BlockSpec((tm, D), lambda s: (jnp.minimum(tile(s), n_ctx_tiles - 1), 0)),
                 pl.BlockSpec((tm, D), lambda s: (jnp.maximum(tile(s) - n_ctx_tiles, 0), 0))]
    two_shapes = [jax.ShapeDtypeStruct((n_ctx, D), F32), jax.ShapeDtypeStruct((n - n_ctx, D), F32)]
    n_o = 2 if split_out else 1
    kern = functools.partial(_ffn_kernel, final_norm=final_norm, n_x=len(xs), n_o=n_o,
                             n_ctx_tiles=n_ctx_tiles, paired=paired)
    out = pl.pallas_call(
        kern,
        grid=(n_stream - 1 + n // tm,),
        in_specs=[
            *(two_specs if len(xs) == 2 else one_spec),
            _mod_spec(layer, k, lambda s: jnp.maximum(tile(s) - (n_ctx_tiles - 1), 0)),
            _gain_spec(layer, k),
            *w_specs,
            pl.BlockSpec((1, D), lambda s: (0, 0)),
        ],
        out_specs=two_specs if split_out else one_spec,
        out_shape=two_shapes if split_out else [jax.ShapeDtypeStruct((n, D), F32)],
        scratch_shapes=[
            pltpu.VMEM((nc, D, fc), BF16),
            pltpu.VMEM((nc, D, fc), BF16),
            pltpu.VMEM((D_FF, D), BF16),
            pltpu.VMEM((tm, D), BF16),
            pltpu.VMEM((tm, D_FF), BF16),
        ],
        compiler_params=_params(("arbitrary",), 60),
        name="ffn",
    )(*xs, mod, gains, *([w_in, w_in, w_out] * len(chunks)), final_g)
    return tuple(out) if split_out else out[0]


def _proj_kernel(x_ref, mod_ref, g_ref, w_ref, *o_refs):
    h = _prenorm(x_ref[...], g_ref[...], mod_ref[0:1, :], mod_ref[1:2, :]).astype(BF16)
    for k, o_ref in enumerate(o_refs):
        o_ref[...] = _dot(h, w_ref[:, k * D:(k + 1) * D])


def _prenorm_proj(x, mod, gains, w_bf16, *, layer, group0, rows_per_group, tm, row0=0, n_rows=None):
    n = x.shape[0] if n_rows is None else n_rows
    n_out = w_bf16.shape[1] // D
    tiles_per_group = rows_per_group // tm
    tile0 = row0 // tm
    return pl.pallas_call(
        _proj_kernel,
        grid=(n // tm,),
        in_specs=[
            pl.BlockSpec((tm, D), lambda i: (tile0 + i, 0)),
            _mod_spec(layer, 1, lambda i: group0 + i // tiles_per_group),
            _gain_spec(layer, 1),
            pl.BlockSpec((D, n_out * D), lambda i: (0, 0)),
        ],
        out_specs=[pl.BlockSpec((tm, D), lambda i: (i, 0)) for _ in range(n_out)],
        out_shape=[jax.ShapeDtypeStruct((n, D), F32) for _ in range(n_out)],
        compiler_params=_params(("parallel",), 56),
        name="prenorm_proj",
    )(x, mod, gains, w_bf16)


def _outproj_kernel(yc_ref, yl_ref, w_ref, *rest, n_ctx_tiles):
    *x_refs, mod_ref, o_ref = rest
    is_ctx = pl.program_id(0) < n_ctx_tiles
    y = jnp.where(is_ctx, yc_ref[...], yl_ref[...])
    if len(x_refs) == 2:
        x = jnp.where(is_ctx, x_refs[0][...], x_refs[1][...])
    else:
        x = x_refs[0][...]
    o_ref[...] = x + mod_ref[2:3, :] * _dot(y, w_ref[...])


def _outproj_residual(y_ctx, y_lat, w_bf16, x, mod, *, layer, lat_len):
    tm = PROJ_TM
    n_ctx_tiles = y_ctx.shape[0] // tm
    n_lat_tiles = y_lat.shape[0] // tm
    lat_tiles_per_seq = lat_len // tm
    ctx_tile = lambda i: jnp.minimum(i, n_ctx_tiles - 1)
    lat_tile = lambda i: jnp.maximum(i - n_ctx_tiles, 0)
    group = lambda i: jnp.where(i < n_ctx_tiles, 0, 1 + lat_tile(i) // lat_tiles_per_seq)
    if isinstance(x, tuple):
        x_args = x
        x_specs = [pl.BlockSpec((tm, D), lambda i: (ctx_tile(i), 0)),
                   pl.BlockSpec((tm, D), lambda i: (lat_tile(i), 0))]
    else:
        x_args = (x,)
        x_specs = [pl.BlockSpec((tm, D), lambda i: (i, 0))]
    n_tiles = n_ctx_tiles + n_lat_tiles
    return pl.pallas_call(
        functools.partial(_outproj_kernel, n_ctx_tiles=n_ctx_tiles),
        grid=(n_tiles,),
        in_specs=[
            pl.BlockSpec((tm, D), lambda i: (ctx_tile(i), 0)),
            pl.BlockSpec((tm, D), lambda i: (lat_tile(i), 0)),
            pl.BlockSpec((D, D), lambda i: (0, 0)),
            *x_specs,
            _mod_spec(layer, 1, group),
        ],
        out_specs=pl.BlockSpec((tm, D), lambda i: (i, 0)),
        out_shape=jax.ShapeDtypeStruct((n_tiles * tm, D), F32),
        compiler_params=_params(("parallel",), 48),
        name="outproj_residual",
    )(y_ctx, y_lat, w_bf16, *x_args, mod)


def _rglru_kernel(gate_ref, xr_ref, cw_ref, cb_ref, wg_ref, bg_ref, lam_ref, h0_ref,
                  y_ref, fin_ref, pad_ref, xc_ref, a0_ref, b0_ref, a1_ref, b1_ref, *, seq_len, n_seq):
    tm = seq_len * n_seq
    pad_ref[0:8, :] = jnp.zeros((8, D), F32)
    pad_ref[8 + tm:16 + tm, :] = jnp.zeros((8, D), F32)
    pad_ref[8:8 + tm, :] = xr_ref[...]
    t_loc = lax.broadcasted_iota(jnp.int32, (tm, 1), 0) % seq_len
    xm2 = jnp.where(t_loc >= 2, pad_ref[6:6 + tm, :], 0.0)
    xm1 = jnp.where(t_loc >= 1, pad_ref[7:7 + tm, :], 0.0)
    xp1 = jnp.where(t_loc <= seq_len - 2, pad_ref[9:9 + tm, :], 0.0)
    xc = xm2 * cw_ref[0:1, :] + xm1 * cw_ref[1:2, :]
    xc = xc + xr_ref[...] * cw_ref[2:3, :]
    xc = xc + xp1 * cw_ref[3:4, :]
    xc_ref[...] = xc + cb_ref[...]

    if n_seq == 1:
        rows = lambda t: pl.ds(t, 1)
    else:
        rows = lambda t: pl.ds(t, n_seq, stride=seq_len)

    n_lg = D // LANES
    ab_refs = ((a0_ref, b0_ref), (a1_ref, b1_ref))
    for d, (a_ref, b_ref) in enumerate(ab_refs):
        half_decay = _lru_half_decay(lam_ref[d:d + 1, :])
        for q in range(N_GATE_GROUPS):
            cols = slice(q * GATE_GROUP, (q + 1) * GATE_GROUP)
            xq = xc_ref[:, cols]
            half_pre = _dot(xq.astype(BF16), wg_ref[d, q]) + bg_ref[d, q]
            a, b = _lru_coeffs(half_pre, 0.5 * xq, half_decay[:, cols])
            for c in range(GATE_GROUP // LANES):
                lg = q * (GATE_GROUP // LANES) + c
                a_ref[lg] = a[:, c * LANES:(c + 1) * LANES]
                b_ref[lg] = b[:, c * LANES:(c + 1) * LANES]

    def step(k, carry):
        nxt = []
        for (a_ref, b_ref), hs, t in zip(ab_refs, carry, (k, seq_len - 1 - k)):
            idx = rows(t)
            out = []
            for lg in range(n_lg):
                h = a_ref[lg, idx, :] * hs[lg] + b_ref[lg, idx, :]
                b_ref[lg, idx, :] = h
                out.append(h)
            nxt.append(tuple(out))
        return tuple(nxt)

    h_init = tuple(tuple(h0_ref[d][:, lg * LANES:(lg + 1) * LANES] for lg in range(n_lg)) for d in range(2))
    h_fin = lax.fori_loop(0, seq_len, step, h_init, unroll=8)
    for d in range(2):
        for lg in range(n_lg):
            fin_ref[d, :, lg * LANES:(lg + 1) * LANES] = h_fin[d][lg]

    for lg in range(n_lg):
        cols = slice(lg * LANES, (lg + 1) * LANES)
        y_ref[:, cols] = ((b0_ref[lg] + b1_ref[lg]) * _gelu_tanh(gate_ref[:, cols])).astype(BF16)


def _rglru_core(gate, xr, conv_w, conv_b, wg, bg, lam, h0, *, seq_len, n_seq):
    n = gate.shape[0]
    tm = seq_len * n_seq
    n_tiles = n // tm
    kern = functools.partial(_rglru_kernel, seq_len=seq_len, n_seq=n_seq)
    return pl.pallas_call(
        kern,
        grid=(n_tiles,),
        in_specs=[
            pl.BlockSpec((tm, D), lambda i: (i, 0)),
            pl.BlockSpec((tm, D), lambda i: (i, 0)),
            pl.BlockSpec((4, D), lambda i: (0, 0)),
            pl.BlockSpec((1, D), lambda i: (0, 0)),
            pl.BlockSpec((2, N_GATE_GROUPS, GATE_GROUP, 2 * GATE_GROUP), lambda i: (0, 0, 0, 0)),
            pl.BlockSpec((2, N_GATE_GROUPS, 1, 2 * GATE_GROUP), lambda i: (0, 0, 0, 0)),
            pl.BlockSpec((2, D), lambda i: (0, 0)),
            pl.BlockSpec((2, None, n_seq, D), lambda i: (0, i, 0, 0)),
        ],
        out_specs=[
            pl.BlockSpec((tm, D), lambda i: (i, 0)),
            pl.BlockSpec((2, None, n_seq, D), lambda i: (0, i, 0, 0)),
        ],
        out_shape=[
            jax.ShapeDtypeStruct((n, D), BF16),
            jax.ShapeDtypeStruct((2, n_tiles, n_seq, D), F32),
        ],
        scratch_shapes=[
            pltpu.VMEM((tm + 16, D), F32),
            pltpu.VMEM((tm, D), F32),
        ] + [pltpu.VMEM((D // LANES, tm, LANES), F32) for _ in range(4)],
        compiler_params=_params(("parallel",), 56),
        name="rglru_core",
    )(gate, xr, conv_w, conv_b, wg, bg, lam, h0)


def _lru_coeffs(half_pre, half_x, half_decay):
    tr = jnp.tanh(half_pre[:, :GATE_GROUP])
    ti = jnp.tanh(half_pre[:, GATE_GROUP:])
    log_a = half_decay * tr + half_decay
    a = jnp.exp(log_a)
    v = -jnp.tanh(log_a) * (1.0 + a * a)
    m = jnp.where(v > 0.0, v * lax.rsqrt(v), 0.0)
    return a, (m * half_x) * (1.0 + ti)


def _lru_half_decay(lam_row):
    z = -lam_row
    return (-0.5 * LRU_C) * (jnp.maximum(z, 0.0) + jnp.log1p(jnp.exp(-jnp.abs(z))))


def _rglru_tm_kernel(gate_ref, xr_ref, cw_ref, cb_ref, wg_ref, bg_ref, lam_ref, y_ref, fin_ref,
                     pad_ref, xc_ref, a0_ref, b0_ref, a1_ref, b1_ref, *, seq_len):
    S = TM_GROUP
    R = seq_len * S
    lead = 2 * S
    n_lg = GATE_GROUP // LANES
    for lg in range(n_lg):
        cols = slice(lg * LANES, (lg + 1) * LANES)
        pad_ref[lg, 0:lead, :] = jnp.zeros((lead, LANES), F32)
        pad_ref[lg, lead + R:lead + R + S, :] = jnp.zeros((S, LANES), F32)
        for s in range(S):
            pad_ref[lg, pl.ds(lead + s, seq_len, stride=S), :] = xr_ref[s * seq_len:(s + 1) * seq_len, cols]
        xc = pad_ref[lg, 0:R, :] * cw_ref[0:1, cols] + pad_ref[lg, S:S + R, :] * cw_ref[1:2, cols]
        xc = xc + pad_ref[lg, lead:lead + R, :] * cw_ref[2:3, cols]
        xc = xc + pad_ref[lg, lead + S:lead + S + R, :] * cw_ref[3:4, cols]
        xc_ref[:, cols] = xc + cb_ref[:, cols]
    xcb = xc_ref[...].astype(BF16)
    for d, (a_ref, b_ref) in enumerate(((a0_ref, b0_ref), (a1_ref, b1_ref))):
        half_pre = _dot(xcb, wg_ref[d]) + bg_ref[d]
        a, b = _lru_coeffs(half_pre, 0.5 * xc_ref[...], _lru_half_decay(lam_ref[d:d + 1, :]))
        a_ref[...] = a
        b_ref[...] = b

    def step(k, carry):
        hf, hb = carry
        rf = pl.ds(pl.multiple_of(k * S, S), S)
        rb = pl.ds(pl.multiple_of((seq_len - 1 - k) * S, S), S)
        hf = a0_ref[rf, :] * hf + b0_ref[rf, :]
        hb = a1_ref[rb, :] * hb + b1_ref[rb, :]
        b0_ref[rf, :] = hf
        b1_ref[rb, :] = hb
        return hf, hb

    zero = jnp.zeros((S, GATE_GROUP), F32)
    hf, hb = lax.fori_loop(0, seq_len, step, (zero, zero), unroll=8)
    fin_ref[0] = hf
    fin_ref[1] = hb
    for lg in range(n_lg):
        cols = slice(lg * LANES, (lg + 1) * LANES)
        pad_ref[lg, 0:R, :] = b0_ref[:, cols] + b1_ref[:, cols]
        for s in range(S):
            rows = slice(s * seq_len, (s + 1) * seq_len)
            ysum = pad_ref[lg, pl.ds(s, seq_len, stride=S), :]
            y_ref[rows, cols] = (ysum * _gelu_tanh(gate_ref[rows, cols])).astype(BF16)


def _rglru_core_tm(gate, xr, conv_w, conv_b, wg, bg, lam, *, seq_len, row_block0=0, n_rows=None):
    n = gate.shape[0] if n_rows is None else n_rows
    rows = seq_len * TM_GROUP
    n_groups = n // rows
    kern = functools.partial(_rglru_tm_kernel, seq_len=seq_len)
    slab_in = pl.BlockSpec((rows, GATE_GROUP), lambda gi, q: (row_block0 + gi, q))
    return pl.pallas_call(
        kern,
        grid=(n_groups, N_GATE_GROUPS),
        in_specs=[
            slab_in,
            slab_in,
            pl.BlockSpec((4, GATE_GROUP), lambda gi, q: (0, q)),
            pl.BlockSpec((1, GATE_GROUP), lambda gi, q: (0, q)),
            pl.BlockSpec((2, None, GATE_GROUP, 2 * GATE_GROUP), lambda gi, q: (0, q, 0, 0)),
            pl.BlockSpec((2, None, 1, 2 * GATE_GROUP), lambda gi, q: (0, q, 0, 0)),
            pl.BlockSpec((2, GATE_GROUP), lambda gi, q: (0, q)),
        ],
        out_specs=[
            pl.BlockSpec((rows, GATE_GROUP), lambda gi, q: (gi, q)),
            pl.BlockSpec((2, None, TM_GROUP, GATE_GROUP), lambda gi, q: (0, gi, 0, q)),
        ],
        out_shape=[
            jax.ShapeDtypeStruct((n, D), BF16),
            jax.ShapeDtypeStruct((2, n_groups, TM_GROUP, D), F32),
        ],
        scratch_shapes=[pltpu.VMEM((GATE_GROUP // LANES, rows + 3 * TM_GROUP, LANES), F32)]
        + [pltpu.VMEM((rows, GATE_GROUP), F32) for _ in range(5)],
        compiler_params=_params(("parallel", "parallel"), 40),
        name="rglru_core_tm",
    )(gate, xr, conv_w, conv_b, wg, bg, lam)


def _gate_weights(gw_a, gb_a, gw_x, gb_x):
    per = GATE_GROUP // LRU_BLOCK
    blk = np.arange(GATE_GROUP) // LRU_BLOCK
    on_diagonal = jnp.asarray(blk[:, None] == blk[None, :])

    def dense(w):
        rows = w.reshape(2, N_GATE_GROUPS, GATE_GROUP, LRU_BLOCK)
        return jnp.where(on_diagonal, jnp.concatenate([rows] * per, axis=-1), 0.0)

    wg = (0.5 * jnp.concatenate([dense(gw_a), dense(gw_x)], axis=-1)).astype(BF16)
    ba = gb_a.reshape(2, N_GATE_GROUPS, 1, GATE_GROUP)
    bx = gb_x.reshape(2, N_GATE_GROUPS, 1, GATE_GROUP)
    return wg, 0.5 * jnp.concatenate([ba, bx], axis=-1)


def _lam_value(lq_ref, lk_ref, lam_init):
    s = jnp.sum(lq_ref[...] * lk_ref[...], axis=-1, keepdims=True)
    e = jnp.exp(s)
    return e[0:1, :] - e[1:2, :] + lam_init


def _softmax_av(s, v_ones):
    e = jnp.exp(s - jnp.max(s, axis=-1, keepdims=True)).astype(BF16)
    ov = _dot(e, v_ones)
    return ov[:, :HEAD_W] * (1.0 / ov[:, HEAD_W:])


def _diff_head(qh, kh_bf16, vh_bf16, lam, g_row, out_scale):
    lane = lax.broadcasted_iota(jnp.int32, qh.shape, 1)
    qs = qh * (MAP_W ** -0.5)
    q0 = jnp.where(lane < MAP_W, qs, 0.0).astype(BF16)
    q1 = jnp.where(lane >= MAP_W, qs, 0.0).astype(BF16)
    v_ones = jnp.concatenate([vh_bf16, jnp.ones_like(vh_bf16)], axis=1)
    o = _softmax_av(_dot_nt(q0, kh_bf16), v_ones) - lam * _softmax_av(_dot_nt(q1, kh_bf16), v_ones)
    return _rms(o, g_row) * out_scale


def _attn_ctx_kernel(q_ref, k_ref, v_ref, lq_ref, lk_ref, g_ref, o_ref, *, lam_init):
    lam = _lam_value(lq_ref, lk_ref, lam_init)
    for hd in range(N_HEADS):
        cols = slice(hd * HEAD_W, (hd + 1) * HEAD_W)
        o = _diff_head(q_ref[:, cols], k_ref[:, cols].astype(BF16), v_ref[:, cols].astype(BF16),
                       lam, g_ref[...], 1.0 - lam_init)
        o_ref[:, cols] = o.astype(BF16)


def _attn_ctx(q, k, v, lq, lk, g, *, seq_len, lam_init):
    n = q.shape[0]
    kern = functools.partial(_attn_ctx_kernel, lam_init=lam_init)
    tok = pl.BlockSpec((seq_len, D), lambda i: (i, 0))
    small = lambda shape: pl.BlockSpec(shape, lambda i: (0, 0))
    return pl.pallas_call(
        kern,
        grid=(n // seq_len,),
        in_specs=[tok, tok, tok, small((2, MAP_W)), small((2, MAP_W)), small((1, HEAD_W))],
        out_specs=tok,
        out_shape=jax.ShapeDtypeStruct((n, D), BF16),
        compiler_params=_params(("parallel",), 32),
        name="attn_context",
    )(q, k, v, lq, lk, g)


def _rope(x, cos, sin_signed):
    lane = lax.broadcasted_iota(jnp.int32, x.shape, 1)
    partner = jnp.where(lane % 32 < 16, pltpu.roll(x, HEAD_W - 16, 1), pltpu.roll(x, 16, 1))
    return x * cos + partner * sin_signed


def _attn_lat_kernel(q_ref, k_ref, v_ref, ck_ref, cv_ref, cos_ref, sin_ref, lq_ref, lk_ref, g_ref,
                     o_ref, kall_ref, vall_ref, *, lam_init, past_len, seq_len, tq):
    qb = pl.program_id(1)

    @pl.when(qb == 0)
    def _():
        kall_ref[0:past_len, :] = ck_ref[...].astype(BF16)
        vall_ref[0:past_len, :] = cv_ref[...].astype(BF16)
        vall_ref[past_len:past_len + seq_len, :] = v_ref[...].astype(BF16)
        for hd in range(N_HEADS):
            cols = slice(hd * HEAD_W, (hd + 1) * HEAD_W)
            kr = _rope(k_ref[:, cols], cos_ref[...], sin_ref[...])
            kall_ref[past_len:past_len + seq_len, cols] = kr.astype(BF16)

    lam = _lam_value(lq_ref, lk_ref, lam_init)
    row0 = pl.multiple_of(qb * tq, tq)
    cos_q = cos_ref[pl.ds(row0, tq), :]
    sin_q = sin_ref[pl.ds(row0, tq), :]
    for hd in range(N_HEADS):
        cols = slice(hd * HEAD_W, (hd + 1) * HEAD_W)
        qh = _rope(q_ref[:, cols], cos_q, sin_q)
        o = _diff_head(qh, kall_ref[:, cols], vall_ref[:, cols], lam, g_ref[...], 1.0 - lam_init)
        o_ref[:, cols] = o.astype(BF16)


def _attn_lat(q, k, v, cache_k, cache_v, cos, sin_signed, lq, lk, g, *, seq_len, lam_init):
    n = q.shape[0]
    n_batch = n // seq_len
    past_len = cache_k.shape[1]
    tq = 256
    n_qb = seq_len // tq
    kern = functools.partial(_attn_lat_kernel, lam_init=lam_init, past_len=past_len,
                             seq_len=seq_len, tq=tq)
    seq = pl.BlockSpec((seq_len, D), lambda b, j: (b, 0))
    cache = pl.BlockSpec((None, past_len, D), lambda b, j: (b, 0, 0))
    small = lambda shape: pl.BlockSpec(shape, lambda b, j: (0, 0))
    qblk = pl.BlockSpec((tq, D), lambda b, j: (b * n_qb + j, 0))
    return pl.pallas_call(
        kern,
        grid=(n_batch, n_qb),
        in_specs=[qblk, seq, seq, cache, cache, small((seq_len, HEAD_W)), small((seq_len, HEAD_W)),
                  small((2, MAP_W)), small((2, MAP_W)), small((1, HEAD_W))],
        out_specs=qblk,
        out_shape=jax.ShapeDtypeStruct((n, D), BF16),
        scratch_shapes=[pltpu.VMEM((past_len + seq_len, D), BF16),
                        pltpu.VMEM((past_len + seq_len, D), BF16)],
        compiler_params=_params(("parallel", "arbitrary"), 48),
        name="attn_latent",
    )(q, k, v, cache_k, cache_v, cos, sin_signed, lq, lk, g)


def _rope_tables(seq_len):
    t = np.arange(seq_len)
    pos = np.stack([t // GRID_W, t % GRID_W], axis=-1).astype(np.float32)
    nf = MAP_W // 4
    inv = (np.float32(ROPE_THETA) ** (-np.arange(nf, dtype=np.float32) / nf)).astype(np.float32)
    lane = np.arange(HEAD_W)
    axis = (lane % MAP_W) // 32
    freq = lane % nf
    sign = np.where(lane % 32 < nf, -1.0, 1.0).astype(np.float32)
    pos_l = jnp.asarray(pos)[:, axis]
    ang = pos_l * jnp.asarray(inv)[freq][None, :]
    return jnp.cos(ang), jnp.sin(ang) * jnp.asarray(sign)[None, :]


def _pool_sequence(x_ref, mod_ref, g_ref, w_ref, sc_ref, o_ref, h_ref, lvl_ref, T):
    R = T + 2 * POOL_PAD
    h = _prenorm(x_ref[...], g_ref[...], mod_ref[0:1, :], mod_ref[1:2, :])
    h_ref[...] = h
    lvl_ref[0:POOL_PAD, :] = jnp.zeros((POOL_PAD, D), F32)
    lvl_ref[POOL_PAD + T:R, :] = jnp.zeros((POOL_PAD, D), F32)
    lvl_ref[POOL_PAD:POOL_PAD + T, :] = h
    t = lax.broadcasted_iota(jnp.int32, (T, 1), 0)
    for gi, win in enumerate(POOL_WINDOWS):
        cols = slice(gi * POOL_GROUP, (gi + 1) * POOL_GROUP)
        half = win // 2
        s = 1
        while s < win:
            lvl_ref[POOL_PAD:R, cols] = lvl_ref[POOL_PAD - s:R - s, cols] + lvl_ref[POOL_PAD:R, cols]
            s *= 2
        wsum = lvl_ref[POOL_PAD + half - 1:POOL_PAD + half - 1 + T, cols]
        cnt = (jnp.minimum(t + half, T) - jnp.maximum(t - half, 0)).astype(F32)
        diff = wsum * (1.0 / cnt) - h_ref[:, cols]
        m = _dot(diff.astype(BF16), w_ref[gi]) * sc_ref[:, cols]
        o_ref[:, cols] = x_ref[:, cols] + mod_ref[2:3, cols] * m


def _pool_kernel(x_ref, mod_ref, g_ref, w_ref, sc_ref, o_ref, h_ref, lvl_ref, *, n_ctx_tiles, ctx_len):
    tile_rows = x_ref.shape[0]
    is_ctx = pl.program_id(0) < n_ctx_tiles

    @pl.when(is_ctx)
    def _():
        for k in range(tile_rows // ctx_len):
            rows = slice(k * ctx_len, (k + 1) * ctx_len)
            _pool_sequence(x_ref.at[rows], mod_ref, g_ref, w_ref, sc_ref, o_ref.at[rows], h_ref.at[rows],
                           lvl_ref.at[0:ctx_len + 2 * POOL_PAD], ctx_len)

    @pl.when(jnp.logical_not(is_ctx))
    def _():
        _pool_sequence(x_ref, mod_ref, g_ref, w_ref, sc_ref, o_ref, h_ref, lvl_ref, tile_rows)


def _pool_mixer(x, mod, gains, w_bf16, scale, *, layer, n_ctx, ctx_len, lat_len):
    n = x.shape[0]
    tm = lat_len
    n_ctx_tiles = n_ctx // tm
    kern = functools.partial(_pool_kernel, n_ctx_tiles=n_ctx_tiles, ctx_len=ctx_len)
    tok = pl.BlockSpec((tm, D), lambda i: (i, 0))
    return pl.pallas_call(
        kern,
        grid=(n // tm,),
        in_specs=[
            tok,
            _mod_spec(layer, 1, lambda i: jnp.maximum(i - (n_ctx_tiles - 1), 0)),
            _gain_spec(layer, 1),
            pl.BlockSpec((len(POOL_WINDOWS), POOL_GROUP, POOL_GROUP), lambda i: (0, 0, 0)),
            pl.BlockSpec((1, D), lambda i: (0, 0)),
        ],
        out_specs=tok,
        out_shape=jax.ShapeDtypeStruct((n, D), F32),
        scratch_shapes=[pltpu.VMEM((tm, D), F32),
                        pltpu.VMEM((tm + 2 * POOL_PAD, D), F32)],
        compiler_params=_params(("parallel",), 40),
        name="pool_mixer",
    )(x, mod, gains, w_bf16, scale)


def kernel(x_prompt, x_sample, state_rglru, cache_k_diff, cache_v_diff, c, c_ctx, norm_g, w_mod, b_mod, w_ffn_in, w_ffn_out, a_w_in, a_conv_w, a_conv_b, a_gate_w_a, a_gate_b_a, a_gate_w_x, a_gate_b_x, a_lambda, a_w_out, b_w_qkv, b_lam_q, b_lam_k, b_subln_g, b_w_o, c_w_pool, c_scale, final_norm_g):
    n_ctx_seq, ctx_len, _ = x_prompt.shape
    n_lat_seq, lat_len, _ = x_sample.shape
    n_ctx, n_lat = n_ctx_seq * ctx_len, n_lat_seq * lat_len
    xp = x_prompt.reshape(n_ctx, D)
    xs = x_sample.reshape(n_lat, D)

    cond8 = jnp.concatenate([c_ctx[None], c, jnp.zeros((8 - 1 - n_lat_seq, D), F32)], axis=0)
    mod = _adaln_all(cond8, w_mod, b_mod).reshape(DEPTH, 8, N_MOD // 3, 3, D)
    gains = norm_g.reshape(DEPTH, 3, 1, D)
    final_g = final_norm_g.reshape(1, D)

    assert n_ctx_seq % TM_GROUP == 0 and lat_len == FFN_TM and n_ctx % FFN_TM == 0
    n_ctx_tiles = n_ctx // FFN_TM
    new_states, new_k, new_v = [], [], []
    x = (xp, xs)
    for l in range(DEPTH):
        kind, j = l % 3, l // 3

        def ffn(x, k, half, last):
            return _ffn(x, mod, gains, w_ffn_in, w_ffn_out, final_g, layer=l, k=k, half=half,
                        n_ctx_tiles=n_ctx_tiles, final_norm=last, split_out=last)

        x = ffn(x, 0, 0, False)

        ctx_proj = dict(mod=mod, gains=gains, layer=l, group0=0, rows_per_group=n_ctx, tm=PROJ_TM)
        lat_proj = dict(mod=mod, gains=gains, layer=l, group0=1, rows_per_group=lat_len, tm=PROJ_TM // 2)
        if isinstance(x, tuple):
            ctx_proj.update(x=x[0])
            lat_proj.update(x=x[1])
        else:
            ctx_proj.update(x=x, row0=0, n_rows=n_ctx)
            lat_proj.update(x=x, row0=n_ctx, n_rows=n_lat)
        if kind == 0:
            w_in = a_w_in[j].astype(BF16)
            w_out = a_w_out[j].astype(BF16)
            wg, bg = _gate_weights(a_gate_w_a[j], a_gate_b_a[j], a_gate_w_x[j], a_gate_b_x[j])
            conv_b = a_conv_b[j].reshape(1, D)
            gate_p, xr_p = _prenorm_proj(w_bf16=w_in, **ctx_proj)
            y_p, fin = _rglru_core_tm(gate_p, xr_p, a_conv_w[j], conv_b, wg, bg, a_lambda[j],
                                      seq_len=ctx_len)
            new_states.append(jnp.transpose(fin.reshape(2, n_ctx_seq, D), (1, 0, 2)))
            gate_s, xr_s = _prenorm_proj(w_bf16=w_in, **lat_proj)
            h0 = jnp.transpose(state_rglru[:, j].astype(F32), (1, 0, 2)).reshape(2, n_lat_seq, 1, D)
            y_s, _ = _rglru_core(gate_s, xr_s, a_conv_w[j], conv_b, wg, bg, a_lambda[j], h0,
                                 seq_len=lat_len, n_seq=1)
            x = _outproj_residual(y_p, y_s, w_out, x, mod, layer=l, lat_len=lat_len)
        elif kind == 1:
            lam_init = 0.8 - 0.6 * math.exp(-0.3 * l)
            w_qkv = b_w_qkv[j].astype(BF16)
            w_o = b_w_o[j].astype(BF16)
            sub_g = b_subln_g[j].reshape(1, HEAD_W)
            q_s, k_s, v_s = _prenorm_proj(w_bf16=w_qkv, **lat_proj)
            q_p, k_p, v_p = _prenorm_proj(w_bf16=w_qkv, **ctx_proj)
            o_p = _attn_ctx(q_p, k_p, v_p, b_lam_q[j], b_lam_k[j], sub_g, seq_len=ctx_len,
                            lam_init=lam_init)
            new_k.append(k_p.reshape(n_ctx_seq, ctx_len, N_HEADS, HEAD_W))
            new_v.append(v_p.reshape(n_ctx_seq, ctx_len, N_HEADS, HEAD_W))
            cos, sin_signed = _rope_tables(lat_len)
            past_len = cache_k_diff.shape[2]
            o_s = _attn_lat(q_s, k_s, v_s,
                            cache_k_diff[:, j].reshape(n_lat_seq, past_len, D),
                            cache_v_diff[:, j].reshape(n_lat_seq, past_len, D),
                            cos, sin_signed, b_lam_q[j], b_lam_k[j], sub_g,
                            seq_len=lat_len, lam_init=lam_init)
            x = _outproj_residual(o_p, o_s, w_o, x, mod, layer=l, lat_len=lat_len)
        else:
            assert not isinstance(x, tuple)
            x = _pool_mixer(x, mod, gains, c_w_pool[j].astype(BF16), c_scale[j].reshape(1, D),
                            layer=l, n_ctx=n_ctx, ctx_len=ctx_len, lat_len=lat_len)

        x = ffn(x, 2, 1, l == DEPTH - 1)

    xp, xs = x
    y_prompt = xp.reshape(x_prompt.shape)
    y_sample = xs.reshape(x_sample.shape)
    new_state = jnp.stack(new_states, axis=1).astype(x_prompt.dtype)
    new_cache_k = jnp.stack(new_k, axis=1).astype(x_prompt.dtype)
    new_cache_v = jnp.stack(new_v, axis=1).astype(x_prompt.dtype)
    return (y_prompt, y_sample, new_state, new_cache_k, new_cache_v)
```

```python
import functools
import math

import jax
import jax.numpy as jnp
import numpy as np
from jax import lax
from jax.experimental import pallas as pl
from jax.experimental.pallas import tpu as pltpu

F32 = jnp.float32
BF16 = jnp.bfloat16

D = 1024
DEPTH = 4
N_MOD = 9
EPS = 1e-6
D_FF = 2816
LRU_BLOCK = 64
N_LRU_BLOCKS = 16
LRU_C = 8.0
N_HEADS = 8
LANES = 128
HEAD_W = 128
MAP_W = 64
assert MAP_W ** -0.5 == 0.125
GRID_W = 64
ROPE_THETA = 10000.0
POOL_WINDOWS = (2, 4, 8, 16)
POOL_GROUP = 256
POOL_PAD = 16

GATE_GROUP = 256
N_GATE_GROUPS = D // GATE_GROUP

MIB = 1024 * 1024


def _params(sem, vmem_mib):
    return pltpu.CompilerParams(dimension_semantics=sem, vmem_limit_bytes=vmem_mib * MIB)


def _dot(a, b):
    return jnp.dot(a, b, preferred_element_type=F32)


def _dot_nt(a, b):
    return lax.dot_general(a, b, (((1,), (1,)), ((), ())), preferred_element_type=F32)


def _sigmoid(x):
    return 0.5 + 0.5 * jnp.tanh(0.5 * x)


def _silu(x):
    return x * _sigmoid(x)


def _gelu_tanh(x):
    c = math.sqrt(2.0 / math.pi)
    hx = 0.5 * x
    return hx + hx * jnp.tanh(x * (c + (0.044715 * c) * (x * x)))


def _rms(x, g):
    ms = jnp.mean(x * x, axis=-1, keepdims=True)
    return (x * lax.rsqrt(ms + EPS)) * g


def _prenorm(x, g, shift, scale):
    return _rms(x, g) * (1.0 + scale) + shift


def _mod_kernel(cond_ref, w_ref, b_ref, o_ref):
    s = _silu(cond_ref[...]).astype(BF16)
    o_ref[...] = _dot(s, w_ref[...].astype(BF16)) + b_ref[...]


def _adaln_all(cond8, w_mod, b_mod):
    tn = 1536
    n = N_MOD * D
    return pl.pallas_call(
        _mod_kernel,
        grid=(DEPTH, n // tn),
        in_specs=[
            pl.BlockSpec((8, D), lambda l, j: (0, 0)),
            pl.BlockSpec((None, D, tn), lambda l, j: (l, 0, j)),
            pl.BlockSpec((None, 1, tn), lambda l, j: (l, 0, j)),
        ],
        out_specs=pl.BlockSpec((None, 8, tn), lambda l, j: (l, 0, j)),
        out_shape=jax.ShapeDtypeStruct((DEPTH, 8, n), F32),
        compiler_params=_params(("parallel", "parallel"), 32),
        name="adaln_mod",
    )(cond8, w_mod, b_mod.reshape(DEPTH, 1, n))


FFN_TM = 1024
FFN_FC = 256
FFN_CHUNKS = D_FF // FFN_FC


TM_GROUP = 8
PROJ_TM = 1024


def _ffn_kernel(*refs, final_norm, n_x, n_o, n_ctx_tiles):
    x_refs = refs[:n_x]
    mod_ref, g_ref, wa_ref, wb_ref, wo_ref, fg_ref = refs[n_x:n_x + 6]
    o_refs = refs[n_x + 6:n_x + 6 + n_o]
    wa_s, wb_s, wo_s, h_ref, u_ref = refs[n_x + 6 + n_o:]
    s = pl.program_id(0)
    fc = FFN_FC
    is_ctx = s - (FFN_CHUNKS - 1) < n_ctx_tiles

    def x_tile():
        if n_x == 1:
            return x_refs[0][...]
        return jnp.where(is_ctx, x_refs[0][...], x_refs[1][...])

    def hidden():
        h = _prenorm(x_tile(), g_ref[...], mod_ref[0:1, :], mod_ref[1:2, :])
        h_ref[...] = h.astype(BF16)

    def result(acc):
        y = x_tile() + (0.5 * mod_ref[2:3, :]) * acc
        return _rms(y, fg_ref[...]) if final_norm else y

    o_ref = o_refs[0]

    def stage():
        wa = wa_ref[...].astype(BF16)
        wb = wb_ref[...].astype(BF16)
        wo = wo_ref[...].astype(BF16)
        wa_s[s] = wa
        wb_s[s] = wb
        wo_s[pl.ds(pl.multiple_of(s * fc, fc), fc), :] = wo
        h = h_ref[...]
        u = (_silu(_dot(h, wa)) * _dot(h, wb)).astype(BF16)
        o_ref[...] += _dot(u, wo)

    @pl.when(s == 0)
    def _():
        hidden()
        o_ref[...] = jnp.zeros_like(o_ref)
        stage()

    @pl.when(jnp.logical_and(s > 0, s < FFN_CHUNKS))
    def _():
        stage()

        @pl.when(s == FFN_CHUNKS - 1)
        def _():
            o_ref[...] = result(o_ref[...])

    @pl.when(s >= FFN_CHUNKS)
    def _():
        hidden()
        for j in range(FFN_CHUNKS):
            h = h_ref[...]
            u = _silu(_dot(h, wa_s[j])) * _dot(h, wb_s[j])
            u_ref[:, j * fc:(j + 1) * fc] = u.astype(BF16)
        y = result(_dot(u_ref[...], wo_s[...]))
        if n_o == 1:
            o_refs[0][...] = y
        else:
            @pl.when(is_ctx)
            def _():
                o_refs[0][...] = y

            @pl.when(jnp.logical_not(is_ctx))
            def _():
                o_refs[1][...] = y


def _mod_spec(layer, k, group):
    return pl.BlockSpec((None, None, None, 3, D), lambda i: (layer, group(i), k, 0, 0))


def _gain_spec(layer, k):
    return pl.BlockSpec((None, None, 1, D), lambda i: (layer, k, 0, 0))


def _ffn(x, mod, gains, w_in, w_out, final_g, *, layer, k, half, n_ctx_tiles, final_norm, split_out):
    tm, fc, nc = FFN_TM, FFN_FC, FFN_CHUNKS
    xs = x if isinstance(x, tuple) else (x,)
    n = sum(a.shape[0] for a in xs)
    n_ctx = n_ctx_tiles * tm
    tile = lambda s: jnp.maximum(s - (nc - 1), 0)
    chunk = lambda s: jnp.minimum(s, nc - 1)
    one_spec = [pl.BlockSpec((tm, D), lambda s: (tile(s), 0))]
    two_specs = [pl.BlockSpec((tm, D), lambda s: (jnp.minimum(tile(s), n_ctx_tiles - 1), 0)),
                 pl.BlockSpec((tm, D), lambda s: (jnp.maximum(tile(s) - n_ctx_tiles, 0), 0))]
    two_shapes = [jax.ShapeDtypeStruct((n_ctx, D), F32), jax.ShapeDtypeStruct((n - n_ctx, D), F32)]
    n_o = 2 if split_out else 1
    kern = functools.partial(_ffn_kernel, final_norm=final_norm, n_x=len(xs), n_o=n_o,
                             n_ctx_tiles=n_ctx_tiles)
    out = pl.pallas_call(
        kern,
        grid=(nc - 1 + n // tm,),
        in_specs=[
            *(two_specs if len(xs) == 2 else one_spec),
            _mod_spec(layer, k, lambda s: jnp.maximum(tile(s) - (n_ctx_tiles - 1), 0)),
            _gain_spec(layer, k),
            pl.BlockSpec((None, None, D, fc), lambda s: (layer, half, 0, chunk(s))),
            pl.BlockSpec((None, None, D, fc), lambda s: (layer, half, 0, nc + chunk(s))),
            pl.BlockSpec((None, None, fc, D), lambda s: (layer, half, chunk(s), 0)),
            pl.BlockSpec((1, D), lambda s: (0, 0)),
        ],
        out_specs=two_specs if split_out else one_spec,
        out_shape=two_shapes if split_out else [jax.ShapeDtypeStruct((n, D), F32)],
        scratch_shapes=[
            pltpu.VMEM((nc, D, fc), BF16),
            pltpu.VMEM((nc, D, fc), BF16),
            pltpu.VMEM((D_FF, D), BF16),
            pltpu.VMEM((tm, D), BF16),
            pltpu.VMEM((tm, D_FF), BF16),
        ],
        compiler_params=_params(("arbitrary",), 60),
        name="ffn",
    )(*xs, mod, gains, w_in, w_in, w_out, final_g)
    return tuple(out) if split_out else out[0]


def _proj_kernel(x_ref, mod_ref, g_ref, w_ref, *o_refs):
    h = _prenorm(x_ref[...], g_ref[...], mod_ref[0:1, :], mod_ref[1:2, :]).astype(BF16)
    for k, o_ref in enumerate(o_refs):
        o_ref[...] = _dot(h, w_ref[:, k * D:(k + 1) * D])


def _prenorm_proj(x, mod, gains, w_bf16, *, layer, group0, rows_per_group, tm, row0=0, n_rows=None):
    n = x.shape[0] if n_rows is None else n_rows
    n_out = w_bf16.shape[1] // D
    tiles_per_group = rows_per_group // tm
    tile0 = row0 // tm
    return pl.pallas_call(
        _proj_kernel,
        grid=(n // tm,),
        in_specs=[
            pl.BlockSpec((tm, D), lambda i: (tile0 + i, 0)),
            _mod_spec(layer, 1, lambda i: group0 + i // tiles_per_group),
            _gain_spec(layer, 1),
            pl.BlockSpec((D, n_out * D), lambda i: (0, 0)),
        ],
        out_specs=[pl.BlockSpec((tm, D), lambda i: (i, 0)) for _ in range(n_out)],
        out_shape=[jax.ShapeDtypeStruct((n, D), F32) for _ in range(n_out)],
        compiler_params=_params(("parallel",), 56),
        name="prenorm_proj",
    )(x, mod, gains, w_bf16)


def _outproj_kernel(yc_ref, yl_ref, w_ref, *rest, n_ctx_tiles):
    *x_refs, mod_ref, o_ref = rest
    is_ctx = pl.program_id(0) < n_ctx_tiles
    y = jnp.where(is_ctx, yc_ref[...], yl_ref[...])
    if len(x_refs) == 2:
        x = jnp.where(is_ctx, x_refs[0][...], x_refs[1][...])
    else:
        x = x_refs[0][...]
    o_ref[...] = x + mod_ref[2:3, :] * _dot(y, w_ref[...])


def _outproj_residual(y_ctx, y_lat, w_bf16, x, mod, *, layer, lat_len):
    tm = PROJ_TM
    n_ctx_tiles = y_ctx.shape[0] // tm
    n_lat_tiles = y_lat.shape[0] // tm
    lat_tiles_per_seq = lat_len // tm
    ctx_tile = lambda i: jnp.minimum(i, n_ctx_tiles - 1)
    lat_tile = lambda i: jnp.maximum(i - n_ctx_tiles, 0)
    group = lambda i: jnp.where(i < n_ctx_tiles, 0, 1 + lat_tile(i) // lat_tiles_per_seq)
    if isinstance(x, tuple):
        x_args = x
        x_specs = [pl.BlockSpec((tm, D), lambda i: (ctx_tile(i), 0)),
                   pl.BlockSpec((tm, D), lambda i: (lat_tile(i), 0))]
    else:
        x_args = (x,)
        x_specs = [pl.BlockSpec((tm, D), lambda i: (i, 0))]
    n_tiles = n_ctx_tiles + n_lat_tiles
    return pl.pallas_call(
        functools.partial(_outproj_kernel, n_ctx_tiles=n_ctx_tiles),
        grid=(n_tiles,),
        in_specs=[
            pl.BlockSpec((tm, D), lambda i: (ctx_tile(i), 0)),
            pl.BlockSpec((tm, D), lambda i: (lat_tile(i), 0)),
            pl.BlockSpec((D, D), lambda i: (0, 0)),
            *x_specs,
            _mod_spec(layer, 1, group),
        ],
        out_specs=pl.BlockSpec((tm, D), lambda i: (i, 0)),
        out_shape=jax.ShapeDtypeStruct((n_tiles * tm, D), F32),
        compiler_params=_params(("parallel",), 48),
        name="outproj_residual",
    )(y_ctx, y_lat, w_bf16, *x_args, mod)


def _rglru_kernel(gate_ref, xr_ref, cw_ref, cb_ref, wg_ref, bg_ref, lam_ref, h0_ref,
                  y_ref, fin_ref, pad_ref, xc_ref, a0_ref, b0_ref, a1_ref, b1_ref, *, seq_len, n_seq):
    tm = seq_len * n_seq
    pad_ref[0:8, :] = jnp.zeros((8, D), F32)
    pad_ref[8 + tm:16 + tm, :] = jnp.zeros((8, D), F32)
    pad_ref[8:8 + tm, :] = xr_ref[...]
    t_loc = lax.broadcasted_iota(jnp.int32, (tm, 1), 0) % seq_len
    xm2 = jnp.where(t_loc >= 2, pad_ref[6:6 + tm, :], 0.0)
    xm1 = jnp.where(t_loc >= 1, pad_ref[7:7 + tm, :], 0.0)
    xp1 = jnp.where(t_loc <= seq_len - 2, pad_ref[9:9 + tm, :], 0.0)
    xc = xm2 * cw_ref[0:1, :] + xm1 * cw_ref[1:2, :]
    xc = xc + xr_ref[...] * cw_ref[2:3, :]
    xc = xc + xp1 * cw_ref[3:4, :]
    xc_ref[...] = xc + cb_ref[...]

    if n_seq == 1:
        rows = lambda t: pl.ds(t, 1)
    else:
        rows = lambda t: pl.ds(t, n_seq, stride=seq_len)

    n_lg = D // LANES
    ab_refs = ((a0_ref, b0_ref), (a1_ref, b1_ref))
    for d, (a_ref, b_ref) in enumerate(ab_refs):
        half_decay = _lru_half_decay(lam_ref[d:d + 1, :])
        for q in range(N_GATE_GROUPS):
            cols = slice(q * GATE_GROUP, (q + 1) * GATE_GROUP)
            xq = xc_ref[:, cols]
            half_pre = _dot(xq.astype(BF16), wg_ref[d, q]) + bg_ref[d, q]
            a, b = _lru_coeffs(half_pre, 0.5 * xq, half_decay[:, cols])
            for c in range(GATE_GROUP // LANES):
                lg = q * (GATE_GROUP // LANES) + c
                a_ref[lg] = a[:, c * LANES:(c + 1) * LANES]
                b_ref[lg] = b[:, c * LANES:(c + 1) * LANES]

    def step(k, carry):
        nxt = []
        for (a_ref, b_ref), hs, t in zip(ab_refs, carry, (k, seq_len - 1 - k)):
            idx = rows(t)
            out = []
            for lg in range(n_lg):
                h = a_ref[lg, idx, :] * hs[lg] + b_ref[lg, idx, :]
                b_ref[lg, idx, :] = h
                out.append(h)
            nxt.append(tuple(out))
        return tuple(nxt)

    h_init = tuple(tuple(h0_ref[d][:, lg * LANES:(lg + 1) * LANES] for lg in range(n_lg)) for d in range(2))
    h_fin = lax.fori_loop(0, seq_len, step, h_init, unroll=8)
    for d in range(2):
        for lg in range(n_lg):
            fin_ref[d, :, lg * LANES:(lg + 1) * LANES] = h_fin[d][lg]

    for lg in range(n_lg):
        cols = slice(lg * LANES, (lg + 1) * LANES)
        y_ref[:, cols] = ((b0_ref[lg] + b1_ref[lg]) * _gelu_tanh(gate_ref[:, cols])).astype(BF16)


def _rglru_core(gate, xr, conv_w, conv_b, wg, bg, lam, h0, *, seq_len, n_seq):
    n = gate.shape[0]
    tm = seq_len * n_seq
    n_tiles = n // tm
    kern = functools.partial(_rglru_kernel, seq_len=seq_len, n_seq=n_seq)
    return pl.pallas_call(
        kern,
        grid=(n_tiles,),
        in_specs=[
            pl.BlockSpec((tm, D), lambda i: (i, 0)),
            pl.BlockSpec((tm, D), lambda i: (i, 0)),
            pl.BlockSpec((4, D), lambda i: (0, 0)),
            pl.BlockSpec((1, D), lambda i: (0, 0)),
            pl.BlockSpec((2, N_GATE_GROUPS, GATE_GROUP, 2 * GATE_GROUP), lambda i: (0, 0, 0, 0)),
            pl.BlockSpec((2, N_GATE_GROUPS, 1, 2 * GATE_GROUP), lambda i: (0, 0, 0, 0)),
            pl.BlockSpec((2, D), lambda i: (0, 0)),
            pl.BlockSpec((2, None, n_seq, D), lambda i: (0, i, 0, 0)),
        ],
        out_specs=[
            pl.BlockSpec((tm, D), lambda i: (i, 0)),
            pl.BlockSpec((2, None, n_seq, D), lambda i: (0, i, 0, 0)),
        ],
        out_shape=[
            jax.ShapeDtypeStruct((n, D), BF16),
            jax.ShapeDtypeStruct((2, n_tiles, n_seq, D), F32),
        ],
        scratch_shapes=[
            pltpu.VMEM((tm + 16, D), F32),
            pltpu.VMEM((tm, D), F32),
        ] + [pltpu.VMEM((D // LANES, tm, LANES), F32) for _ in range(4)],
        compiler_params=_params(("parallel",), 56),
        name="rglru_core",
    )(gate, xr, conv_w, conv_b, wg, bg, lam, h0)


def _lru_coeffs(half_pre, half_x, half_decay):
    tr = jnp.tanh(half_pre[:, :GATE_GROUP])
    ti = jnp.tanh(half_pre[:, GATE_GROUP:])
    log_a = half_decay * tr + half_decay
    a = jnp.exp(log_a)
    v = -jnp.tanh(log_a) * (1.0 + a * a)
    m = jnp.where(v > 0.0, v * lax.rsqrt(v), 0.0)
    return a, (m * half_x) * (1.0 + ti)


def _lru_half_decay(lam_row):
    z = -lam_row
    return (-0.5 * LRU_C) * (jnp.maximum(z, 0.0) + jnp.log1p(jnp.exp(-jnp.abs(z))))


def _rglru_tm_kernel(gate_ref, xr_ref, cw_ref, cb_ref, wg_ref, bg_ref, lam_ref, y_ref, fin_ref,
                     pad_ref, xc_ref, a0_ref, b0_ref, a1_ref, b1_ref, *, seq_len):
    S = TM_GROUP
    R = seq_len * S
    lead = 2 * S
    n_lg = GATE_GROUP // LANES
    for lg in range(n_lg):
        cols = slice(lg * LANES, (lg + 1) * LANES)
        pad_ref[lg, 0:lead, :] = jnp.zeros((lead, LANES), F32)
        pad_ref[lg, lead + R:lead + R + S, :] = jnp.zeros((S, LANES), F32)
        for s in range(S):
            pad_ref[lg, pl.ds(lead + s, seq_len, stride=S), :] = xr_ref[s * seq_len:(s + 1) * seq_len, cols]
        xc = pad_ref[lg, 0:R, :] * cw_ref[0:1, cols] + pad_ref[lg, S:S + R, :] * cw_ref[1:2, cols]
        xc = xc + pad_ref[lg, lead:lead + R, :] * cw_ref[2:3, cols]
        xc = xc + pad_ref[lg, lead + S:lead + S + R, :] * cw_ref[3:4, cols]
        xc_ref[:, cols] = xc + cb_ref[:, cols]
    xcb = xc_ref[...].astype(BF16)
    for d, (a_ref, b_ref) in enumerate(((a0_ref, b0_ref), (a1_ref, b1_ref))):
        half_pre = _dot(xcb, wg_ref[d]) + bg_ref[d]
        a, b = _lru_coeffs(half_pre, 0.5 * xc_ref[...], _lru_half_decay(lam_ref[d:d + 1, :]))
        a_ref[...] = a
        b_ref[...] = b

    def step(k, carry):
        hf, hb = carry
        rf = pl.ds(pl.multiple_of(k * S, S), S)
        rb = pl.ds(pl.multiple_of((seq_len - 1 - k) * S, S), S)
        hf = a0_ref[rf, :] * hf + b0_ref[rf, :]
        hb = a1_ref[rb, :] * hb + b1_ref[rb, :]
        b0_ref[rf, :] = hf
        b1_ref[rb, :] = hb
        return hf, hb

    zero = jnp.zeros((S, GATE_GROUP), F32)
    hf, hb = lax.fori_loop(0, seq_len, step, (zero, zero), unroll=8)
    fin_ref[0] = hf
    fin_ref[1] = hb
    for lg in range(n_lg):
        cols = slice(lg * LANES, (lg + 1) * LANES)
        pad_ref[lg, 0:R, :] = b0_ref[:, cols] + b1_ref[:, cols]
        for s in range(S):
            rows = slice(s * seq_len, (s + 1) * seq_len)
            ysum = pad_ref[lg, pl.ds(s, seq_len, stride=S), :]
            y_ref[rows, cols] = (ysum * _gelu_tanh(gate_ref[rows, cols])).astype(BF16)


def _rglru_core_tm(gate, xr, conv_w, conv_b, wg, bg, lam, *, seq_len, row_block0=0, n_rows=None):
    n = gate.shape[0] if n_rows is None else n_rows
    rows = seq_len * TM_GROUP
    n_groups = n // rows
    kern = functools.partial(_rglru_tm_kernel, seq_len=seq_len)
    slab_in = pl.BlockSpec((rows, GATE_GROUP), lambda gi, q: (row_block0 + gi, q))
    return pl.pallas_call(
        kern,
        grid=(n_groups, N_GATE_GROUPS),
        in_specs=[
            slab_in,
            slab_in,
            pl.BlockSpec((4, GATE_GROUP), lambda gi, q: (0, q)),
            pl.BlockSpec((1, GATE_GROUP), lambda gi, q: (0, q)),
            pl.BlockSpec((2, None, GATE_GROUP, 2 * GATE_GROUP), lambda gi, q: (0, q, 0, 0)),
            pl.BlockSpec((2, None, 1, 2 * GATE_GROUP), lambda gi, q: (0, q, 0, 0)),
            pl.BlockSpec((2, GATE_GROUP), lambda gi, q: (0, q)),
        ],
        out_specs=[
            pl.BlockSpec((rows, GATE_GROUP), lambda gi, q: (gi, q)),
            pl.BlockSpec((2, None, TM_GROUP, GATE_GROUP), lambda gi, q: (0, gi, 0, q)),
        ],
        out_shape=[
            jax.ShapeDtypeStruct((n, D), BF16),
            jax.ShapeDtypeStruct((2, n_groups, TM_GROUP, D), F32),
        ],
        scratch_shapes=[pltpu.VMEM((GATE_GROUP // LANES, rows + 3 * TM_GROUP, LANES), F32)]
        + [pltpu.VMEM((rows, GATE_GROUP), F32) for _ in range(5)],
        compiler_params=_params(("parallel", "parallel"), 40),
        name="rglru_core_tm",
    )(gate, xr, conv_w, conv_b, wg, bg, lam)


def _gate_weights(gw_a, gb_a, gw_x, gb_x):
    per = GATE_GROUP // LRU_BLOCK
    blk = np.arange(GATE_GROUP) // LRU_BLOCK
    on_diagonal = jnp.asarray(blk[:, None] == blk[None, :])

    def dense(w):
        rows = w.reshape(2, N_GATE_GROUPS, GATE_GROUP, LRU_BLOCK)
        return jnp.where(on_diagonal, jnp.concatenate([rows] * per, axis=-1), 0.0)

    wg = (0.5 * jnp.concatenate([dense(gw_a), dense(gw_x)], axis=-1)).astype(BF16)
    ba = gb_a.reshape(2, N_GATE_GROUPS, 1, GATE_GROUP)
    bx = gb_x.reshape(2, N_GATE_GROUPS, 1, GATE_GROUP)
    return wg, 0.5 * jnp.concatenate([ba, bx], axis=-1)


def _lam_value(lq_ref, lk_ref, lam_init):
    s = jnp.sum(lq_ref[...] * lk_ref[...], axis=-1, keepdims=True)
    e = jnp.exp(s)
    return e[0:1, :] - e[1:2, :] + lam_init


def _softmax_av(s, v_ones):
    e = jnp.exp(s - jnp.max(s, axis=-1, keepdims=True)).astype(BF16)
    ov = _dot(e, v_ones)
    return ov[:, :HEAD_W] * (1.0 / ov[:, HEAD_W:])


def _diff_head(qh, kh_bf16, vh_bf16, lam, g_row, out_scale):
    lane = lax.broadcasted_iota(jnp.int32, qh.shape, 1)
    qs = qh * (MAP_W ** -0.5)
    q0 = jnp.where(lane < MAP_W, qs, 0.0).astype(BF16)
    q1 = jnp.where(lane >= MAP_W, qs, 0.0).astype(BF16)
    v_ones = jnp.concatenate([vh_bf16, jnp.ones_like(vh_bf16)], axis=1)
    o = _softmax_av(_dot_nt(q0, kh_bf16), v_ones) - lam * _softmax_av(_dot_nt(q1, kh_bf16), v_ones)
    return _rms(o, g_row) * out_scale


ATTN_CTX_SEQS = 2


def _attn_ctx_kernel(q_ref, k_ref, v_ref, lq_ref, lk_ref, g_ref, o_ref, *, lam_init, seq_len):
    lam = _lam_value(lq_ref, lk_ref, lam_init)
    for sq in range(ATTN_CTX_SEQS):
        rows = slice(sq * seq_len, (sq + 1) * seq_len)
        for hd in range(N_HEADS):
            cols = slice(hd * HEAD_W, (hd + 1) * HEAD_W)
            o = _diff_head(q_ref[rows, cols], k_ref[rows, cols].astype(BF16), v_ref[rows, cols].astype(BF16),
                           lam, g_ref[...], 1.0 - lam_init)
            o_ref[rows, cols] = o.astype(BF16)


def _attn_ctx(q, k, v, lq, lk, g, *, seq_len, lam_init):
    n = q.shape[0]
    kern = functools.partial(_attn_ctx_kernel, lam_init=lam_init, seq_len=seq_len)
    tok = pl.BlockSpec((ATTN_CTX_SEQS * seq_len, D), lambda i: (i, 0))
    small = lambda shape: pl.BlockSpec(shape, lambda i: (0, 0))
    return pl.pallas_call(
        kern,
        grid=(n // (ATTN_CTX_SEQS * seq_len),),
        in_specs=[tok, tok, tok, small((2, MAP_W)), small((2, MAP_W)), small((1, HEAD_W))],
        out_specs=tok,
        out_shape=jax.ShapeDtypeStruct((n, D), BF16),
        compiler_params=_params(("parallel",), 32),
        name="attn_context",
    )(q, k, v, lq, lk, g)


def _rope(x, cos, sin_signed):
    lane = lax.broadcasted_iota(jnp.int32, x.shape, 1)
    partner = jnp.where(lane % 32 < 16, pltpu.roll(x, HEAD_W - 16, 1), pltpu.roll(x, 16, 1))
    return x * cos + partner * sin_signed


def _attn_lat_kernel(q_ref, k_ref, v_ref, ck_ref, cv_ref, cos_ref, sin_ref, lq_ref, lk_ref, g_ref,
                     o_ref, kall_ref, vall_ref, *, lam_init, past_len, seq_len, tq):
    qb = pl.program_id(1)

    @pl.when(qb == 0)
    def _():
        kall_ref[0:past_len, :] = ck_ref[...].astype(BF16)
        vall_ref[0:past_len, :] = cv_ref[...].astype(BF16)
        vall_ref[past_len:past_len + seq_len, :] = v_ref[...].astype(BF16)
        for hd in range(N_HEADS):
            cols = slice(hd * HEAD_W, (hd + 1) * HEAD_W)
            kr = _rope(k_ref[:, cols], cos_ref[...], sin_ref[...])
            kall_ref[past_len:past_len + seq_len, cols] = kr.astype(BF16)

    lam = _lam_value(lq_ref, lk_ref, lam_init)
    row0 = pl.multiple_of(qb * tq, tq)
    cos_q = cos_ref[pl.ds(row0, tq), :]
    sin_q = sin_ref[pl.ds(row0, tq), :]
    for hd in range(N_HEADS):
        cols = slice(hd * HEAD_W, (hd + 1) * HEAD_W)
        qh = _rope(q_ref[:, cols], cos_q, sin_q)
        o = _diff_head(qh, kall_ref[:, cols], vall_ref[:, cols], lam, g_ref[...], 1.0 - lam_init)
        o_ref[:, cols] = o.astype(BF16)


def _attn_lat(q, k, v, cache_k, cache_v, cos, sin_signed, lq, lk, g, *, seq_len, lam_init):
    n = q.shape[0]
    n_batch = n // seq_len
    past_len = cache_k.shape[1]
    tq = 256
    n_qb = seq_len // tq
    kern = functools.partial(_attn_lat_kernel, lam_init=lam_init, past_len=past_len,
                             seq_len=seq_len, tq=tq)
    seq = pl.BlockSpec((seq_len, D), lambda b, j: (b, 0))
    cache = pl.BlockSpec((None, past_len, D), lambda b, j: (b, 0, 0))
    small = lambda shape: pl.BlockSpec(shape, lambda b, j: (0, 0))
    qblk = pl.BlockSpec((tq, D), lambda b, j: (b * n_qb + j, 0))
    return pl.pallas_call(
        kern,
        grid=(n_batch, n_qb),
        in_specs=[qblk, seq, seq, cache, cache, small((seq_len, HEAD_W)), small((seq_len, HEAD_W)),
                  small((2, MAP_W)), small((2, MAP_W)), small((1, HEAD_W))],
        out_specs=qblk,
        out_shape=jax.ShapeDtypeStruct((n, D), BF16),
        scratch_shapes=[pltpu.VMEM((past_len + seq_len, D), BF16),
                        pltpu.VMEM((past_len + seq_len, D), BF16)],
        compiler_params=_params(("parallel", "arbitrary"), 48),
        name="attn_latent",
    )(q, k, v, cache_k, cache_v, cos, sin_signed, lq, lk, g)


def _rope_tables(seq_len):
    t = np.arange(seq_len)
    pos = np.stack([t // GRID_W, t % GRID_W], axis=-1).astype(np.float32)
    nf = MAP_W // 4
    inv = (np.float32(ROPE_THETA) ** (-np.arange(nf, dtype=np.float32) / nf)).astype(np.float32)
    lane = np.arange(HEAD_W)
    axis = (lane % MAP_W) // 32
    freq = lane % nf
    sign = np.where(lane % 32 < nf, -1.0, 1.0).astype(np.float32)
    pos_l = jnp.asarray(pos)[:, axis]
    ang = pos_l * jnp.asarray(inv)[freq][None, :]
    return jnp.cos(ang), jnp.sin(ang) * jnp.asarray(sign)[None, :]


def _pool_sequence(x_ref, mod_ref, g_ref, w_ref, sc_ref, o_ref, h_ref, lvl_ref, T):
    R = T + 2 * POOL_PAD
    h = _prenorm(x_ref[...], g_ref[...], mod_ref[0:1, :], mod_ref[1:2, :])
    h_ref[...] = h
    lvl_ref[0:POOL_PAD, :] = jnp.zeros((POOL_PAD, D), F32)
    lvl_ref[POOL_PAD + T:R, :] = jnp.zeros((POOL_PAD, D), F32)
    lvl_ref[POOL_PAD:POOL_PAD + T, :] = h
    t = lax.broadcasted_iota(jnp.int32, (T, 1), 0)
    for gi, win in enumerate(POOL_WINDOWS):
        cols = slice(gi * POOL_GROUP, (gi + 1) * POOL_GROUP)
        half = win // 2
        s = 1
        while s < win:
            lvl_ref[POOL_PAD:R, cols] = lvl_ref[POOL_PAD - s:R - s, cols] + lvl_ref[POOL_PAD:R, cols]
            s *= 2
        wsum = lvl_ref[POOL_PAD + half - 1:POOL_PAD + half - 1 + T, cols]
        cnt = (jnp.minimum(t + half, T) - jnp.maximum(t - half, 0)).astype(F32)
        diff = wsum * (1.0 / cnt) - h_ref[:, cols]
        m = _dot(diff.astype(BF16), w_ref[gi]) * sc_ref[:, cols]
        o_ref[:, cols] = x_ref[:, cols] + mod_ref[2:3, cols] * m


def _pool_kernel(x_ref, mod_ref, g_ref, w_ref, sc_ref, o_ref, h_ref, lvl_ref, *, n_ctx_tiles, ctx_len):
    tile_rows = x_ref.shape[0]
    is_ctx = pl.program_id(0) < n_ctx_tiles

    @pl.when(is_ctx)
    def _():
        padded = ctx_len + 2 * POOL_PAD
        for k in range(tile_rows // ctx_len):
            rows = slice(k * ctx_len, (k + 1) * ctx_len)
            _pool_sequence(x_ref.at[rows], mod_ref, g_ref, w_ref, sc_ref, o_ref.at[rows], h_ref.at[rows],
                           lvl_ref.at[k * padded:(k + 1) * padded], ctx_len)

    @pl.when(jnp.logical_not(is_ctx))
    def _():
        _pool_sequence(x_ref, mod_ref, g_ref, w_ref, sc_ref, o_ref, h_ref, lvl_ref, tile_rows)


def _pool_mixer(x, mod, gains, w_bf16, scale, *, layer, n_ctx, ctx_len, lat_len):
    n = x.shape[0]
    tm = lat_len
    n_ctx_tiles = n_ctx // tm
    kern = functools.partial(_pool_kernel, n_ctx_tiles=n_ctx_tiles, ctx_len=ctx_len)
    tok = pl.BlockSpec((tm, D), lambda i: (i, 0))
    return pl.pallas_call(
        kern,
        grid=(n // tm,),
        in_specs=[
            tok,
            _mod_spec(layer, 1, lambda i: jnp.maximum(i - (n_ctx_tiles - 1), 0)),
            _gain_spec(layer, 1),
            pl.BlockSpec((len(POOL_WINDOWS), POOL_GROUP, POOL_GROUP), lambda i: (0, 0, 0)),
            pl.BlockSpec((1, D), lambda i: (0, 0)),
        ],
        out_specs=tok,
        out_shape=jax.ShapeDtypeStruct((n, D), F32),
        scratch_shapes=[pltpu.VMEM((tm, D), F32),
                        pltpu.VMEM(((tm // ctx_len) * (ctx_len + 2 * POOL_PAD), D), F32)],
        compiler_params=_params(("parallel",), 40),
        name="pool_mixer",
    )(x, mod, gains, w_bf16, scale)


def kernel(x_prompt, x_sample, state_rglru, cache_k_diff, cache_v_diff, c, c_ctx, norm_g, w_mod, b_mod, w_ffn_in, w_ffn_out, a_w_in, a_conv_w, a_conv_b, a_gate_w_a, a_gate_b_a, a_gate_w_x, a_gate_b_x, a_lambda, a_w_out, b_w_qkv, b_lam_q, b_lam_k, b_subln_g, b_w_o, c_w_pool, c_scale, final_norm_g):
    n_ctx_seq, ctx_len, _ = x_prompt.shape
    n_lat_seq, lat_len, _ = x_sample.shape
    n_ctx, n_lat = n_ctx_seq * ctx_len, n_lat_seq * lat_len
    xp = x_prompt.reshape(n_ctx, D)
    xs = x_sample.reshape(n_lat, D)

    cond8 = jnp.concatenate([c_ctx[None], c, jnp.zeros((8 - 1 - n_lat_seq, D), F32)], axis=0)
    mod = _adaln_all(cond8, w_mod, b_mod).reshape(DEPTH, 8, N_MOD // 3, 3, D)
    gains = norm_g.reshape(DEPTH, 3, 1, D)
    final_g = final_norm_g.reshape(1, D)

    assert n_ctx_seq % TM_GROUP == 0 and lat_len == FFN_TM and n_ctx % FFN_TM == 0
    n_ctx_tiles = n_ctx // FFN_TM
    new_states, new_k, new_v = [], [], []
    x = (xp, xs)
    for l in range(DEPTH):
        kind, j = l % 3, l // 3

        def ffn(x, k, half, last):
            return _ffn(x, mod, gains, w_ffn_in, w_ffn_out, final_g, layer=l, k=k, half=half,
                        n_ctx_tiles=n_ctx_tiles, final_norm=last, split_out=last)

        x = ffn(x, 0, 0, False)

        ctx_proj = dict(mod=mod, gains=gains, layer=l, group0=0, rows_per_group=n_ctx, tm=PROJ_TM)
        lat_proj = dict(mod=mod, gains=gains, layer=l, group0=1, rows_per_group=lat_len, tm=PROJ_TM // 2)
        if isinstance(x, tuple):
            ctx_proj.update(x=x[0])
            lat_proj.update(x=x[1])
        else:
            ctx_proj.update(x=x, row0=0, n_rows=n_ctx)
            lat_proj.update(x=x, row0=n_ctx, n_rows=n_lat)
        if kind == 0:
            w_in = a_w_in[j].astype(BF16)
            w_out = a_w_out[j].astype(BF16)
            wg, bg = _gate_weights(a_gate_w_a[j], a_gate_b_a[j], a_gate_w_x[j], a_gate_b_x[j])
            conv_b = a_conv_b[j].reshape(1, D)
            gate_p, xr_p = _prenorm_proj(w_bf16=w_in, **ctx_proj)
            y_p, fin = _rglru_core_tm(gate_p, xr_p, a_conv_w[j], conv_b, wg, bg, a_lambda[j],
                                      seq_len=ctx_len)
            new_states.append(jnp.transpose(fin.reshape(2, n_ctx_seq, D), (1, 0, 2)))
            gate_s, xr_s = _prenorm_proj(w_bf16=w_in, **lat_proj)
            h0 = jnp.transpose(state_rglru[:, j].astype(F32), (1, 0, 2)).reshape(2, n_lat_seq, 1, D)
            y_s, _ = _rglru_core(gate_s, xr_s, a_conv_w[j], conv_b, wg, bg, a_lambda[j], h0,
                                 seq_len=lat_len, n_seq=1)
            x = _outproj_residual(y_p, y_s, w_out, x, mod, layer=l, lat_len=lat_len)
        elif kind == 1:
            lam_init = 0.8 - 0.6 * math.exp(-0.3 * l)
            w_qkv = b_w_qkv[j].astype(BF16)
            w_o = b_w_o[j].astype(BF16)
            sub_g = b_subln_g[j].reshape(1, HEAD_W)
            q_p, k_p, v_p = _prenorm_proj(w_bf16=w_qkv, **ctx_proj)
            o_p = _attn_ctx(q_p, k_p, v_p, b_lam_q[j], b_lam_k[j], sub_g, seq_len=ctx_len,
                            lam_init=lam_init)
            new_k.append(k_p.reshape(n_ctx_seq, ctx_len, N_HEADS, HEAD_W))
            new_v.append(v_p.reshape(n_ctx_seq, ctx_len, N_HEADS, HEAD_W))
            q_s, k_s, v_s = _prenorm_proj(w_bf16=w_qkv, **lat_proj)
            cos, sin_signed = _rope_tables(lat_len)
            past_len = cache_k_diff.shape[2]
            o_s = _attn_lat(q_s, k_s, v_s,
                            cache_k_diff[:, j].reshape(n_lat_seq, past_len, D),
                            cache_v_diff[:, j].reshape(n_lat_seq, past_len, D),
                            cos, sin_signed, b_lam_q[j], b_lam_k[j], sub_g,
                            seq_len=lat_len, lam_init=lam_init)
            x = _outproj_residual(o_p, o_s, w_o, x, mod, layer=l, lat_len=lat_len)
        else:
            assert not isinstance(x, tuple)
            x = _pool_mixer(x, mod, gains, c_w_pool[j].astype(BF16), c_scale[j].reshape(1, D),
                            layer=l, n_ctx=n_ctx, ctx_len=ctx_len, lat_len=lat_len)

        x = ffn(x, 2, 1, l == DEPTH - 1)

    xp, xs = x
    y_prompt = xp.reshape(x_prompt.shape)
    y_sample = xs.reshape(x_sample.shape)
    new_state = jnp.stack(new_states, axis=1).astype(x_prompt.dtype)
    new_cache_k = jnp.stack(new_k, axis=1).astype(x_prompt.dtype)
    new_cache_v = jnp.stack(new_v, axis=1).astype(x_prompt.dtype)
    return (y_prompt, y_sample, new_state, new_cache_k, new_cache_v)
```
